```python
import math
import jax, jax.numpy as jnp
from jax import lax
import numpy as np

D_MODEL = 1024
BATCH = 4
SEQ = 8192
DEPTH = 1

CHUNK = 64
N_HEADS_A = 4
HEAD_DIM_A = 64
WIDTH_A = N_HEADS_A * 2 * HEAD_DIM_A
Q_BLOCK = 128
N_HEADS_B = 8
HEAD_DIM_B = 64
WIDTH_B = N_HEADS_B * HEAD_DIM_B
N_PREV_CHUNKS = 8
REL_CLIP = 128
IN_COLS = 3 * WIDTH_A + 3 * WIDTH_B + 2 * D_MODEL
N_EXPERTS = 32
TOP_K = 4
D_FF = 1024
SWIGLU_ALPHA = 1.702
SWIGLU_LIMIT = 7.0
EXPERT_BLOCK = 128
DEEPNORM_ALPHA = (2 * DEPTH) ** 0.25
DEEPNORM_BETA = (8 * DEPTH) ** -0.25
LN_EPS = 1e-5

kernel_name = "hybrid_diffattn_chunkband_moe_deepnorm"


def layer_norm(x, g, b):
    xf = x.astype(jnp.float32)
    mu = jnp.mean(xf, axis=-1, keepdims=True)
    var = jnp.mean(jnp.square(xf - mu), axis=-1, keepdims=True)
    y = (xf - mu) * lax.rsqrt(var + LN_EPS) * g.astype(jnp.float32) + b.astype(jnp.float32)
    return y.astype(x.dtype)


def rms_norm(x, w):
    xf = x.astype(jnp.float32)
    y = xf * lax.rsqrt(jnp.mean(jnp.square(xf), axis=-1, keepdims=True) + LN_EPS) * w.astype(jnp.float32)
    return y.astype(x.dtype)


def alibi_slopes(n):
    return jnp.asarray(np.array([2.0 ** (-8.0 * (i + 1) / n) for i in range(n)], dtype=np.float32))


def diff_attention(q, k, v, lam, lam_init, subln_w):
    B, S = q.shape[0], q.shape[1]
    scale = HEAD_DIM_A ** -0.5
    q = q.transpose(0, 2, 1, 3, 4) * scale
    k = k.transpose(0, 2, 1, 3, 4)
    v = v.transpose(0, 2, 1, 3)
    n_qb = S // Q_BLOCK
    qb = q.reshape(B, N_HEADS_A, n_qb, Q_BLOCK, 2, HEAD_DIM_A).transpose(2, 0, 1, 3, 4, 5)
    kpos = jnp.arange(S, dtype=jnp.int32)
    kchunk = kpos // CHUNK
    slopes = alibi_slopes(N_HEADS_A)

    def one_block(args):
        q_blk, i = args
        qpos = i * Q_BLOCK + jnp.arange(Q_BLOCK, dtype=jnp.int32)
        s = jnp.einsum('bhqcd,bhkcd->cbhqk', q_blk, k).astype(jnp.float32)
        dist = jnp.abs(qpos[:, None] - kpos[None, :]).astype(jnp.float32)
        bias = -slopes[:, None, None] * dist
        allowed = kchunk[None, :] <= (qpos // CHUNK)[:, None]
        p = jax.nn.softmax(jnp.where(allowed, s + bias, -jnp.inf), axis=-1)
        a = p[0] - lam * p[1]
        return jnp.einsum('bhqk,bhkd->bhqd', a.astype(v.dtype), v)

    o = lax.map(one_block, (qb, jnp.arange(n_qb, dtype=jnp.int32)))
    o = o.transpose(1, 0, 3, 2, 4).reshape(B, S, N_HEADS_A, 2 * HEAD_DIM_A)
    o = rms_norm(o, subln_w) * (1.0 - lam_init)
    return o.reshape(B, S, WIDTH_A)


def chunk_band_attention(q, k, v, rel_bias):
    B, S = q.shape[0], q.shape[1]
    n_chunks = S // CHUNK
    pad_len = N_PREV_CHUNKS * CHUNK
    band = (N_PREV_CHUNKS + 1) * CHUNK
    scale = HEAD_DIM_B ** -0.5
    q = q.transpose(0, 2, 1, 3) * scale
    kp = jnp.pad(k.transpose(0, 2, 1, 3), ((0, 0), (0, 0), (pad_len, 0), (0, 0)))
    vp = jnp.pad(v.transpose(0, 2, 1, 3), ((0, 0), (0, 0), (pad_len, 0), (0, 0)))
    qc = q.reshape(B, N_HEADS_B, n_chunks, CHUNK, HEAD_DIM_B).transpose(2, 0, 1, 3, 4)
    qi = jnp.arange(CHUNK, dtype=jnp.int32)
    kj = jnp.arange(band, dtype=jnp.int32)
    rel = (kj[None, :] - pad_len) - qi[:, None]
    bias = rel_bias.astype(jnp.float32)[:, jnp.clip(rel, -REL_CLIP, REL_CLIP) + REL_CLIP]

    def one_chunk(args):
        q_blk, c = args
        start = c * CHUNK
        kb = lax.dynamic_slice_in_dim(kp, start, band, axis=2)
        vb = lax.dynamic_slice_in_dim(vp, start, band, axis=2)
        s = jnp.einsum('bhqd,bhkd->bhqk', q_blk, kb).astype(jnp.float32) + bias
        valid = (start - pad_len + kj) >= 0
        p = jax.nn.softmax(jnp.where(valid, s, -jnp.inf), axis=-1)
        return jnp.einsum('bhqk,bhkd->bhqd', p.astype(vb.dtype), vb)

    o = lax.map(one_chunk, (qc, jnp.arange(n_chunks, dtype=jnp.int32)))
    return o.transpose(1, 0, 3, 2, 4).reshape(B, S, WIDTH_B)


def gated_mixer(h, w_in, b_gate, lam_q1, lam_k1, lam_q2, lam_k2, subln_w, rel_bias,
                w_branch_a, w_branch_b, w_out, lam_init):
    B, S, D = h.shape
    proj = h @ w_in
    o0 = 0
    qa = proj[..., o0:o0 + WIDTH_A].reshape(B, S, N_HEADS_A, 2, HEAD_DIM_A); o0 += WIDTH_A
    ka = proj[..., o0:o0 + WIDTH_A].reshape(B, S, N_HEADS_A, 2, HEAD_DIM_A); o0 += WIDTH_A
    va = proj[..., o0:o0 + WIDTH_A].reshape(B, S, N_HEADS_A, 2 * HEAD_DIM_A); o0 += WIDTH_A
    qb = proj[..., o0:o0 + WIDTH_B].reshape(B, S, N_HEADS_B, HEAD_DIM_B); o0 += WIDTH_B
    kb = proj[..., o0:o0 + WIDTH_B].reshape(B, S, N_HEADS_B, HEAD_DIM_B); o0 += WIDTH_B
    vb = proj[..., o0:o0 + WIDTH_B].reshape(B, S, N_HEADS_B, HEAD_DIM_B); o0 += WIDTH_B
    gates = jax.nn.sigmoid(proj[..., o0:] + b_gate).reshape(B, S, 2, D)

    lam = (jnp.exp(jnp.sum(lam_q1.astype(jnp.float32) * lam_k1.astype(jnp.float32)))
           - jnp.exp(jnp.sum(lam_q2.astype(jnp.float32) * lam_k2.astype(jnp.float32))) + lam_init)
    out_a = diff_attention(qa, ka, va, lam, lam_init, subln_w)
    out_b = chunk_band_attention(qb, kb, vb, rel_bias)
    merged = gates[:, :, 0] * (out_a @ w_branch_a) + gates[:, :, 1] * (out_b @ w_branch_b)
    return merged @ w_out


def moe_ffn(h, w_router, b_router, w_exp_in, b_exp_in, w_exp_out, b_exp_out):
    B, S, D = h.shape
    t = h.reshape(-1, D)
    n_tok = t.shape[0]
    logits = (t @ w_router + b_router).astype(jnp.float32)
    top_val, top_idx = lax.top_k(logits, TOP_K)
    gate_w = jax.nn.softmax(top_val, axis=-1).astype(h.dtype)

    n_slots = n_tok * TOP_K
    e_flat = top_idx.reshape(-1).astype(jnp.int32)
    tok_flat = jnp.arange(n_slots, dtype=jnp.int32) // TOP_K
    order = jnp.argsort(e_flat)
    e_sorted = e_flat[order]
    counts = jnp.bincount(e_flat, length=N_EXPERTS).astype(jnp.int32)
    padded = (counts + EXPERT_BLOCK - 1) // EXPERT_BLOCK * EXPERT_BLOCK
    start = jnp.cumsum(counts) - counts
    pend = jnp.cumsum(padded)
    pstart = pend - padded
    rank = jnp.arange(n_slots, dtype=jnp.int32) - start[e_sorted]
    dest_sorted = pstart[e_sorted] + rank
    buf_len = (-(-n_slots // EXPERT_BLOCK)) * EXPERT_BLOCK + N_EXPERTS * EXPERT_BLOCK
    n_blocks = buf_len // EXPERT_BLOCK
    buf_tok = jnp.full((buf_len,), n_tok, jnp.int32).at[dest_sorted].set(tok_flat[order])
    dest = jnp.zeros((n_slots,), jnp.int32).at[order].set(dest_sorted)
    block_start = jnp.arange(n_blocks, dtype=jnp.int32) * EXPERT_BLOCK
    block_exp = jnp.minimum(jnp.searchsorted(pend, block_start, side='right'), N_EXPERTS - 1).astype(jnp.int32)
    t_pad = jnp.concatenate([t, jnp.zeros((1, D), t.dtype)], axis=0)

    def expert_block(args):
        tok_idx, e = args
        xb = t_pad[tok_idx]
        hb = xb @ w_exp_in[e] + b_exp_in[e]
        g = jnp.minimum(hb[:, :D_FF], SWIGLU_LIMIT)
        u = jnp.clip(hb[:, D_FF:], -SWIGLU_LIMIT, SWIGLU_LIMIT)
        a = g * jax.nn.sigmoid(SWIGLU_ALPHA * g) * (u + 1.0)
        return a @ w_exp_out[e] + b_exp_out[e]

    y_buf = lax.map(expert_block, (buf_tok.reshape(n_blocks, EXPERT_BLOCK), block_exp)).reshape(buf_len, D)
    y_slot = y_buf[dest].reshape(n_tok, TOP_K, D)
    y = jnp.einsum('nk,nkd->nd', gate_w, y_slot)
    return y.reshape(B, S, D)


def setup_inputs(seed: int = 0) -> dict:
    key = jax.random.key(seed)
    ks = jax.random.split(key, 24)
    L, D, E, F = DEPTH, D_MODEL, N_EXPERTS, D_FF
    nrm = lambda k, shape: jax.random.normal(k, shape, jnp.float32)
    col_scale = np.ones((IN_COLS,), np.float32)
    col_scale[2 * WIDTH_A:3 * WIDTH_A] = DEEPNORM_BETA
    col_scale[3 * WIDTH_A + 2 * WIDTH_B:3 * WIDTH_A + 3 * WIDTH_B] = DEEPNORM_BETA
    return {
        "x": nrm(ks[0], (BATCH, SEQ, D)),
        "ln_in_g": 1.0 + 0.05 * nrm(ks[1], (D,)),
        "ln_in_b": 0.02 * nrm(ks[2], (D,)),
        "w_in": nrm(ks[3], (L, D, IN_COLS)) * (D ** -0.5) * jnp.asarray(col_scale),
        "b_gate": 0.02 * nrm(ks[4], (L, 2 * D)),
        "lambda_q1": 0.1 * nrm(ks[5], (L, HEAD_DIM_A)),
        "lambda_k1": 0.1 * nrm(ks[6], (L, HEAD_DIM_A)),
        "lambda_q2": 0.1 * nrm(ks[7], (L, HEAD_DIM_A)),
        "lambda_k2": 0.1 * nrm(ks[8], (L, HEAD_DIM_A)),
        "subln_w": 1.0 + 0.05 * nrm(ks[9], (L, 2 * HEAD_DIM_A)),
        "rel_bias": 0.1 * nrm(ks[10], (L, N_HEADS_B, 2 * REL_CLIP + 1)),
        "w_branch_a": nrm(ks[11], (L, WIDTH_A, D)) * (WIDTH_A ** -0.5) * DEEPNORM_BETA,
        "w_branch_b": nrm(ks[12], (L, WIDTH_B, D)) * (WIDTH_B ** -0.5) * DEEPNORM_BETA,
        "w_out": nrm(ks[13], (L, D, D)) * (D ** -0.5) * DEEPNORM_BETA,
        "ln1_g": 1.0 + 0.05 * nrm(ks[14], (L, D)),
        "ln1_b": 0.02 * nrm(ks[15], (L, D)),
        "w_router": nrm(ks[16], (L, D, E)) * (D ** -0.5),
        "b_router": 0.01 * nrm(ks[17], (L, E)),
        "w_exp_in": nrm(ks[18], (L, E, D, 2 * F)) * (D ** -0.5) * DEEPNORM_BETA,
        "b_exp_in": 0.01 * nrm(ks[19], (L, E, 2 * F)),
        "w_exp_out": nrm(ks[20], (L, E, F, D)) * (F ** -0.5) * DEEPNORM_BETA,
        "b_exp_out": 0.01 * nrm(ks[21], (L, E, D)),
        "ln2_g": 1.0 + 0.05 * nrm(ks[22], (L, D)),
        "ln2_b": 0.02 * nrm(ks[23], (L, D)),
    }


def reference(x, ln_in_g, ln_in_b, w_in, b_gate, lambda_q1, lambda_k1, lambda_q2, lambda_k2,
              subln_w, rel_bias, w_branch_a, w_branch_b, w_out, ln1_g, ln1_b,
              w_router, b_router, w_exp_in, b_exp_in, w_exp_out, b_exp_out, ln2_g, ln2_b):
    h = layer_norm(x, ln_in_g, ln_in_b)
    for l in range(DEPTH):
        lam_init = 0.8 - 0.6 * math.exp(-0.3 * l)
        m = gated_mixer(h, w_in[l], b_gate[l], lambda_q1[l], lambda_k1[l], lambda_q2[l], lambda_k2[l],
                        subln_w[l], rel_bias[l], w_branch_a[l], w_branch_b[l], w_out[l], lam_init)
        h = layer_norm(DEEPNORM_ALPHA * h + m, ln1_g[l], ln1_b[l])
        f = moe_ffn(h, w_router[l], b_router[l], w_exp_in[l], b_exp_in[l], w_exp_out[l], b_exp_out[l])
        h = layer_norm(DEEPNORM_ALPHA * h + f, ln2_g[l], ln2_b[l])
    return h
```

```python
import functools
import math

import jax
import jax.numpy as jnp
import numpy as np
from jax import lax
from jax.experimental import pallas as pl
from jax.experimental.pallas import tpu as pltpu

F32 = jnp.float32
BF16 = jnp.bfloat16

D_MODEL = 1024
CHUNK = 64
N_HEADS_A = 4
HEAD_DIM_A = 64
WIDTH_A = 512
N_HEADS_B = 8
HEAD_DIM_B = 64
WIDTH_B = 512
N_PREV_CHUNKS = 8
REL_CLIP = 128
N_EXPERTS = 32
TOP_K = 4
D_FF = 1024
SWIGLU_ALPHA = 1.702
SWIGLU_LIMIT = 7.0
DEEPNORM_ALPHA = 2.0 ** 0.25
LN_EPS = 1e-5
LAM_INIT = 0.8 - 0.6 * math.exp(-0.3 * 0)

LANES = 128
N_SLABS = (3 * WIDTH_A + 3 * WIDTH_B) // LANES
GATE_COLS = 2 * D_MODEL
IN_COLS = 3 * WIDTH_A + 3 * WIDTH_B + GATE_COLS
NEG = -1e30

TM_PROJ = 512
PROJ_CHUNK = 512
TQ = 256
TK = 256
BAND = 3 * TK
TM_MERGE = 256
ROWS_MOE = 256
TM_COMB = 256
VMEM_LIMIT = 56 * 1024 * 1024


def _layer_norm(x, g, b):
    mu = jnp.mean(x, axis=-1, keepdims=True)
    xc = x - mu
    var = jnp.mean(xc * xc, axis=-1, keepdims=True)
    return xc * lax.rsqrt(var + LN_EPS) * g + b


def _params(sem):
    return pltpu.CompilerParams(dimension_semantics=sem, vmem_limit_bytes=VMEM_LIMIT)


def _ln_proj_kernel(x_ref, g_ref, b_ref, w_ref, bg_ref, h_ref, qkv_ref, gates_ref):
    h = _layer_norm(x_ref[...], g_ref[...], b_ref[...])
    h_ref[...] = h
    hb = h.astype(BF16)
    n_qkv_chunks = (N_SLABS * LANES) // PROJ_CHUNK
    slabs_per_chunk = PROJ_CHUNK // LANES
    q_scale = HEAD_DIM_A ** -0.5
    for c in range(n_qkv_chunks):
        r = jnp.dot(hb, w_ref[:, c * PROJ_CHUNK:(c + 1) * PROJ_CHUNK], preferred_element_type=F32)
        first = c * slabs_per_chunk
        is_q = (first < WIDTH_A // LANES) or (3 * WIDTH_A // LANES <= first < (3 * WIDTH_A + WIDTH_B) // LANES)
        if is_q:
            r = r * q_scale
        for s in range(slabs_per_chunk):
            qkv_ref[first + s] = r[:, s * LANES:(s + 1) * LANES].astype(BF16)
    g0 = N_SLABS * LANES
    for c in range(GATE_COLS // PROJ_CHUNK):
        r = jnp.dot(hb, w_ref[:, g0 + c * PROJ_CHUNK:g0 + (c + 1) * PROJ_CHUNK], preferred_element_type=F32)
        r = r + bg_ref[:, c * PROJ_CHUNK:(c + 1) * PROJ_CHUNK]
        gates_ref[:, c * PROJ_CHUNK:(c + 1) * PROJ_CHUNK] = jax.nn.sigmoid(r).astype(BF16)


def _ln_proj(x2d, g, b, w_bf16, b_gate):
    n = x2d.shape[0]
    tm = min(TM_PROJ, n)
    const = lambda i: (0, 0)
    return pl.pallas_call(
        _ln_proj_kernel,
        grid=(n // tm,),
        in_specs=[
            pl.BlockSpec((tm, D_MODEL), lambda i: (i, 0)),
            pl.BlockSpec((1, D_MODEL), const),
            pl.BlockSpec((1, D_MODEL), const),
            pl.BlockSpec((D_MODEL, IN_COLS), const, pipeline_mode=pl.Buffered(1)),
            pl.BlockSpec((1, GATE_COLS), const),
        ],
        out_specs=[
            pl.BlockSpec((tm, D_MODEL), lambda i: (i, 0)),
            pl.BlockSpec((N_SLABS, tm, LANES), lambda i: (0, i, 0)),
            pl.BlockSpec((tm, GATE_COLS), lambda i: (i, 0)),
        ],
        out_shape=[
            jax.ShapeDtypeStruct((n, D_MODEL), F32),
            jax.ShapeDtypeStruct((N_SLABS, n, LANES), BF16),
            jax.ShapeDtypeStruct((n, GATE_COLS), BF16),
        ],
        compiler_params=_params(("parallel",)),
        name="ln_proj",
    )(x2d, g, b, w_bf16, b_gate)


def _stack_halves(q):
    lane = lax.broadcasted_iota(jnp.int32, q.shape, 1)
    zero = jnp.zeros_like(q)
    return jnp.concatenate([jnp.where(lane < 64, q, zero), jnp.where(lane >= 64, q, zero)], axis=0)


def _dot_nt(a, b):
    return lax.dot_general(a, b, (((1,), (1,)), ((), ())), preferred_element_type=F32)


def _diff_attn_kernel(slopes_ref, lq1_ref, lk1_ref, lq2_ref, lk2_ref, subln_ref, q_ref, k_ref, v_ref, o_ref):
    hd = pl.program_id(1)
    i = pl.program_id(2)
    slope = slopes_ref[hd]
    q2 = _stack_halves(q_ref[...])
    col = lax.broadcasted_iota(jnp.int32, (1, TK), 1).astype(F32)
    col_bias = slope * col

    def kv_step(j, carry, tile):
        m_prev, l_prev, acc = carry
        k = k_ref[pl.ds(pl.multiple_of(j * TK, TK), TK), :]
        v = v_ref[pl.ds(pl.multiple_of(j * TK, TK), TK), :]
        s = _dot_nt(q2, k) + tile
        c = slope * ((j - i) * TK).astype(F32)
        m_new = jnp.maximum(m_prev, jnp.max(s, axis=-1, keepdims=True) + c)
        p = jnp.exp(s - (m_new - c))
        alpha = jnp.exp(m_prev - m_new)
        l_new = alpha * l_prev + jnp.sum(p, axis=-1, keepdims=True)
        acc = alpha * acc + jnp.dot(p.astype(BF16), v, preferred_element_type=F32)
        return m_new, l_new, acc

    init = (jnp.full((2 * TQ, 1), NEG, F32), jnp.zeros((2 * TQ, 1), F32), jnp.zeros((2 * TQ, LANES), F32))
    carry = lax.fori_loop(0, i, lambda j, cr: kv_step(j, cr, col_bias), init)

    ri = lax.broadcasted_iota(jnp.int32, (TQ, TK), 0)
    ci = lax.broadcasted_iota(jnp.int32, (TQ, TK), 1)
    dist = jnp.abs(ri - ci)
    diag = jnp.where(ci // CHUNK <= ri // CHUNK, slope * (ri - dist).astype(F32), NEG)
    _, l_fin, acc = kv_step(i, carry, jnp.concatenate([diag, diag], axis=0))

    lam = (jnp.exp(jnp.sum(lq1_ref[...] * lk1_ref[...], axis=-1, keepdims=True))
           - jnp.exp(jnp.sum(lq2_ref[...] * lk2_ref[...], axis=-1, keepdims=True)) + LAM_INIT)
    o = acc / l_fin
    o = o[:TQ] - lam * o[TQ:]
    o = o * lax.rsqrt(jnp.mean(o * o, axis=-1, keepdims=True) + LN_EPS) * subln_ref[...]
    o_ref[...] = (o * (1.0 - LAM_INIT)).astype(BF16)


def _diff_attention(qkv, slopes, lq1, lk1, lq2, lk2, subln, batch, seq):
    nq = seq // TQ
    n = batch * seq
    vec = lambda w: pl.BlockSpec((1, w), lambda b, h, i: (0, 0))
    return pl.pallas_call(
        _diff_attn_kernel,
        grid=(batch, N_HEADS_A, nq),
        in_specs=[
            pl.BlockSpec(memory_space=pltpu.SMEM),
            vec(HEAD_DIM_A), vec(HEAD_DIM_A), vec(HEAD_DIM_A), vec(HEAD_DIM_A), vec(2 * HEAD_DIM_A),
            pl.BlockSpec((None, TQ, LANES), lambda b, h, i: (h, b * nq + i, 0)),
            pl.BlockSpec((None, seq, LANES), lambda b, h, i: (N_HEADS_A + h, b, 0)),
            pl.BlockSpec((None, seq, LANES), lambda b, h, i: (2 * N_HEADS_A + h, b, 0)),
        ],
        out_specs=pl.BlockSpec((TQ, LANES), lambda b, h, i: (b * nq + i, h)),
        out_shape=jax.ShapeDtypeStruct((n, WIDTH_A), BF16),
        compiler_params=_params(("parallel", "parallel", "arbitrary")),
        name="diff_attn",
    )(slopes, lq1, lk1, lq2, lk2, subln, qkv, qkv, qkv)


def _band_attn_kernel(bias_ref, q_ref, k_ref, v_ref, o_ref):
    i = pl.program_id(2)
    q2 = _stack_halves(q_ref[...])
    n_sub = BAND // TK
    scores, vals = [], []
    for jj in range(n_sub):
        start = i * TQ - N_PREV_CHUNKS * CHUNK + jj * TK
        before_start = jnp.where(start < 0, NEG, 0.0).astype(F32)
        rows = pl.ds(pl.multiple_of(jnp.maximum(start, 0), TK), TK)
        s = _dot_nt(q2, k_ref[rows, :]) + (bias_ref[:, jj * TK:(jj + 1) * TK] + before_start)
        scores.append(s)
        vals.append(v_ref[rows, :])
    m = functools.reduce(jnp.maximum, [jnp.max(s, axis=-1, keepdims=True) for s in scores])
    l = jnp.zeros((2 * TQ, 1), F32)
    acc = jnp.zeros((2 * TQ, LANES), F32)
    for s, v in zip(scores, vals):
        p = jnp.exp(s - m)
        l = l + jnp.sum(p, axis=-1, keepdims=True)
        acc = acc + jnp.dot(p.astype(BF16), v, preferred_element_type=F32)
    o = acc / l
    lane = lax.broadcasted_iota(jnp.int32, (TQ, LANES), 1)
    o_ref[...] = jnp.where(lane < 64, o[:TQ], o[TQ:]).astype(BF16)


def _band_bias(rel_bias):
    qi = np.arange(TQ)[:, None]
    kj = np.arange(BAND)[None, :] - N_PREV_CHUNKS * CHUNK
    rel = np.clip(kj - qi, -REL_CLIP, REL_CLIP) + REL_CLIP
    qc = qi // CHUNK
    kc = np.floor_divide(kj, CHUNK)
    allowed = (kc <= qc) & (kc >= qc - N_PREV_CHUNKS)
    tile = jnp.where(jnp.asarray(allowed)[None], rel_bias.astype(F32)[:, jnp.asarray(rel)], NEG)
    return tile.reshape(N_HEADS_B // 2, 2 * TQ, BAND)


def _band_attention(qkv, bias, batch, seq):
    nq = seq // TQ
    n = batch * seq
    base = 3 * WIDTH_A // LANES
    pairs = N_HEADS_B // 2
    return pl.pallas_call(
        _band_attn_kernel,
        grid=(batch, pairs, nq),
        in_specs=[
            pl.BlockSpec((None, 2 * TQ, BAND), lambda b, p, i: (p, 0, 0)),
            pl.BlockSpec((None, TQ, LANES), lambda b, p, i: (base + p, b * nq + i, 0)),
            pl.BlockSpec((None, seq, LANES), lambda b, p, i: (base + pairs + p, b, 0)),
            pl.BlockSpec((None, seq, LANES), lambda b, p, i: (base + 2 * pairs + p, b, 0)),
        ],
        out_specs=pl.BlockSpec((TQ, LANES), lambda b, p, i: (b * nq + i, p)),
        out_shape=jax.ShapeDtypeStruct((n, WIDTH_B), BF16),
        compiler_params=_params(("parallel", "parallel", "arbitrary")),
        name="band_attn",
    )(bias, qkv, qkv, qkv)


def _merge_kernel(oa_ref, ob_ref, gates_ref, h_ref, pa_ref, pb_ref, wo_ref, g_ref, b_ref, wr_ref, br_ref,
                  h1_ref, logits_ref):
    ma = jnp.dot(oa_ref[...], pa_ref[...], preferred_element_type=F32)
    mb = jnp.dot(ob_ref[...], pb_ref[...], preferred_element_type=F32)
    merged = gates_ref[:, :D_MODEL].astype(F32) * ma + gates_ref[:, D_MODEL:].astype(F32) * mb
    m = jnp.dot(merged.astype(BF16), wo_ref[...], preferred_element_type=F32)
    h1 = _layer_norm(DEEPNORM_ALPHA * h_ref[...] + m, g_ref[...], b_ref[...])
    h1_ref[...] = h1
    logits_ref[...] = jnp.dot(h1, wr_ref[...], preferred_element_type=F32,
                              precision=lax.Precision.HIGHEST) + br_ref[...]


def _merge(oa, ob, gates, h, pa, pb, wo, g, b, wr, br):
    n = h.shape[0]
    tm = min(TM_MERGE, n)
    row = lambda w: pl.BlockSpec((tm, w), lambda i: (i, 0))
    full = lambda r, c: pl.BlockSpec((r, c), lambda i: (0, 0))
    return pl.pallas_call(
        _merge_kernel,
        grid=(n // tm,),
        in_specs=[row(WIDTH_A), row(WIDTH_B), row(GATE_COLS), row(D_MODEL),
                  full(WIDTH_A, D_MODEL), full(WIDTH_B, D_MODEL), full(D_MODEL, D_MODEL),
                  full(1, D_MODEL), full(1, D_MODEL), full(D_MODEL, N_EXPERTS), full(1, N_EXPERTS)],
        out_specs=[row(D_MODEL), row(N_EXPERTS)],
        out_shape=[jax.ShapeDtypeStruct((n, D_MODEL), F32), jax.ShapeDtypeStruct((n, N_EXPERTS), F32)],
        compiler_params=_params(("parallel",)),
        name="merge_ln1_router",
    )(oa, ob, gates, h, pa, pb, wo, g, b, wr, br)


def _row_gather(src_hbm, idx_ref, dst_ref, sem, n_rows):
    def start(r, _):
        pltpu.make_async_copy(src_hbm.at[pl.ds(idx_ref[0, 0, r], 1), :], dst_ref.at[pl.ds(r, 1), :], sem).start()
        return 0
    lax.fori_loop(0, n_rows, start, 0, unroll=8)


def _moe_kernel(bexp_ref, nused_ref, tok_cur_ref, tok_nxt_ref, h1_hbm, wi_ref, bi_ref, wo_ref, bo_ref,
                y_ref, xbuf, sems):
    i = pl.program_id(0)
    slot = i % 2
    n_used = nused_ref[0]

    @pl.when(i == 0)
    def _():
        _row_gather(h1_hbm, tok_cur_ref, xbuf.at[0], sems.at[0], ROWS_MOE)

    @pl.when(i + 1 < n_used)
    def _():
        _row_gather(h1_hbm, tok_nxt_ref, xbuf.at[1 - slot], sems.at[1 - slot], ROWS_MOE)

    @pl.when(i < n_used)
    def _():
        pltpu.make_async_copy(h1_hbm.at[pl.ds(0, ROWS_MOE), :], xbuf.at[slot], sems.at[slot]).wait()
        x = xbuf[slot].astype(BF16)
        hb = jnp.dot(x, wi_ref[...], preferred_element_type=F32) + bi_ref[...]
        g = jnp.minimum(hb[:, :D_FF], SWIGLU_LIMIT)
        u = jnp.clip(hb[:, D_FF:], -SWIGLU_LIMIT, SWIGLU_LIMIT)
        a = g * jax.nn.sigmoid(SWIGLU_ALPHA * g) * (u + 1.0)
        y_ref[...] = jnp.dot(a.astype(BF16), wo_ref[...], preferred_element_type=F32) + bo_ref[...]

    @pl.when(i >= n_used)
    def _():
        y_ref[...] = jnp.zeros_like(y_ref)


def _moe(block_exp, n_used, buf_tok, h1, wi, bi, wo, bo):
    n_blocks = block_exp.shape[0]
    tok3 = buf_tok.reshape(n_blocks, 1, ROWS_MOE)
    grid_spec = pltpu.PrefetchScalarGridSpec(
        num_scalar_prefetch=2,
        grid=(n_blocks,),
        in_specs=[
            pl.BlockSpec((1, 1, ROWS_MOE), lambda i, be, nu: (i, 0, 0), memory_space=pltpu.SMEM),
            pl.BlockSpec((1, 1, ROWS_MOE), lambda i, be, nu: (jnp.minimum(i + 1, n_blocks - 1), 0, 0),
                         memory_space=pltpu.SMEM),
            pl.BlockSpec(memory_space=pl.ANY),
            pl.BlockSpec((None, D_MODEL, 2 * D_FF), lambda i, be, nu: (be[i], 0, 0)),
            pl.BlockSpec((None, 1, 2 * D_FF), lambda i, be, nu: (be[i], 0, 0)),
            pl.BlockSpec((None, D_FF, D_MODEL), lambda i, be, nu: (be[i], 0, 0)),
            pl.BlockSpec((None, 1, D_MODEL), lambda i, be, nu: (be[i], 0, 0)),
        ],
        out_specs=pl.BlockSpec((ROWS_MOE, D_MODEL), lambda i, be, nu: (i, 0)),
        scratch_shapes=[pltpu.VMEM((2, ROWS_MOE, D_MODEL), F32), pltpu.SemaphoreType.DMA((2,))],
    )
    return pl.pallas_call(
        _moe_kernel,
        grid_spec=grid_spec,
        out_shape=jax.ShapeDtypeStruct((n_blocks * ROWS_MOE, D_MODEL), F32),
        compiler_params=_params(("arbitrary",)),
        name="moe_experts",
    )(block_exp, n_used, tok3, tok3, h1, wi, bi, wo, bo)


def _combine_kernel(dest_cur_ref, dest_nxt_ref, gw_ref, h1_ref, g_ref, b_ref, y_hbm, o_ref, ybuf, sems):
    i = pl.program_id(0)
    n_steps = pl.num_programs(0)
    slot = i % 2
    rows = TOP_K * TM_COMB

    @pl.when(i == 0)
    def _():
        _row_gather(y_hbm, dest_cur_ref, ybuf.at[0], sems.at[0], rows)

    @pl.when(i + 1 < n_steps)
    def _():
        _row_gather(y_hbm, dest_nxt_ref, ybuf.at[1 - slot], sems.at[1 - slot], rows)

    pltpu.make_async_copy(y_hbm.at[pl.ds(0, rows), :], ybuf.at[slot], sems.at[slot]).wait()
    gw = gw_ref[...]
    f = jnp.zeros((TM_COMB, D_MODEL), F32)
    for k in range(TOP_K):
        f = f + gw[:, k:k + 1] * ybuf[slot, k * TM_COMB:(k + 1) * TM_COMB, :]
    o_ref[...] = _layer_norm(DEEPNORM_ALPHA * h1_ref[...] + f, g_ref[...], b_ref[...])


def _combine(dest_km, gate_w, h1, g, b, y_buf):
    n = h1.shape[0]
    n_steps = n // TM_COMB
    rows = TOP_K * TM_COMB
    return pl.pallas_call(
        _combine_kernel,
        grid=(n_steps,),
        in_specs=[
            pl.BlockSpec((1, 1, rows), lambda i: (i, 0, 0), memory_space=pltpu.SMEM),
            pl.BlockSpec((1, 1, rows), lambda i: (jnp.minimum(i + 1, n_steps - 1), 0, 0), memory_space=pltpu.SMEM),
            pl.BlockSpec((TM_COMB, TOP_K), lambda i: (i, 0)),
            pl.BlockSpec((TM_COMB, D_MODEL), lambda i: (i, 0)),
            pl.BlockSpec((1, D_MODEL), lambda i: (0, 0)),
            pl.BlockSpec((1, D_MODEL), lambda i: (0, 0)),
            pl.BlockSpec(memory_space=pl.ANY),
        ],
        out_specs=pl.BlockSpec((TM_COMB, D_MODEL), lambda i: (i, 0)),
        out_shape=jax.ShapeDtypeStruct((n, D_MODEL), F32),
        scratch_shapes=[pltpu.VMEM((2, rows, D_MODEL), F32), pltpu.SemaphoreType.DMA((2,))],
        compiler_params=_params(("arbitrary",)),
        name="combine_ln2",
    )(dest_km, dest_km, gate_w, h1, g, b, y_buf)


def _route(logits):
    n_tok = logits.shape[0]
    top_val, top_idx = lax.top_k(logits, TOP_K)
    gate_w = jax.nn.softmax(top_val, axis=-1)
    n_slots = n_tok * TOP_K
    e_flat = top_idx.reshape(-1).astype(jnp.int32)
    order = jnp.argsort(e_flat)
    e_sorted = e_flat[order]
    counts = jnp.bincount(e_flat, length=N_EXPERTS).astype(jnp.int32)
    padded = (counts + ROWS_MOE - 1) // ROWS_MOE * ROWS_MOE
    start = jnp.cumsum(counts) - counts
    pend = jnp.cumsum(padded)
    pstart = pend - padded
    rank = jnp.arange(n_slots, dtype=jnp.int32) - start[e_sorted]
    dest_sorted = pstart[e_sorted] + rank
    n_blocks = n_slots // ROWS_MOE + N_EXPERTS
    buf_len = n_blocks * ROWS_MOE
    buf_tok = jnp.zeros((buf_len,), jnp.int32).at[dest_sorted].set((order // TOP_K).astype(jnp.int32))
    dest = jnp.zeros((n_slots,), jnp.int32).at[order].set(dest_sorted)
    block_start = jnp.arange(n_blocks, dtype=jnp.int32) * ROWS_MOE
    block_exp = jnp.minimum(jnp.searchsorted(pend, block_start, side='right'), N_EXPERTS - 1).astype(jnp.int32)
    n_used = (pend[-1] // ROWS_MOE).astype(jnp.int32).reshape(1)
    return gate_w, dest.reshape(n_tok, TOP_K), buf_tok, block_exp, n_used


def kernel(x, ln_in_g, ln_in_b, w_in, b_gate, lambda_q1, lambda_k1, lambda_q2, lambda_k2, subln_w, rel_bias,
           w_branch_a, w_branch_b, w_out, ln1_g, ln1_b, w_router, b_router, w_exp_in, b_exp_in, w_exp_out,
           b_exp_out, ln2_g, ln2_b):
    batch, seq, d = x.shape
    n = batch * seq
    row = lambda a: a.reshape(1, -1).astype(F32)
    l = 0
    h, qkv, gates = _ln_proj(x.reshape(n, d), row(ln_in_g), row(ln_in_b), w_in[l].astype(BF16), row(b_gate[l]))
    slopes = jnp.asarray([2.0 ** (-8.0 * (i + 1) / N_HEADS_A) for i in range(N_HEADS_A)], F32)
    out_a = _diff_attention(qkv, slopes, row(lambda_q1[l]), row(lambda_k1[l]), row(lambda_q2[l]),
                            row(lambda_k2[l]), row(subln_w[l]), batch, seq)
    out_b = _band_attention(qkv, _band_bias(rel_bias[l]), batch, seq)
    h1, logits = _merge(out_a, out_b, gates, h, w_branch_a[l].astype(BF16), w_branch_b[l].astype(BF16),
                        w_out[l].astype(BF16), row(ln1_g[l]), row(ln1_b[l]), w_router[l].astype(F32),
                        row(b_router[l]))
    gate_w, dest, buf_tok, block_exp, n_used = _route(logits)
    y_buf = _moe(block_exp, n_used, buf_tok, h1, w_exp_in[l].astype(BF16), b_exp_in[l].reshape(N_EXPERTS, 1, -1),
                 w_exp_out[l].astype(BF16), b_exp_out[l].reshape(N_EXPERTS, 1, -1))
    n_steps = n // TM_COMB
    dest_km = dest.reshape(n_steps, TM_COMB, TOP_K).transpose(0, 2, 1).reshape(n_steps, 1, TOP_K * TM_COMB)
    out = _combine(dest_km, gate_w, h1, row(ln2_g[l]), row(ln2_b[l]), y_buf)
    return out.reshape(batch, seq, d)
```

```python
import functools
import math

import jax
import jax.numpy as jnp
import numpy as np
from jax import lax
from jax.experimental import pallas as pl
from jax.experimental.pallas import tpu as pltpu

F32 = jnp.float32
BF16 = jnp.bfloat16

D_MODEL = 1024
CHUNK = 64
N_HEADS_A = 4
HEAD_DIM_A = 64
WIDTH_A = 512
N_HEADS_B = 8
HEAD_DIM_B = 64
WIDTH_B = 512
N_PREV_CHUNKS = 8
REL_CLIP = 128
N_EXPERTS = 32
TOP_K = 4
D_FF = 1024
SWIGLU_ALPHA = 1.702
SWIGLU_LIMIT = 7.0
DEEPNORM_ALPHA = 2.0 ** 0.25
LN_EPS = 1e-5
LAM_INIT = 0.8 - 0.6 * math.exp(-0.3 * 0)

LANES = 128
N_SLABS = (3 * WIDTH_A + 3 * WIDTH_B) // LANES
GATE_COLS = 2 * D_MODEL
IN_COLS = 3 * WIDTH_A + 3 * WIDTH_B + GATE_COLS
LOG2E = math.log2(math.e)
NEG = -1e30

TM_PROJ = 512
PROJ_CHUNK = 512
TQ = 256
TK = 256
KV_GROUP = 4
BAND = 3 * TK
TM_MERGE = 256
TM_DISPATCH = 512
ROWS_MOE = 256
TM_COMB = 256
VMEM_LIMIT = 56 * 1024 * 1024


def _layer_norm(x, g, b):
    mu = jnp.mean(x, axis=-1, keepdims=True)
    xc = x - mu
    var = jnp.mean(xc * xc, axis=-1, keepdims=True)
    return xc * lax.rsqrt(var + LN_EPS) * g + b


def _params(sem):
    return pltpu.CompilerParams(dimension_semantics=sem, vmem_limit_bytes=VMEM_LIMIT)


def _ln_proj_kernel(x_ref, g_ref, b_ref, w_ref, bg_ref, h_ref, qkv_ref, gates_ref):
    h = _layer_norm(x_ref[...], g_ref[...], b_ref[...])
    h_ref[...] = h
    hb = h.astype(BF16)
    n_qkv_chunks = (N_SLABS * LANES) // PROJ_CHUNK
    slabs_per_chunk = PROJ_CHUNK // LANES
    q_scale = HEAD_DIM_A ** -0.5 * LOG2E
    for c in range(n_qkv_chunks):
        r = jnp.dot(hb, w_ref[:, c * PROJ_CHUNK:(c + 1) * PROJ_CHUNK], preferred_element_type=F32)
        first = c * slabs_per_chunk
        is_q = (first < WIDTH_A // LANES) or (3 * WIDTH_A // LANES <= first < (3 * WIDTH_A + WIDTH_B) // LANES)
        if is_q:
            r = r * q_scale
        for s in range(slabs_per_chunk):
            qkv_ref[first + s] = r[:, s * LANES:(s + 1) * LANES].astype(BF16)
    g0 = N_SLABS * LANES
    for c in range(GATE_COLS // PROJ_CHUNK):
        r = jnp.dot(hb, w_ref[:, g0 + c * PROJ_CHUNK:g0 + (c + 1) * PROJ_CHUNK], preferred_element_type=F32)
        r = r + bg_ref[:, c * PROJ_CHUNK:(c + 1) * PROJ_CHUNK]
        gates_ref[:, c * PROJ_CHUNK:(c + 1) * PROJ_CHUNK] = jax.nn.sigmoid(r).astype(BF16)


def _ln_proj(x2d, g, b, w_bf16, b_gate):
    n = x2d.shape[0]
    tm = min(TM_PROJ, n)
    const = lambda i: (0, 0)
    return pl.pallas_call(
        _ln_proj_kernel,
        grid=(n // tm,),
        in_specs=[
            pl.BlockSpec((tm, D_MODEL), lambda i: (i, 0)),
            pl.BlockSpec((1, D_MODEL), const),
            pl.BlockSpec((1, D_MODEL), const),
            pl.BlockSpec((D_MODEL, IN_COLS), const, pipeline_mode=pl.Buffered(1)),
            pl.BlockSpec((1, GATE_COLS), const),
        ],
        out_specs=[
            pl.BlockSpec((tm, D_MODEL), lambda i: (i, 0)),
            pl.BlockSpec((N_SLABS, tm, LANES), lambda i: (0, i, 0)),
            pl.BlockSpec((tm, GATE_COLS), lambda i: (i, 0)),
        ],
        out_shape=[
            jax.ShapeDtypeStruct((n, D_MODEL), F32),
            jax.ShapeDtypeStruct((N_SLABS, n, LANES), BF16),
            jax.ShapeDtypeStruct((n, GATE_COLS), BF16),
        ],
        compiler_params=_params(("parallel",)),
        name="ln_proj",
    )(x2d, g, b, w_bf16, b_gate)


def _stack_halves(q):
    lane = lax.broadcasted_iota(jnp.int32, q.shape, 1)
    zero = jnp.zeros_like(q)
    return jnp.concatenate([jnp.where(lane < 64, q, zero), jnp.where(lane >= 64, q, zero)], axis=0)


def _dot_nt(a, b):
    return lax.dot_general(a, b, (((1,), (1,)), ((), ())), preferred_element_type=F32)


def _diff_attn_kernel(slopes_ref, lq1_ref, lk1_ref, lq2_ref, lk2_ref, subln_ref, q_ref, k_ref, v_ref, o_ref):
    hd = pl.program_id(1)
    i = pl.program_id(2)
    slope = slopes_ref[hd] * LOG2E
    q2 = _stack_halves(q_ref[...])
    key_pos = lax.broadcasted_iota(jnp.int32, (TK, 2 * TQ), 0).astype(F32)
    key_bias = slope * key_pos

    def kv_group(j0, carry, tiles, offsets):
        m_prev, l_prev, acc = carry
        scores, vals = [], []
        for g in range(KV_GROUP):
            rows = pl.ds(pl.multiple_of((j0 + g) * TK, TK), TK)
            scores.append(_dot_nt(k_ref[rows, :], q2) + tiles[g])
            vals.append(v_ref[rows, :])
        m_new = m_prev
        for s, c in zip(scores, offsets):
            m_new = jnp.maximum(m_new, jnp.max(s, axis=0, keepdims=True) + c)
        alpha = jnp.exp2(m_prev - m_new)
        l_new = alpha * l_prev
        acc = alpha * acc
        for s, c, v in zip(scores, offsets, vals):
            p = jnp.exp2(s - (m_new - c))
            l_new = l_new + jnp.sum(p, axis=0, keepdims=True)
            acc = acc + lax.dot_general(v, p.astype(BF16), (((0,), (0,)), ((), ())),
                                        preferred_element_type=F32)
        return m_new, l_new, acc

    def block_offset(jb):
        return slope * ((jb - i) * TK).astype(F32)

    def full_group(a, carry):
        j0 = a * KV_GROUP
        return kv_group(j0, carry, [key_bias] * KV_GROUP, [block_offset(j0 + g) for g in range(KV_GROUP)])

    init = (jnp.full((1, 2 * TQ), NEG, F32), jnp.zeros((1, 2 * TQ), F32), jnp.zeros((LANES, 2 * TQ), F32))
    n_full = i // KV_GROUP
    carry = lax.fori_loop(0, n_full, full_group, init)

    kk = lax.broadcasted_iota(jnp.int32, (TK, TQ), 0)
    qq = lax.broadcasted_iota(jnp.int32, (TK, TQ), 1)
    diag = jnp.where(kk // CHUNK <= qq // CHUNK, slope * (qq - jnp.abs(qq - kk)).astype(F32), NEG)
    diag = jnp.concatenate([diag, diag], axis=1)
    j0 = n_full * KV_GROUP
    tiles = [jnp.where(j0 + g == i, diag, key_bias) for g in range(KV_GROUP)]
    offsets = [jnp.where(j0 + g > i, NEG, block_offset(j0 + g)) for g in range(KV_GROUP)]
    _, l_fin, acc = kv_group(j0, carry, tiles, offsets)

    lam = (jnp.exp(jnp.sum(lq1_ref[...] * lk1_ref[...], axis=-1, keepdims=True))
           - jnp.exp(jnp.sum(lq2_ref[...] * lk2_ref[...], axis=-1, keepdims=True)) + LAM_INIT)
    o = acc / l_fin
    o = o[:, :TQ] - lam * o[:, TQ:]
    o = o * lax.rsqrt(jnp.mean(o * o, axis=0, keepdims=True) + LN_EPS) * subln_ref[...]
    o_ref[...] = (o * (1.0 - LAM_INIT)).T.astype(BF16)


def _diff_attention(qkv, slopes, lq1, lk1, lq2, lk2, subln, batch, seq):
    nq = seq // TQ
    assert seq % (TQ * KV_GROUP) == 0, "the last key group of a query block must stay inside the sequence"
    n = batch * seq
    vec = lambda w: pl.BlockSpec((1, w), lambda b, h, i: (0, 0))
    return pl.pallas_call(
        _diff_attn_kernel,
        grid=(batch, N_HEADS_A, nq),
        in_specs=[
            pl.BlockSpec(memory_space=pltpu.SMEM),
            vec(HEAD_DIM_A), vec(HEAD_DIM_A), vec(HEAD_DIM_A), vec(HEAD_DIM_A),
            pl.BlockSpec((2 * HEAD_DIM_A, 1), lambda b, h, i: (0, 0)),
            pl.BlockSpec((None, TQ, LANES), lambda b, h, i: (h, b * nq + i, 0)),
            pl.BlockSpec((None, seq, LANES), lambda b, h, i: (N_HEADS_A + h, b, 0)),
            pl.BlockSpec((None, seq, LANES), lambda b, h, i: (2 * N_HEADS_A + h, b, 0)),
        ],
        out_specs=pl.BlockSpec((TQ, LANES), lambda b, h, i: (b * nq + i, h)),
        out_shape=jax.ShapeDtypeStruct((n, WIDTH_A), BF16),
        compiler_params=_params(("parallel", "parallel", "arbitrary")),
        name="diff_attn",
    )(slopes, lq1, lk1, lq2, lk2, subln, qkv, qkv, qkv)


def _band_attn_kernel(bias_ref, q_ref, k_ref, v_ref, o_ref):
    i = pl.program_id(2)
    q2 = _stack_halves(q_ref[...])
    n_sub = BAND // TK
    scores, vals = [], []
    for jj in range(n_sub):
        start = i * TQ - N_PREV_CHUNKS * CHUNK + jj * TK
        before_start = jnp.where(start < 0, NEG, 0.0).astype(F32)
        rows = pl.ds(pl.multiple_of(jnp.maximum(start, 0), TK), TK)
        s = _dot_nt(k_ref[rows, :], q2) + (bias_ref[jj * TK:(jj + 1) * TK, :] + before_start)
        scores.append(s)
        vals.append(v_ref[rows, :])
    m = functools.reduce(jnp.maximum, [jnp.max(s, axis=0, keepdims=True) for s in scores])
    l = jnp.zeros((1, 2 * TQ), F32)
    acc = jnp.zeros((LANES, 2 * TQ), F32)
    for s, v in zip(scores, vals):
        p = jnp.exp2(s - m)
        l = l + jnp.sum(p, axis=0, keepdims=True)
        acc = acc + lax.dot_general(v, p.astype(BF16), (((0,), (0,)), ((), ())), preferred_element_type=F32)
    o = acc / l
    feat = lax.broadcasted_iota(jnp.int32, (LANES, TQ), 0)
    o_ref[...] = jnp.where(feat < 64, o[:, :TQ], o[:, TQ:]).T.astype(BF16)


def _band_bias(rel_bias):
    kj = np.arange(BAND)[:, None] - N_PREV_CHUNKS * CHUNK
    qi = np.arange(TQ)[None, :]
    rel = np.clip(kj - qi, -REL_CLIP, REL_CLIP) + REL_CLIP
    qc = qi // CHUNK
    kc = np.floor_divide(kj, CHUNK)
    allowed = (kc <= qc) & (kc >= qc - N_PREV_CHUNKS)
    tile = jnp.where(jnp.asarray(allowed)[None], LOG2E * rel_bias.astype(F32)[:, jnp.asarray(rel)], NEG)
    pairs = tile.reshape(N_HEADS_B // 2, 2, BAND, TQ).transpose(0, 2, 1, 3)
    return pairs.reshape(N_HEADS_B // 2, BAND, 2 * TQ)


def _band_attention(qkv, bias, batch, seq):
    nq = seq // TQ
    n = batch * seq
    base = 3 * WIDTH_A // LANES
    pairs = N_HEADS_B // 2
    return pl.pallas_call(
        _band_attn_kernel,
        grid=(batch, pairs, nq),
        in_specs=[
            pl.BlockSpec((None, BAND, 2 * TQ), lambda b, p, i: (p, 0, 0)),
            pl.BlockSpec((None, TQ, LANES), lambda b, p, i: (base + p, b * nq + i, 0)),
            pl.BlockSpec((None, seq, LANES), lambda b, p, i: (base + pairs + p, b, 0)),
            pl.BlockSpec((None, seq, LANES), lambda b, p, i: (base + 2 * pairs + p, b, 0)),
        ],
        out_specs=pl.BlockSpec((TQ, LANES), lambda b, p, i: (b * nq + i, p)),
        out_shape=jax.ShapeDtypeStruct((n, WIDTH_B), BF16),
        compiler_params=_params(("parallel", "parallel", "arbitrary")),
        name="band_attn",
    )(bias, qkv, qkv, qkv)


def _pack_bf16_pairs(x):
    w = x.shape[1] // 2
    bits = pltpu.bitcast(x.astype(BF16).astype(F32), jnp.uint32)
    return (bits[:, :w] >> 16) | (bits[:, w:] & jnp.uint32(0xFFFF0000))


def _unpack_bf16_pairs(p):
    lo = pltpu.bitcast(p << 16, F32)
    hi = pltpu.bitcast(p & jnp.uint32(0xFFFF0000), F32)
    return jnp.concatenate([lo, hi], axis=1).astype(BF16)


def _columns(cols, dtype):
    lane = lax.broadcasted_iota(jnp.int32, (cols[0].shape[0], TOP_K), 1)
    out = jnp.zeros(lane.shape, dtype)
    for k, c in enumerate(cols):
        out = jnp.where(lane == k, c.astype(dtype), out)
    return out


def _merge_kernel(oa_ref, ob_ref, gates_ref, h_ref, pa_ref, pb_ref, wo_ref, g_ref, b_ref, wr_ref, br_ref,
                  h1_ref, h1p_ref, gw_ref, idx_ref, rank_ref, counts_ref, seen_ref):
    ma = jnp.dot(oa_ref[...], pa_ref[...], preferred_element_type=F32)
    mb = jnp.dot(ob_ref[...], pb_ref[...], preferred_element_type=F32)
    merged = gates_ref[:, :D_MODEL].astype(F32) * ma + gates_ref[:, D_MODEL:].astype(F32) * mb
    m = jnp.dot(merged.astype(BF16), wo_ref[...], preferred_element_type=F32)
    h1 = _layer_norm(DEEPNORM_ALPHA * h_ref[...] + m, g_ref[...], b_ref[...])
    h1_ref[...] = h1
    h1p_ref[...] = _pack_bf16_pairs(h1)
    logits = jnp.dot(h1, wr_ref[...], preferred_element_type=F32, precision=lax.Precision.HIGHEST) + br_ref[...]

    tm = logits.shape[0]
    expert = lax.broadcasted_iota(jnp.int32, (tm, N_EXPERTS), 1)
    vals = logits
    top_val, top_idx = [], []
    for _ in range(TOP_K):
        mx = jnp.max(vals, axis=-1, keepdims=True)
        sel = jnp.min(jnp.where(vals == mx, expert, N_EXPERTS), axis=-1, keepdims=True)
        top_val.append(mx)
        top_idx.append(sel)
        vals = jnp.where(expert == sel, -jnp.inf, vals)
    ex = [jnp.exp(v - top_val[0]) for v in top_val]
    denom = functools.reduce(jnp.add, ex)
    gw_ref[...] = _columns([e / denom for e in ex], F32)
    idx_ref[...] = _columns(top_idx, jnp.int32)

    @pl.when(pl.program_id(0) == 0)
    def _():
        seen_ref[...] = jnp.zeros_like(seen_ref)

    chosen = functools.reduce(jnp.logical_or, [expert == s for s in top_idx])
    onehot = jnp.where(chosen, 1.0, 0.0).astype(BF16)
    r_i = lax.broadcasted_iota(jnp.int32, (tm, tm), 0)
    c_i = lax.broadcasted_iota(jnp.int32, (tm, tm), 1)
    earlier = jnp.where(c_i < r_i, 1.0, 0.0).astype(BF16)
    before = jnp.dot(earlier, onehot, preferred_element_type=F32) + seen_ref[...]
    rank_ref[...] = _columns([jnp.sum(jnp.where(expert == s, before, 0.0), axis=-1, keepdims=True)
                              for s in top_idx], jnp.int32)
    seen_ref[...] = seen_ref[...] + jnp.sum(onehot.astype(F32), axis=0, keepdims=True)
    counts_ref[...] = seen_ref[...].astype(jnp.int32)


def _merge(oa, ob, gates, h, pa, pb, wo, g, b, wr, br):
    n = h.shape[0]
    tm = min(TM_MERGE, n)
    row = lambda w: pl.BlockSpec((tm, w), lambda i: (i, 0))
    full = lambda r, c: pl.BlockSpec((r, c), lambda i: (0, 0))
    return pl.pallas_call(
        _merge_kernel,
        grid=(n // tm,),
        in_specs=[row(WIDTH_A), row(WIDTH_B), row(GATE_COLS), row(D_MODEL),
                  full(WIDTH_A, D_MODEL), full(WIDTH_B, D_MODEL), full(D_MODEL, D_MODEL),
                  full(1, D_MODEL), full(1, D_MODEL), full(D_MODEL, N_EXPERTS), full(1, N_EXPERTS)],
        out_specs=[row(D_MODEL), row(D_MODEL // 2), row(TOP_K), row(TOP_K), row(TOP_K), full(1, N_EXPERTS)],
        out_shape=[jax.ShapeDtypeStruct((n, D_MODEL), F32), jax.ShapeDtypeStruct((n, D_MODEL // 2), jnp.uint32),
                   jax.ShapeDtypeStruct((n, TOP_K), F32), jax.ShapeDtypeStruct((n, TOP_K), jnp.int32),
                   jax.ShapeDtypeStruct((n, TOP_K), jnp.int32), jax.ShapeDtypeStruct((1, N_EXPERTS), jnp.int32)],
        scratch_shapes=[pltpu.VMEM((1, N_EXPERTS), F32)],
        compiler_params=_params(("arbitrary",)),
        name="merge_ln1_router",
    )(oa, ob, gates, h, pa, pb, wo, g, b, wr, br)


def _dispatch_kernel(dest_ref, h1p_ref, zeros_hbm, xs_hbm, sem):
    del zeros_hbm

    def start(t, _):
        for k in range(TOP_K):
            pltpu.make_async_copy(h1p_ref.at[pl.ds(t, 1), :],
                                  xs_hbm.at[pl.ds(dest_ref[0, 0, t * TOP_K + k], 1), :], sem).start()
        return 0
    lax.fori_loop(0, TM_DISPATCH, start, 0, unroll=2)

    for k in range(TOP_K):
        pltpu.make_async_copy(h1p_ref, xs_hbm.at[pl.ds(0, TM_DISPATCH), :], sem).wait()


def _dispatch(dest, h1p, n_rows):
    n = h1p.shape[0]
    tm = min(TM_DISPATCH, n)
    assert tm == TM_DISPATCH
    n_steps = n // tm
    zeros = jnp.zeros((n_rows, D_MODEL // 2), jnp.uint32)
    return pl.pallas_call(
        _dispatch_kernel,
        grid=(n_steps,),
        in_specs=[
            pl.BlockSpec((1, 1, tm * TOP_K), lambda i: (i, 0, 0), memory_space=pltpu.SMEM),
            pl.BlockSpec((tm, D_MODEL // 2), lambda i: (i, 0)),
            pl.BlockSpec(memory_space=pl.ANY),
        ],
        out_specs=pl.BlockSpec(memory_space=pl.ANY),
        out_shape=jax.ShapeDtypeStruct((n_rows, D_MODEL // 2), jnp.uint32),
        scratch_shapes=[pltpu.SemaphoreType.DMA(())],
        input_output_aliases={2: 0},
        compiler_params=_params(("arbitrary",)),
        name="dispatch_rows",
    )(dest.reshape(n_steps, 1, tm * TOP_K), h1p, zeros)


def _row_gather(src_hbm, idx_ref, dst_ref, sem, n_rows):
    def start(r, _):
        pltpu.make_async_copy(src_hbm.at[pl.ds(idx_ref[0, 0, r], 1), :], dst_ref.at[pl.ds(r, 1), :], sem).start()
        return 0
    lax.fori_loop(0, n_rows, start, 0, unroll=8)


def _moe_kernel(bexp_ref, nused_ref, x_ref, wi_ref, bi_ref, wo_ref, bo_ref, y_ref):
    del bexp_ref
    i = pl.program_id(0)
    n_used = nused_ref[0]

    @pl.when(i < n_used)
    def _():
        x = _unpack_bf16_pairs(x_ref[...])
        hb = jnp.dot(x, wi_ref[...], preferred_element_type=F32) + bi_ref[...]
        g = jnp.minimum(hb[:, :D_FF], SWIGLU_LIMIT)
        u = jnp.clip(hb[:, D_FF:], -SWIGLU_LIMIT, SWIGLU_LIMIT)
        a = g * jax.nn.sigmoid(SWIGLU_ALPHA * g) * (u + 1.0)
        y_ref[...] = jnp.dot(a.astype(BF16), wo_ref[...], preferred_element_type=F32) + bo_ref[...]

    @pl.when(i >= n_used)
    def _():
        y_ref[...] = jnp.zeros_like(y_ref)


def _moe(block_exp, n_used, x_sorted, wi, bi, wo, bo):
    n_blocks = block_exp.shape[0]
    x_block = lambda i, be, nu: (jnp.minimum(i, nu[0] - 1), 0)
    grid_spec = pltpu.PrefetchScalarGridSpec(
        num_scalar_prefetch=2,
        grid=(n_blocks,),
        in_specs=[
            pl.BlockSpec((ROWS_MOE, D_MODEL // 2), x_block),
            pl.BlockSpec((None, D_MODEL, 2 * D_FF), lambda i, be, nu: (be[i], 0, 0)),
            pl.BlockSpec((None, 1, 2 * D_FF), lambda i, be, nu: (be[i], 0, 0)),
            pl.BlockSpec((None, D_FF, D_MODEL), lambda i, be, nu: (be[i], 0, 0)),
            pl.BlockSpec((None, 1, D_MODEL), lambda i, be, nu: (be[i], 0, 0)),
        ],
        out_specs=pl.BlockSpec((ROWS_MOE, D_MODEL), lambda i, be, nu: (i, 0)),
    )
    return pl.pallas_call(
        _moe_kernel,
        grid_spec=grid_spec,
        out_shape=jax.ShapeDtypeStruct((n_blocks * ROWS_MOE, D_MODEL), F32),
        compiler_params=_params(("arbitrary",)),
        name="moe_experts",
    )(block_exp, n_used, x_sorted, wi, bi, wo, bo)


def _combine_kernel(dest_cur_ref, dest_nxt_ref, gw_ref, h1_ref, g_ref, b_ref, y_hbm, o_ref, ybuf, sems):
    i = pl.program_id(0)
    n_steps = pl.num_programs(0)
    slot = i % 2
    rows = TOP_K * TM_COMB

    @pl.when(i == 0)
    def _():
        _row_gather(y_hbm, dest_cur_ref, ybuf.at[0], sems.at[0], rows)

    @pl.when(i + 1 < n_steps)
    def _():
        _row_gather(y_hbm, dest_nxt_ref, ybuf.at[1 - slot], sems.at[1 - slot], rows)

    pltpu.make_async_copy(y_hbm.at[pl.ds(0, rows), :], ybuf.at[slot], sems.at[slot]).wait()
    gw = gw_ref[...]
    f = jnp.zeros((TM_COMB, D_MODEL), F32)
    for k in range(TOP_K):
        f = f + gw[:, k:k + 1] * ybuf[slot, k * TM_COMB:(k + 1) * TM_COMB, :]
    o_ref[...] = _layer_norm(DEEPNORM_ALPHA * h1_ref[...] + f, g_ref[...], b_ref[...])


def _combine(dest_km, gate_w, h1, g, b, y_buf):
    n = h1.shape[0]
    n_steps = n // TM_COMB
    rows = TOP_K * TM_COMB
    return pl.pallas_call(
        _combine_kernel,
        grid=(n_steps,),
        in_specs=[
            pl.BlockSpec((1, 1, rows), lambda i: (i, 0, 0), memory_space=pltpu.SMEM),
            pl.BlockSpec((1, 1, rows), lambda i: (jnp.minimum(i + 1, n_steps - 1), 0, 0), memory_space=pltpu.SMEM),
            pl.BlockSpec((TM_COMB, TOP_K), lambda i: (i, 0)),
            pl.BlockSpec((TM_COMB, D_MODEL), lambda i: (i, 0)),
            pl.BlockSpec((1, D_MODEL), lambda i: (0, 0)),
            pl.BlockSpec((1, D_MODEL), lambda i: (0, 0)),
            pl.BlockSpec(memory_space=pl.ANY),
        ],
        out_specs=pl.BlockSpec((TM_COMB, D_MODEL), lambda i: (i, 0)),
        out_shape=jax.ShapeDtypeStruct((n, D_MODEL), F32),
        scratch_shapes=[pltpu.VMEM((2, rows, D_MODEL), F32), pltpu.SemaphoreType.DMA((2,))],
        compiler_params=_params(("arbitrary",)),
        name="combine_ln2",
    )(dest_km, dest_km, gate_w, h1, g, b, y_buf)


def _block_layout(top_idx, rank, counts):
    n_slots = top_idx.shape[0] * TOP_K
    padded = (counts + ROWS_MOE - 1) // ROWS_MOE * ROWS_MOE
    pend = jnp.cumsum(padded)
    pstart = pend - padded
    experts = jnp.arange(N_EXPERTS, dtype=jnp.int32)
    dest = rank + jnp.sum(jnp.where(top_idx[..., None] == experts, pstart, 0), axis=-1)
    n_blocks = n_slots // ROWS_MOE + N_EXPERTS
    block_start = jnp.arange(n_blocks, dtype=jnp.int32) * ROWS_MOE
    block_exp = jnp.minimum(jnp.sum(block_start[:, None] >= pend[None, :], axis=-1), N_EXPERTS - 1).astype(jnp.int32)
    n_used = (pend[-1] // ROWS_MOE).astype(jnp.int32).reshape(1)
    return dest.astype(jnp.int32), block_exp, n_used


def kernel(x, ln_in_g, ln_in_b, w_in, b_gate, lambda_q1, lambda_k1, lambda_q2, lambda_k2, subln_w, rel_bias,
           w_branch_a, w_branch_b, w_out, ln1_g, ln1_b, w_router, b_router, w_exp_in, b_exp_in, w_exp_out,
           b_exp_out, ln2_g, ln2_b):
    batch, seq, d = x.shape
    n = batch * seq
    row = lambda a: a.reshape(1, -1).astype(F32)
    l = 0
    h, qkv, gates = _ln_proj(x.reshape(n, d), row(ln_in_g), row(ln_in_b), w_in[l].astype(BF16), row(b_gate[l]))
    slopes = jnp.asarray([2.0 ** (-8.0 * (i + 1) / N_HEADS_A) for i in range(N_HEADS_A)], F32)
    out_a = _diff_attention(qkv, slopes, row(lambda_q1[l]), row(lambda_k1[l]), row(lambda_q2[l]),
                            row(lambda_k2[l]), subln_w[l].reshape(-1, 1).astype(F32), batch, seq)
    out_b = _band_attention(qkv, _band_bias(rel_bias[l]), batch, seq)
    h1, h1p, gate_w, top_idx, rank, counts = _merge(
        out_a, out_b, gates, h, w_branch_a[l].astype(BF16), w_branch_b[l].astype(BF16), w_out[l].astype(BF16),
        row(ln1_g[l]), row(ln1_b[l]), w_router[l].astype(F32), row(b_router[l]))
    dest, block_exp, n_used = _block_layout(top_idx, rank, counts[0])
    x_sorted = _dispatch(dest, h1p, block_exp.shape[0] * ROWS_MOE)
    y_buf = _moe(block_exp, n_used, x_sorted, w_exp_in[l].astype(BF16), b_exp_in[l].reshape(N_EXPERTS, 1, -1),
                 w_exp_out[l].astype(BF16), b_exp_out[l].reshape(N_EXPERTS, 1, -1))
    n_steps = n // TM_COMB
    dest_km = dest.reshape(n_steps, TM_COMB, TOP_K).transpose(0, 2, 1).reshape(n_steps, 1, TOP_K * TM_COMB)
    out = _combine(dest_km, gate_w, h1, row(ln2_g[l]), row(ln2_b[l]), y_buf)
    return out.reshape(batch, seq, d)
```

```python
import functools
import math

import jax
import jax.numpy as jnp
import numpy as np
from jax import lax
from jax.experimental import pallas as pl
from jax.experimental.pallas import tpu as pltpu

F32 = jnp.float32
BF16 = jnp.bfloat16

D_MODEL = 1024
CHUNK = 64
N_HEADS_A = 4
HEAD_DIM_A = 64
WIDTH_A = 512
N_HEADS_B = 8
HEAD_DIM_B = 64
WIDTH_B = 512
N_PREV_CHUNKS = 8
REL_CLIP = 128
N_EXPERTS = 32
TOP_K = 4
D_FF = 1024
SWIGLU_ALPHA = 1.702
SWIGLU_LIMIT = 7.0
DEEPNORM_ALPHA = 2.0 ** 0.25
LN_EPS = 1e-5
LAM_INIT = 0.8 - 0.6 * math.exp(-0.3 * 0)

LANES = 128
N_SLABS = (3 * WIDTH_A + 3 * WIDTH_B) // LANES
GATE_COLS = 2 * D_MODEL
IN_COLS = 3 * WIDTH_A + 3 * WIDTH_B + GATE_COLS
LOG2E = math.log2(math.e)
NEG = -1e30

TM_PROJ = 512
PROJ_CHUNK = 512
TQ = 256
TK = 256
KV_GROUP = 2
BAND = 3 * TK
TM_MERGE = 256
TM_DISPATCH = 512
ROWS_MOE = 256
TM_COMB = 256
VMEM_LIMIT = 56 * 1024 * 1024


def _layer_norm(x, g, b):
    mu = jnp.mean(x, axis=-1, keepdims=True)
    xc = x - mu
    var = jnp.mean(xc * xc, axis=-1, keepdims=True)
    return xc * lax.rsqrt(var + LN_EPS) * g + b


def _params(sem):
    return pltpu.CompilerParams(dimension_semantics=sem, vmem_limit_bytes=VMEM_LIMIT)


def _ln_proj_kernel(x_ref, g_ref, b_ref, w_ref, bg_ref, h_ref, qkv_ref, gates_ref):
    h = _layer_norm(x_ref[...], g_ref[...], b_ref[...])
    h_ref[...] = h
    hb = h.astype(BF16)
    n_qkv_chunks = (N_SLABS * LANES) // PROJ_CHUNK
    slabs_per_chunk = PROJ_CHUNK // LANES
    q_scale = HEAD_DIM_A ** -0.5 * LOG2E
    for c in range(n_qkv_chunks):
        r = jnp.dot(hb, w_ref[:, c * PROJ_CHUNK:(c + 1) * PROJ_CHUNK], preferred_element_type=F32)
        first = c * slabs_per_chunk
        is_q = (first < WIDTH_A // LANES) or (3 * WIDTH_A // LANES <= first < (3 * WIDTH_A + WIDTH_B) // LANES)
        if is_q:
            r = r * q_scale
        for s in range(slabs_per_chunk):
            qkv_ref[first + s] = r[:, s * LANES:(s + 1) * LANES].astype(BF16)
    g0 = N_SLABS * LANES
    for c in range(GATE_COLS // PROJ_CHUNK):
        r = jnp.dot(hb, w_ref[:, g0 + c * PROJ_CHUNK:g0 + (c + 1) * PROJ_CHUNK], preferred_element_type=F32)
        r = r + bg_ref[:, c * PROJ_CHUNK:(c + 1) * PROJ_CHUNK]
        gates_ref[:, c * PROJ_CHUNK:(c + 1) * PROJ_CHUNK] = jax.nn.sigmoid(r).astype(BF16)


def _ln_proj(x2d, g, b, w_bf16, b_gate):
    n = x2d.shape[0]
    tm = min(TM_PROJ, n)
    const = lambda i: (0, 0)
    return pl.pallas_call(
        _ln_proj_kernel,
        grid=(n // tm,),
        in_specs=[
            pl.BlockSpec((tm, D_MODEL), lambda i: (i, 0)),
            pl.BlockSpec((1, D_MODEL), const),
            pl.BlockSpec((1, D_MODEL), const),
            pl.BlockSpec((D_MODEL, IN_COLS), const, pipeline_mode=pl.Buffered(1)),
            pl.BlockSpec((1, GATE_COLS), const),
        ],
        out_specs=[
            pl.BlockSpec((tm, D_MODEL), lambda i: (i, 0)),
            pl.BlockSpec((N_SLABS, tm, LANES), lambda i: (0, i, 0)),
            pl.BlockSpec((tm, GATE_COLS), lambda i: (i, 0)),
        ],
        out_shape=[
            jax.ShapeDtypeStruct((n, D_MODEL), F32),
            jax.ShapeDtypeStruct((N_SLABS, n, LANES), BF16),
            jax.ShapeDtypeStruct((n, GATE_COLS), BF16),
        ],
        compiler_params=_params(("parallel",)),
        name="ln_proj",
    )(x2d, g, b, w_bf16, b_gate)


def _stack_halves(q):
    lane = lax.broadcasted_iota(jnp.int32, q.shape, 1)
    zero = jnp.zeros_like(q)
    return jnp.concatenate([jnp.where(lane < 64, q, zero), jnp.where(lane >= 64, q, zero)], axis=0)


def _dot_nt(a, b):
    return lax.dot_general(a, b, (((1,), (1,)), ((), ())), preferred_element_type=F32)


def _split3_bf16(x):
    hi = x.astype(BF16)
    r1 = x - hi.astype(F32)
    mid = r1.astype(BF16)
    lo = (r1 - mid.astype(F32)).astype(BF16)
    return hi, mid, lo


def _diff_attn_kernel(slopes_ref, lq1_ref, lk1_ref, lq2_ref, lk2_ref, subln_ref, q_ref, k_ref, v_ref, o_ref,
                      s0_ref, s1_ref, m_ref, l_ref, acc_ref):
    hd = pl.program_id(1)
    i = pl.program_id(2)
    slope = slopes_ref[hd] * LOG2E

    lane_q = lax.broadcasted_iota(jnp.int32, (2 * TQ, LANES), 1)
    q_aug = jnp.concatenate([_stack_halves(q_ref[...]), jnp.where(lane_q < 3, 1.0, 0.0).astype(BF16)], axis=1)
    lane_k = lax.broadcasted_iota(jnp.int32, (TK, LANES), 1)
    key_pos = lax.broadcasted_iota(jnp.int32, (TK, LANES), 0).astype(F32)
    hi, mid, lo = [t.astype(F32) for t in _split3_bf16(slope * key_pos)]
    k_bias = jnp.where(lane_k == 0, hi, jnp.where(lane_k == 1, mid, jnp.where(lane_k == 2, lo, 0.0))).astype(BF16)

    def rows_of(jb):
        return pl.ds(pl.multiple_of(jb * TK, TK), TK)

    def scores_into(s_ref, a):
        for g in range(KV_GROUP):
            k_aug = jnp.concatenate([k_ref[rows_of(a * KV_GROUP + g), :], k_bias], axis=1)
            s_ref[g * TK:(g + 1) * TK, :] = _dot_nt(k_aug, q_aug)

    def block_offset(jb):
        return slope * ((jb - i) * TK).astype(F32)

    def softmax_update(s_ref, a, tiles, offsets):
        def score(g):
            s = s_ref[g * TK:(g + 1) * TK, :]
            return s if tiles is None else s + tiles[g]
        m_prev = m_ref[...]
        m_new = m_prev
        for g in range(KV_GROUP):
            m_new = jnp.maximum(m_new, jnp.max(score(g), axis=0, keepdims=True) + offsets[g])
        alpha = jnp.exp2(m_prev - m_new)
        l_new = alpha * l_ref[...]
        acc = alpha * acc_ref[...]
        for g in range(KV_GROUP):
            p = jnp.exp2(score(g) - (m_new - offsets[g]))
            l_new = l_new + jnp.sum(p, axis=0, keepdims=True)
            acc = acc + lax.dot_general(v_ref[rows_of(a * KV_GROUP + g), :], p.astype(BF16),
                                        (((0,), (0,)), ((), ())), preferred_element_type=F32)
        m_ref[...] = m_new
        l_ref[...] = l_new
        acc_ref[...] = acc

    def full_update(s_ref, a):
        softmax_update(s_ref, a, None, [block_offset(a * KV_GROUP + g) for g in range(KV_GROUP)])

    m_ref[...] = jnp.full(m_ref.shape, NEG, F32)
    l_ref[...] = jnp.zeros(l_ref.shape, F32)
    acc_ref[...] = jnp.zeros(acc_ref.shape, F32)
    n_full = i // KV_GROUP
    odd = n_full % 2

    @pl.when(odd == 1)
    def _():
        scores_into(s0_ref, 0)
        full_update(s0_ref, 0)

    scores_into(s0_ref, odd)

    def pair(b, _):
        a = odd + 2 * b
        scores_into(s1_ref, a + 1)
        full_update(s0_ref, a)
        scores_into(s0_ref, a + 2)
        full_update(s1_ref, a + 1)
        return 0
    lax.fori_loop(0, (n_full - odd) // 2, pair, 0)

    kk = lax.broadcasted_iota(jnp.int32, (TK, TQ), 0)
    qq = lax.broadcasted_iota(jnp.int32, (TK, TQ), 1)
    diag = jnp.where(kk // CHUNK <= qq // CHUNK, slope * (qq - jnp.abs(qq - kk) - kk).astype(F32), NEG)
    diag = jnp.concatenate([diag, diag], axis=1)
    j0 = n_full * KV_GROUP
    tiles = [jnp.where(j0 + g == i, diag, 0.0) for g in range(KV_GROUP)]
    offsets = [jnp.where(j0 + g > i, NEG, block_offset(j0 + g)) for g in range(KV_GROUP)]
    softmax_update(s0_ref, n_full, tiles, offsets)

    lam = (jnp.exp(jnp.sum(lq1_ref[...] * lk1_ref[...], axis=-1, keepdims=True))
           - jnp.exp(jnp.sum(lq2_ref[...] * lk2_ref[...], axis=-1, keepdims=True)) + LAM_INIT)
    o = acc_ref[...] / l_ref[...]
    o = o[:, :TQ] - lam * o[:, TQ:]
    o = o * lax.rsqrt(jnp.mean(o * o, axis=0, keepdims=True) + LN_EPS) * subln_ref[...]
    o_ref[...] = (o * (1.0 - LAM_INIT)).T.astype(BF16)


def _diff_attention(qkv, slopes, lq1, lk1, lq2, lk2, subln, batch, seq):
    nq = seq // TQ
    assert seq % (TQ * KV_GROUP) == 0, "the last key group of a query block must stay inside the sequence"
    n = batch * seq
    vec = lambda w: pl.BlockSpec((1, w), lambda b, h, i: (0, 0))
    return pl.pallas_call(
        _diff_attn_kernel,
        grid=(batch, N_HEADS_A, nq),
        in_specs=[
            pl.BlockSpec(memory_space=pltpu.SMEM),
            vec(HEAD_DIM_A), vec(HEAD_DIM_A), vec(HEAD_DIM_A), vec(HEAD_DIM_A),
            pl.BlockSpec((2 * HEAD_DIM_A, 1), lambda b, h, i: (0, 0)),
            pl.BlockSpec((None, TQ, LANES), lambda b, h, i: (h, b * nq + i, 0)),
            pl.BlockSpec((None, seq, LANES), lambda b, h, i: (N_HEADS_A + h, b, 0)),
            pl.BlockSpec((None, seq, LANES), lambda b, h, i: (2 * N_HEADS_A + h, b, 0)),
        ],
        out_specs=pl.BlockSpec((TQ, LANES), lambda b, h, i: (b * nq + i, h)),
        out_shape=jax.ShapeDtypeStruct((n, WIDTH_A), BF16),
        scratch_shapes=[pltpu.VMEM((KV_GROUP * TK, 2 * TQ), F32), pltpu.VMEM((KV_GROUP * TK, 2 * TQ), F32),
                        pltpu.VMEM((1, 2 * TQ), F32), pltpu.VMEM((1, 2 * TQ), F32),
                        pltpu.VMEM((LANES, 2 * TQ), F32)],
        compiler_params=_params(("parallel", "parallel", "arbitrary")),
        name="diff_attn",
    )(slopes, lq1, lk1, lq2, lk2, subln, qkv, qkv, qkv)


def _band_attn_kernel(bias_ref, q_ref, k_ref, v_ref, o_ref):
    i = pl.program_id(2)
    q2 = _stack_halves(q_ref[...])
    n_sub = BAND // TK
    scores, vals = [], []
    for jj in range(n_sub):
        start = i * TQ - N_PREV_CHUNKS * CHUNK + jj * TK
        before_start = jnp.where(start < 0, NEG, 0.0).astype(F32)
        rows = pl.ds(pl.multiple_of(jnp.maximum(start, 0), TK), TK)
        s = _dot_nt(k_ref[rows, :], q2) + (bias_ref[jj * TK:(jj + 1) * TK, :] + before_start)
        scores.append(s)
        vals.append(v_ref[rows, :])
    m = functools.reduce(jnp.maximum, [jnp.max(s, axis=0, keepdims=True) for s in scores])
    l = jnp.zeros((1, 2 * TQ), F32)
    acc = jnp.zeros((LANES, 2 * TQ), F32)
    for s, v in zip(scores, vals):
        p = jnp.exp2(s - m)
        l = l + jnp.sum(p, axis=0, keepdims=True)
        acc = acc + lax.dot_general(v, p.astype(BF16), (((0,), (0,)), ((), ())), preferred_element_type=F32)
    o = acc / l
    feat = lax.broadcasted_iota(jnp.int32, (LANES, TQ), 0)
    o_ref[...] = jnp.where(feat < 64, o[:, :TQ], o[:, TQ:]).T.astype(BF16)


BIAS_SPAN = 1024


def _band_bias_kernel(line_ref, o_ref):
    kj = lax.broadcasted_iota(jnp.int32, (BAND, TQ), 0)
    qi = lax.broadcasted_iota(jnp.int32, (BAND, TQ), 1)
    kc = kj // CHUNK - N_PREV_CHUNKS
    qc = qi // CHUNK
    allowed = jnp.logical_and(kc <= qc, kc >= qc - N_PREV_CHUNKS)
    for hh in range(2):
        line = jnp.broadcast_to(line_ref[hh:hh + 1, :], (BAND, BIAS_SPAN))
        rolled = pltpu.roll(line, BIAS_SPAN - (BAND - 1), 1, stride=1, stride_axis=0)
        o_ref[:, hh * TQ:(hh + 1) * TQ] = jnp.where(allowed, LOG2E * rolled[:, :TQ], NEG)


def _band_bias(rel_bias):
    rb = rel_bias.astype(F32)
    n_low = (BAND - 1) - REL_CLIP
    n_high = (BAND + TQ - 1) - (BAND - 1) - REL_CLIP - 1
    line = jnp.concatenate([jnp.broadcast_to(rb[:, :1], (N_HEADS_B, n_low)), rb,
                            jnp.broadcast_to(rb[:, -1:], (N_HEADS_B, n_high))], axis=1)
    line = jnp.pad(line[:, ::-1], ((0, 0), (0, BIAS_SPAN - line.shape[1])))
    pairs = N_HEADS_B // 2
    return pl.pallas_call(
        _band_bias_kernel,
        grid=(pairs,),
        in_specs=[pl.BlockSpec((None, 2, BIAS_SPAN), lambda p: (p, 0, 0))],
        out_specs=pl.BlockSpec((None, BAND, 2 * TQ), lambda p: (p, 0, 0)),
        out_shape=jax.ShapeDtypeStruct((pairs, BAND, 2 * TQ), F32),
        compiler_params=_params(("parallel",)),
        name="band_bias",
    )(line.reshape(pairs, 2, BIAS_SPAN))


def _band_attention(qkv, bias, batch, seq):
    nq = seq // TQ
    n = batch * seq
    base = 3 * WIDTH_A // LANES
    pairs = N_HEADS_B // 2
    return pl.pallas_call(
        _band_attn_kernel,
        grid=(batch, pairs, nq),
        in_specs=[
            pl.BlockSpec((None, BAND, 2 * TQ), lambda b, p, i: (p, 0, 0)),
            pl.BlockSpec((None, TQ, LANES), lambda b, p, i: (base + p, b * nq + i, 0)),
            pl.BlockSpec((None, seq, LANES), lambda b, p, i: (base + pairs + p, b, 0)),
            pl.BlockSpec((None, seq, LANES), lambda b, p, i: (base + 2 * pairs + p, b, 0)),
        ],
        out_specs=pl.BlockSpec((TQ, LANES), lambda b, p, i: (b * nq + i, p)),
        out_shape=jax.ShapeDtypeStruct((n, WIDTH_B), BF16),
        compiler_params=_params(("parallel", "parallel", "arbitrary")),
        name="band_attn",
    )(bias, qkv, qkv, qkv)


def _pack_bf16_pairs(x):
    w = x.shape[1] // 2
    bits = pltpu.bitcast(x.astype(BF16).astype(F32), jnp.uint32)
    return (bits[:, :w] >> 16) | (bits[:, w:] & jnp.uint32(0xFFFF0000))


def _unpack_bf16_pairs(p):
    lo = pltpu.bitcast(p << 16, F32)
    hi = pltpu.bitcast(p & jnp.uint32(0xFFFF0000), F32)
    return jnp.concatenate([lo, hi], axis=1).astype(BF16)


def _columns(cols, dtype):
    lane = lax.broadcasted_iota(jnp.int32, (cols[0].shape[0], TOP_K), 1)
    out = jnp.zeros(lane.shape, dtype)
    for k, c in enumerate(cols):
        out = jnp.where(lane == k, c.astype(dtype), out)
    return out


def _merge_kernel(oa_ref, ob_ref, gates_ref, h_ref, pa_ref, pb_ref, wo_ref, g_ref, b_ref, wr_ref, br_ref,
                  h1_ref, h1p_ref, gw_ref, idx_ref, rank_ref, counts_ref, seen_ref):
    ma = jnp.dot(oa_ref[...], pa_ref[...], preferred_element_type=F32)
    mb = jnp.dot(ob_ref[...], pb_ref[...], preferred_element_type=F32)
    merged = gates_ref[:, :D_MODEL].astype(F32) * ma + gates_ref[:, D_MODEL:].astype(F32) * mb
    m = jnp.dot(merged.astype(BF16), wo_ref[...], preferred_element_type=F32)
    h1 = _layer_norm(DEEPNORM_ALPHA * h_ref[...] + m, g_ref[...], b_ref[...])
    h1_ref[...] = h1
    h1p_ref[...] = _pack_bf16_pairs(h1)
    logits = jnp.dot(h1, wr_ref[...], preferred_element_type=F32, precision=lax.Precision.HIGHEST) + br_ref[...]

    tm = logits.shape[0]
    expert = lax.broadcasted_iota(jnp.int32, (tm, N_EXPERTS), 1)
    vals = logits
    top_val, top_idx = [], []
    for _ in range(TOP_K):
        mx = jnp.max(vals, axis=-1, keepdims=True)
        sel = jnp.min(jnp.where(vals == mx, expert, N_EXPERTS), axis=-1, keepdims=True)
        top_val.append(mx)
        top_idx.append(sel)
        vals = jnp.where(expert == sel, -jnp.inf, vals)
    ex = [jnp.exp(v - top_val[0]) for v in top_val]
    denom = functools.reduce(jnp.add, ex)
    gw_ref[...] = _columns([e / denom for e in ex], F32)
    idx_ref[...] = _columns(top_idx, jnp.int32)

    @pl.when(pl.program_id(0) == 0)
    def _():
        seen_ref[...] = jnp.zeros_like(seen_ref)

    chosen = functools.reduce(jnp.logical_or, [expert == s for s in top_idx])
    onehot = jnp.where(chosen, 1.0, 0.0).astype(BF16)
    r_i = lax.broadcasted_iota(jnp.int32, (tm, tm), 0)
    c_i = lax.broadcasted_iota(jnp.int32, (tm, tm), 1)
    earlier = jnp.where(c_i < r_i, 1.0, 0.0).astype(BF16)
    before = jnp.dot(earlier, onehot, preferred_element_type=F32) + seen_ref[...]
    rank_ref[...] = _columns([jnp.sum(jnp.where(expert == s, before, 0.0), axis=-1, keepdims=True)
                              for s in top_idx], jnp.int32)
    seen_ref[...] = seen_ref[...] + jnp.sum(onehot.astype(F32), axis=0, keepdims=True)
    counts_ref[...] = seen_ref[...].astype(jnp.int32)


def _merge(oa, ob, gates, h, pa, pb, wo, g, b, wr, br):
    n = h.shape[0]
    tm = min(TM_MERGE, n)
    row = lambda w: pl.BlockSpec((tm, w), lambda i: (i, 0))
    full = lambda r, c: pl.BlockSpec((r, c), lambda i: (0, 0))
    return pl.pallas_call(
        _merge_kernel,
        grid=(n // tm,),
        in_specs=[row(WIDTH_A), row(WIDTH_B), row(GATE_COLS), row(D_MODEL),
                  full(WIDTH_A, D_MODEL), full(WIDTH_B, D_MODEL), full(D_MODEL, D_MODEL),
                  full(1, D_MODEL), full(1, D_MODEL), full(D_MODEL, N_EXPERTS), full(1, N_EXPERTS)],
        out_specs=[row(D_MODEL), row(D_MODEL // 2), row(TOP_K), row(TOP_K), row(TOP_K), full(1, N_EXPERTS)],
        out_shape=[jax.ShapeDtypeStruct((n, D_MODEL), F32), jax.ShapeDtypeStruct((n, D_MODEL // 2), jnp.uint32),
                   jax.ShapeDtypeStruct((n, TOP_K), F32), jax.ShapeDtypeStruct((n, TOP_K), jnp.int32),
                   jax.ShapeDtypeStruct((n, TOP_K), jnp.int32), jax.ShapeDtypeStruct((1, N_EXPERTS), jnp.int32)],
        scratch_shapes=[pltpu.VMEM((1, N_EXPERTS), F32)],
        compiler_params=_params(("arbitrary",)),
        name="merge_ln1_router",
    )(oa, ob, gates, h, pa, pb, wo, g, b, wr, br)


def _dispatch_kernel(dest_ref, h1p_ref, zeros_hbm, xs_hbm, sem):
    del zeros_hbm

    def start(t, _):
        for k in range(TOP_K):
            pltpu.make_async_copy(h1p_ref.at[pl.ds(t, 1), :],
                                  xs_hbm.at[pl.ds(dest_ref[0, 0, t * TOP_K + k], 1), :], sem).start(priority=k % 2)
        return 0
    lax.fori_loop(0, TM_DISPATCH, start, 0, unroll=2)

    for k in range(TOP_K):
        pltpu.make_async_copy(h1p_ref, xs_hbm.at[pl.ds(0, TM_DISPATCH), :], sem).wait()


def _dispatch(dest, h1p, n_rows):
    n = h1p.shape[0]
    tm = min(TM_DISPATCH, n)
    assert tm == TM_DISPATCH
    n_steps = n // tm
    zeros = jnp.zeros((n_rows, D_MODEL // 2), jnp.uint32)
    return pl.pallas_call(
        _dispatch_kernel,
        grid=(n_steps,),
        in_specs=[
            pl.BlockSpec((1, 1, tm * TOP_K), lambda i: (i, 0, 0), memory_space=pltpu.SMEM),
            pl.BlockSpec((tm, D_MODEL // 2), lambda i: (i, 0)),
            pl.BlockSpec(memory_space=pl.ANY),
        ],
        out_specs=pl.BlockSpec(memory_space=pl.ANY),
        out_shape=jax.ShapeDtypeStruct((n_rows, D_MODEL // 2), jnp.uint32),
        scratch_shapes=[pltpu.SemaphoreType.DMA(())],
        input_output_aliases={2: 0},
        compiler_params=_params(("arbitrary",)),
        name="dispatch_rows",
    )(dest.reshape(n_steps, 1, tm * TOP_K), h1p, zeros)


def _row_gather(src_hbm, idx_ref, dst_ref, sem, n_rows):
    def start(pair, _):
        for prio in range(2):
            r = 2 * pair + prio
            pltpu.make_async_copy(src_hbm.at[pl.ds(idx_ref[0, 0, r], 1), :], dst_ref.at[pl.ds(r, 1), :],
                                  sem).start(priority=prio)
        return 0
    lax.fori_loop(0, n_rows // 2, start, 0, unroll=4)


def _moe_kernel(bexp_ref, nused_ref, x_ref, wi_ref, bi_ref, wo_ref, bo_ref, y_ref):
    del bexp_ref
    i = pl.program_id(0)
    n_used = nused_ref[0]

    @pl.when(i < n_used)
    def _():
        x = _unpack_bf16_pairs(x_ref[...])
        hb = jnp.dot(x, wi_ref[...], preferred_element_type=F32) + bi_ref[...]
        g = jnp.minimum(hb[:, :D_FF], SWIGLU_LIMIT)
        u = jnp.clip(hb[:, D_FF:], -SWIGLU_LIMIT, SWIGLU_LIMIT)
        a = g * jax.nn.sigmoid(SWIGLU_ALPHA * g) * (u + 1.0)
        y_ref[...] = jnp.dot(a.astype(BF16), wo_ref[...], preferred_element_type=F32) + bo_ref[...]

    @pl.when(i >= n_used)
    def _():
        y_ref[...] = jnp.zeros_like(y_ref)


def _moe(block_exp, n_used, x_sorted, wi, bi, wo, bo):
    n_blocks = block_exp.shape[0]
    x_block = lambda i, be, nu: (jnp.minimum(i, nu[0] - 1), 0)
    grid_spec = pltpu.PrefetchScalarGridSpec(
        num_scalar_prefetch=2,
        grid=(n_blocks,),
        in_specs=[
            pl.BlockSpec((ROWS_MOE, D_MODEL // 2), x_block),
            pl.BlockSpec((None, D_MODEL, 2 * D_FF), lambda i, be, nu: (be[i], 0, 0)),
            pl.BlockSpec((None, 1, 2 * D_FF), lambda i, be, nu: (be[i], 0, 0)),
            pl.BlockSpec((None, D_FF, D_MODEL), lambda i, be, nu: (be[i], 0, 0)),
            pl.BlockSpec((None, 1, D_MODEL), lambda i, be, nu: (be[i], 0, 0)),
        ],
        out_specs=pl.BlockSpec((ROWS_MOE, D_MODEL), lambda i, be, nu: (i, 0)),
    )
    return pl.pallas_call(
        _moe_kernel,
        grid_spec=grid_spec,
        out_shape=jax.ShapeDtypeStruct((n_blocks * ROWS_MOE, D_MODEL), F32),
        compiler_params=_params(("arbitrary",)),
        name="moe_experts",
    )(block_exp, n_used, x_sorted, wi, bi, wo, bo)


def _combine_kernel(dest_cur_ref, dest_nxt_ref, gw_ref, h1_ref, g_ref, b_ref, y_hbm, o_ref, ybuf, sems):
    i = pl.program_id(0)
    n_steps = pl.num_programs(0)
    slot = i % 2
    rows = TOP_K * TM_COMB

    @pl.when(i == 0)
    def _():
        _row_gather(y_hbm, dest_cur_ref, ybuf.at[0], sems.at[0], rows)

    @pl.when(i + 1 < n_steps)
    def _():
        _row_gather(y_hbm, dest_nxt_ref, ybuf.at[1 - slot], sems.at[1 - slot], rows)

    pltpu.make_async_copy(y_hbm.at[pl.ds(0, rows), :], ybuf.at[slot], sems.at[slot]).wait()
    gw = gw_ref[...]
    f = jnp.zeros((TM_COMB, D_MODEL), F32)
    for k in range(TOP_K):
        f = f + gw[:, k:k + 1] * ybuf[slot, k * TM_COMB:(k + 1) * TM_COMB, :]
    o_ref[...] = _layer_norm(DEEPNORM_ALPHA * h1_ref[...] + f, g_ref[...], b_ref[...])


def _combine(dest_km, gate_w, h1, g, b, y_buf):
    n = h1.shape[0]
    n_steps = n // TM_COMB
    rows = TOP_K * TM_COMB
    return pl.pallas_call(
        _combine_kernel,
        grid=(n_steps,),
        in_specs=[
            pl.BlockSpec((1, 1, rows), lambda i: (i, 0, 0), memory_space=pltpu.SMEM),
            pl.BlockSpec((1, 1, rows), lambda i: (jnp.minimum(i + 1, n_steps - 1), 0, 0), memory_space=pltpu.SMEM),
            pl.BlockSpec((TM_COMB, TOP_K), lambda i: (i, 0)),
            pl.BlockSpec((TM_COMB, D_MODEL), lambda i: (i, 0)),
            pl.BlockSpec((1, D_MODEL), lambda i: (0, 0)),
            pl.BlockSpec((1, D_MODEL), lambda i: (0, 0)),
            pl.BlockSpec(memory_space=pl.ANY),
        ],
        out_specs=pl.BlockSpec((TM_COMB, D_MODEL), lambda i: (i, 0)),
        out_shape=jax.ShapeDtypeStruct((n, D_MODEL), F32),
        scratch_shapes=[pltpu.VMEM((2, rows, D_MODEL), F32), pltpu.SemaphoreType.DMA((2,))],
        compiler_params=_params(("arbitrary",)),
        name="combine_ln2",
    )(dest_km, dest_km, gate_w, h1, g, b, y_buf)


def _block_layout(top_idx, rank, counts):
    n_slots = top_idx.shape[0] * TOP_K
    padded = (counts + ROWS_MOE - 1) // ROWS_MOE * ROWS_MOE
    pend = jnp.cumsum(padded)
    pstart = pend - padded
    experts = jnp.arange(N_EXPERTS, dtype=jnp.int32)
    dest = rank + jnp.sum(jnp.where(top_idx[..., None] == experts, pstart, 0), axis=-1)
    n_blocks = n_slots // ROWS_MOE + N_EXPERTS
    block_start = jnp.arange(n_blocks, dtype=jnp.int32) * ROWS_MOE
    block_exp = jnp.minimum(jnp.sum(block_start[:, None] >= pend[None, :], axis=-1), N_EXPERTS - 1).astype(jnp.int32)
    n_used = (pend[-1] // ROWS_MOE).astype(jnp.int32).reshape(1)
    return dest.astype(jnp.int32), block_exp, n_used


def kernel(x, ln_in_g, ln_in_b, w_in, b_gate, lambda_q1, lambda_k1, lambda_q2, lambda_k2, subln_w, rel_bias,
           w_branch_a, w_branch_b, w_out, ln1_g, ln1_b, w_router, b_router, w_exp_in, b_exp_in, w_exp_out,
           b_exp_out, ln2_g, ln2_b):
    batch, seq, d = x.shape
    n = batch * seq
    row = lambda a: a.reshape(1, -1).astype(F32)
    l = 0
    h, qkv, gates = _ln_proj(x.reshape(n, d), row(ln_in_g), row(ln_in_b), w_in[l].astype(BF16), row(b_gate[l]))
    slopes = jnp.asarray([2.0 ** (-8.0 * (i + 1) / N_HEADS_A) for i in range(N_HEADS_A)], F32)
    out_a = _diff_attention(qkv, slopes, row(lambda_q1[l]), row(lambda_k1[l]), row(lambda_q2[l]),
                            row(lambda_k2[l]), subln_w[l].reshape(-1, 1).astype(F32), batch, seq)
    out_b = _band_attention(qkv, _band_bias(rel_bias[l]), batch, seq)
    h1, h1p, gate_w, top_idx, rank, counts = _merge(
        out_a, out_b, gates, h, w_branch_a[l].astype(BF16), w_branch_b[l].astype(BF16), w_out[l].astype(BF16),
        row(ln1_g[l]), row(ln1_b[l]), w_router[l].astype(F32), row(b_router[l]))
    dest, block_exp, n_used = _block_layout(top_idx, rank, counts[0])
    x_sorted = _dispatch(dest, h1p, block_exp.shape[0] * ROWS_MOE)
    y_buf = _moe(block_exp, n_used, x_sorted, w_exp_in[l].astype(BF16), b_exp_in[l].reshape(N_EXPERTS, 1, -1),
                 w_exp_out[l].astype(BF16), b_exp_out[l].reshape(N_EXPERTS, 1, -1))
    n_steps = n // TM_COMB
    dest_km = dest.reshape(n_steps, TM_COMB, TOP_K).transpose(0, 2, 1).reshape(n_steps, 1, TOP_K * TM_COMB)
    out = _combine(dest_km, gate_w, h1, row(ln2_g[l]), row(ln2_b[l]), y_buf)
    return out.reshape(batch, seq, d)
```

```python
import functools
import math

import jax
import jax.numpy as jnp
import numpy as np
from jax import lax
from jax.experimental import pallas as pl
from jax.experimental.pallas import tpu as pltpu

F32 = jnp.float32
BF16 = jnp.bfloat16

D_MODEL = 1024
CHUNK = 64
N_HEADS_A = 4
HEAD_DIM_A = 64
WIDTH_A = 512
N_HEADS_B = 8
HEAD_DIM_B = 64
WIDTH_B = 512
N_PREV_CHUNKS = 8
REL_CLIP = 128
N_EXPERTS = 32
TOP_K = 4
D_FF = 1024
SWIGLU_ALPHA = 1.702
SWIGLU_LIMIT = 7.0
DEEPNORM_ALPHA = 2.0 ** 0.25
LN_EPS = 1e-5
LAM_INIT = 0.8 - 0.6 * math.exp(-0.3 * 0)

LANES = 128
N_SLABS = (3 * WIDTH_A + 3 * WIDTH_B) // LANES
GATE_COLS = 2 * D_MODEL
IN_COLS = 3 * WIDTH_A + 3 * WIDTH_B + GATE_COLS
LOG2E = math.log2(math.e)
NEG = -1e30

TM_PROJ = 512
PROJ_CHUNK = 512
TQ = 256
TK = 256
KV_GROUP = 2
HEADS_PER_STEP = 2
PAIRS_PER_STEP = 2
BAND = 3 * TK
TM_MERGE = 256
TM_DISPATCH = 512
ROWS_MOE = 256
TM_COMB = 256
VMEM_LIMIT = 56 * 1024 * 1024


def _layer_norm(x, g, b):
    mu = jnp.mean(x, axis=-1, keepdims=True)
    xc = x - mu
    var = jnp.mean(xc * xc, axis=-1, keepdims=True)
    return xc * lax.rsqrt(var + LN_EPS) * g + b


def _params(sem):
    return pltpu.CompilerParams(dimension_semantics=sem, vmem_limit_bytes=VMEM_LIMIT)


def _ln_proj_kernel(x_ref, g_ref, b_ref, w_ref, bg_ref, h_ref, qkv_ref, gates_ref):
    h = _layer_norm(x_ref[...], g_ref[...], b_ref[...])
    h_ref[...] = h
    hb = h.astype(BF16)
    n_qkv_chunks = (N_SLABS * LANES) // PROJ_CHUNK
    slabs_per_chunk = PROJ_CHUNK // LANES
    q_scale = HEAD_DIM_A ** -0.5 * LOG2E
    for c in range(n_qkv_chunks):
        r = jnp.dot(hb, w_ref[:, c * PROJ_CHUNK:(c + 1) * PROJ_CHUNK], preferred_element_type=F32)
        first = c * slabs_per_chunk
        is_q = (first < WIDTH_A // LANES) or (3 * WIDTH_A // LANES <= first < (3 * WIDTH_A + WIDTH_B) // LANES)
        if is_q:
            r = r * q_scale
        for s in range(slabs_per_chunk):
            qkv_ref[first + s] = r[:, s * LANES:(s + 1) * LANES].astype(BF16)
    g0 = N_SLABS * LANES
    for c in range(GATE_COLS // PROJ_CHUNK):
        r = jnp.dot(hb, w_ref[:, g0 + c * PROJ_CHUNK:g0 + (c + 1) * PROJ_CHUNK], preferred_element_type=F32)
        r = r + bg_ref[:, c * PROJ_CHUNK:(c + 1) * PROJ_CHUNK]
        gates_ref[:, c * PROJ_CHUNK:(c + 1) * PROJ_CHUNK] = jax.nn.sigmoid(r).astype(BF16)


def _ln_proj(x2d, g, b, w_bf16, b_gate):
    n = x2d.shape[0]
    tm = min(TM_PROJ, n)
    const = lambda i: (0, 0)
    return pl.pallas_call(
        _ln_proj_kernel,
        grid=(n // tm,),
        in_specs=[
            pl.BlockSpec((tm, D_MODEL), lambda i: (i, 0)),
            pl.BlockSpec((1, D_MODEL), const),
            pl.BlockSpec((1, D_MODEL), const),
            pl.BlockSpec((D_MODEL, IN_COLS), const, pipeline_mode=pl.Buffered(1)),
            pl.BlockSpec((1, GATE_COLS), const),
        ],
        out_specs=[
            pl.BlockSpec((tm, D_MODEL), lambda i: (i, 0)),
            pl.BlockSpec((N_SLABS, tm, LANES), lambda i: (0, i, 0)),
            pl.BlockSpec((tm, GATE_COLS), lambda i: (i, 0)),
        ],
        out_shape=[
            jax.ShapeDtypeStruct((n, D_MODEL), F32),
            jax.ShapeDtypeStruct((N_SLABS, n, LANES), BF16),
            jax.ShapeDtypeStruct((n, GATE_COLS), BF16),
        ],
        compiler_params=_params(("parallel",)),
        name="ln_proj",
    )(x2d, g, b, w_bf16, b_gate)


def _stack_halves(q):
    lane = lax.broadcasted_iota(jnp.int32, q.shape, 1)
    zero = jnp.zeros_like(q)
    return jnp.concatenate([jnp.where(lane < 64, q, zero), jnp.where(lane >= 64, q, zero)], axis=0)


def _dot_nt(a, b):
    return lax.dot_general(a, b, (((1,), (1,)), ((), ())), preferred_element_type=F32)


def _split3_bf16(x):
    hi = x.astype(BF16)
    r1 = x - hi.astype(F32)
    mid = r1.astype(BF16)
    lo = (r1 - mid.astype(F32)).astype(BF16)
    return hi, mid, lo


class _DiffHead:
    def __init__(self, slope, i, q_ref, k_ref, v_ref, s_refs, m_ref, l_ref, acc_ref):
        self.slope, self.i = slope, i
        self.k_ref, self.v_ref, self.s_refs = k_ref, v_ref, s_refs
        self.m_ref, self.l_ref, self.acc_ref = m_ref, l_ref, acc_ref
        lane_q = lax.broadcasted_iota(jnp.int32, (2 * TQ, LANES), 1)
        self.q_aug = jnp.concatenate([_stack_halves(q_ref[...]), jnp.where(lane_q < 3, 1.0, 0.0).astype(BF16)],
                                     axis=1)
        lane_k = lax.broadcasted_iota(jnp.int32, (TK, LANES), 1)
        key_pos = lax.broadcasted_iota(jnp.int32, (TK, LANES), 0).astype(F32)
        hi, mid, lo = [t.astype(F32) for t in _split3_bf16(slope * key_pos)]
        self.k_bias = jnp.where(lane_k == 0, hi,
                                jnp.where(lane_k == 1, mid, jnp.where(lane_k == 2, lo, 0.0))).astype(BF16)
        m_ref[...] = jnp.full(m_ref.shape, NEG, F32)
        l_ref[...] = jnp.zeros(l_ref.shape, F32)
        acc_ref[...] = jnp.zeros(acc_ref.shape, F32)

    @staticmethod
    def _rows(jb):
        return pl.ds(pl.multiple_of(jb * TK, TK), TK)

    def scores_into(self, buf, a):
        for g in range(KV_GROUP):
            k_aug = jnp.concatenate([self.k_ref[self._rows(a * KV_GROUP + g), :], self.k_bias], axis=1)
            self.s_refs[buf][g * TK:(g + 1) * TK, :] = _dot_nt(k_aug, self.q_aug)

    def _block_offset(self, jb):
        return self.slope * ((jb - self.i) * TK).astype(F32)

    def _update(self, buf, a, tiles, offsets):
        s_ref = self.s_refs[buf]

        def score(g):
            s = s_ref[g * TK:(g + 1) * TK, :]
            return s if tiles is None else s + tiles[g]
        m_prev = self.m_ref[...]
        m_new = m_prev
        for g in range(KV_GROUP):
            m_new = jnp.maximum(m_new, jnp.max(score(g), axis=0, keepdims=True) + offsets[g])
        alpha = jnp.exp2(m_prev - m_new)
        l_new = alpha * self.l_ref[...]
        acc = alpha * self.acc_ref[...]
        for g in range(KV_GROUP):
            p = jnp.exp2(score(g) - (m_new - offsets[g]))
            l_new = l_new + jnp.sum(p, axis=0, keepdims=True)
            acc = acc + lax.dot_general(self.v_ref[self._rows(a * KV_GROUP + g), :], p.astype(BF16),
                                        (((0,), (0,)), ((), ())), preferred_element_type=F32)
        self.m_ref[...] = m_new
        self.l_ref[...] = l_new
        self.acc_ref[...] = acc

    def full_update(self, buf, a):
        self._update(buf, a, None, [self._block_offset(a * KV_GROUP + g) for g in range(KV_GROUP)])

    def last_update(self, buf, a):
        kk = lax.broadcasted_iota(jnp.int32, (TK, TQ), 0)
        qq = lax.broadcasted_iota(jnp.int32, (TK, TQ), 1)
        diag = jnp.where(kk // CHUNK <= qq // CHUNK, self.slope * (qq - jnp.abs(qq - kk) - kk).astype(F32), NEG)
        diag = jnp.concatenate([diag, diag], axis=1)
        j0 = a * KV_GROUP
        tiles = [jnp.where(j0 + g == self.i, diag, 0.0) for g in range(KV_GROUP)]
        offsets = [jnp.where(j0 + g > self.i, NEG, self._block_offset(j0 + g)) for g in range(KV_GROUP)]
        self._update(buf, a, tiles, offsets)

    def output(self, lam, subln):
        o = self.acc_ref[...] / self.l_ref[...]
        o = o[:, :TQ] - lam * o[:, TQ:]
        o = o * lax.rsqrt(jnp.mean(o * o, axis=0, keepdims=True) + LN_EPS) * subln
        return (o * (1.0 - LAM_INIT)).T.astype(BF16)


def _diff_attn_kernel(slopes_ref, lq1_ref, lk1_ref, lq2_ref, lk2_ref, subln_ref, *refs):
    hp = pl.program_id(1)
    i = pl.program_id(2)
    nh = HEADS_PER_STEP
    q_refs, k_refs, v_refs = refs[:nh], refs[nh:2 * nh], refs[2 * nh:3 * nh]
    o_ref = refs[3 * nh]
    scratch = refs[3 * nh + 1:]
    heads = []
    for t in range(nh):
        s0, s1, m, l, acc = scratch[5 * t:5 * t + 5]
        heads.append(_DiffHead(slopes_ref[hp * nh + t] * LOG2E, i, q_refs[t], k_refs[t], v_refs[t],
                               (s0, s1), m, l, acc))

    n_full = i // KV_GROUP
    odd = n_full % 2

    @pl.when(odd == 1)
    def _():
        for h in heads:
            h.scores_into(0, 0)
        for h in heads:
            h.full_update(0, 0)

    for h in heads:
        h.scores_into(0, odd)

    def pair(b, _):
        a = odd + 2 * b
        for h in heads:
            h.scores_into(1, a + 1)
        for h in heads:
            h.full_update(0, a)
        for h in heads:
            h.scores_into(0, a + 2)
        for h in heads:
            h.full_update(1, a + 1)
        return 0
    lax.fori_loop(0, (n_full - odd) // 2, pair, 0)

    for h in heads:
        h.last_update(0, n_full)

    lam = (jnp.exp(jnp.sum(lq1_ref[...] * lk1_ref[...], axis=-1, keepdims=True))
           - jnp.exp(jnp.sum(lq2_ref[...] * lk2_ref[...], axis=-1, keepdims=True)) + LAM_INIT)
    for t, h in enumerate(heads):
        o_ref[:, t * LANES:(t + 1) * LANES] = h.output(lam, subln_ref[...])


def _diff_attention(qkv, slopes, lq1, lk1, lq2, lk2, subln, batch, seq):
    nq = seq // TQ
    assert seq % (TQ * KV_GROUP) == 0, "the last key group of a query block must stay inside the sequence"
    n = batch * seq
    nh = HEADS_PER_STEP
    vec = lambda w: pl.BlockSpec((1, w), lambda b, h, i: (0, 0))
    q_spec = lambda t: pl.BlockSpec((None, TQ, LANES), lambda b, h, i: (h * nh + t, b * nq + i, 0))
    kv_spec = lambda t, base: pl.BlockSpec((None, seq, LANES), lambda b, h, i: (base + h * nh + t, b, 0))
    head_scratch = [pltpu.VMEM((KV_GROUP * TK, 2 * TQ), F32), pltpu.VMEM((KV_GROUP * TK, 2 * TQ), F32),
                    pltpu.VMEM((1, 2 * TQ), F32), pltpu.VMEM((1, 2 * TQ), F32), pltpu.VMEM((LANES, 2 * TQ), F32)]
    return pl.pallas_call(
        _diff_attn_kernel,
        grid=(batch, N_HEADS_A // nh, nq),
        in_specs=[
            pl.BlockSpec(memory_space=pltpu.SMEM),
            vec(HEAD_DIM_A), vec(HEAD_DIM_A), vec(HEAD_DIM_A), vec(HEAD_DIM_A),
            pl.BlockSpec((2 * HEAD_DIM_A, 1), lambda b, h, i: (0, 0)),
            *[q_spec(t) for t in range(nh)],
            *[kv_spec(t, N_HEADS_A) for t in range(nh)],
            *[kv_spec(t, 2 * N_HEADS_A) for t in range(nh)],
        ],
        out_specs=pl.BlockSpec((TQ, nh * LANES), lambda b, h, i: (b * nq + i, h)),
        out_shape=jax.ShapeDtypeStruct((n, WIDTH_A), BF16),
        scratch_shapes=head_scratch * nh,
        compiler_params=_params(("parallel", "parallel", "arbitrary")),
        name="diff_attn",
    )(slopes, lq1, lk1, lq2, lk2, subln, *([qkv] * (3 * nh)))


def _band_attn_kernel(*refs):
    i = pl.program_id(2)
    npair = PAIRS_PER_STEP
    bias_refs, q_refs = refs[:npair], refs[npair:2 * npair]
    k_refs, v_refs = refs[2 * npair:3 * npair], refs[3 * npair:4 * npair]
    o_ref = refs[4 * npair]
    n_sub = BAND // TK
    rows, offsets = [], []
    for jj in range(n_sub):
        start = i * TQ - N_PREV_CHUNKS * CHUNK + jj * TK
        offsets.append(jnp.where(start < 0, NEG, 0.0).astype(F32))
        rows.append(pl.ds(pl.multiple_of(jnp.maximum(start, 0), TK), TK))
    scores = [[_dot_nt(k_refs[t][rows[jj], :], _stack_halves(q_refs[t][...]))
               + bias_refs[t][jj * TK:(jj + 1) * TK, :] for jj in range(n_sub)] for t in range(npair)]
    feat = lax.broadcasted_iota(jnp.int32, (LANES, TQ), 0)
    for t in range(npair):
        m = functools.reduce(jnp.maximum, [jnp.max(s, axis=0, keepdims=True) + off
                                           for s, off in zip(scores[t], offsets)])
        l = jnp.zeros((1, 2 * TQ), F32)
        acc = jnp.zeros((LANES, 2 * TQ), F32)
        for jj in range(n_sub):
            p = jnp.exp2(scores[t][jj] - (m - offsets[jj]))
            l = l + jnp.sum(p, axis=0, keepdims=True)
            acc = acc + lax.dot_general(v_refs[t][rows[jj], :], p.astype(BF16), (((0,), (0,)), ((), ())),
                                        preferred_element_type=F32)
        o = acc / l
        o_ref[:, t * LANES:(t + 1) * LANES] = jnp.where(feat < 64, o[:, :TQ], o[:, TQ:]).T.astype(BF16)


BIAS_SPAN = 1024


def _band_bias_kernel(line_ref, o_ref):
    kj = lax.broadcasted_iota(jnp.int32, (BAND, TQ), 0)
    qi = lax.broadcasted_iota(jnp.int32, (BAND, TQ), 1)
    kc = kj // CHUNK - N_PREV_CHUNKS
    qc = qi // CHUNK
    allowed = jnp.logical_and(kc <= qc, kc >= qc - N_PREV_CHUNKS)
    for hh in range(2):
        line = jnp.broadcast_to(line_ref[hh:hh + 1, :], (BAND, BIAS_SPAN))
        rolled = pltpu.roll(line, BIAS_SPAN - (BAND - 1), 1, stride=1, stride_axis=0)
        o_ref[:, hh * TQ:(hh + 1) * TQ] = jnp.where(allowed, LOG2E * rolled[:, :TQ], NEG)


def _band_bias(rel_bias):
    rb = rel_bias.astype(F32)
    n_low = (BAND - 1) - REL_CLIP
    n_high = (BAND + TQ - 1) - (BAND - 1) - REL_CLIP - 1
    line = jnp.concatenate([jnp.broadcast_to(rb[:, :1], (N_HEADS_B, n_low)), rb,
                            jnp.broadcast_to(rb[:, -1:], (N_HEADS_B, n_high))], axis=1)
    line = jnp.pad(line[:, ::-1], ((0, 0), (0, BIAS_SPAN - line.shape[1])))
    pairs = N_HEADS_B // 2
    return pl.pallas_call(
        _band_bias_kernel,
        grid=(pairs,),
        in_specs=[pl.BlockSpec((None, 2, BIAS_SPAN), lambda p: (p, 0, 0))],
        out_specs=pl.BlockSpec((None, BAND, 2 * TQ), lambda p: (p, 0, 0)),
        out_shape=jax.ShapeDtypeStruct((pairs, BAND, 2 * TQ), F32),
        compiler_params=_params(("parallel",)),
        name="band_bias",
    )(line.reshape(pairs, 2, BIAS_SPAN))


def _band_attention(qkv, bias, batch, seq):
    nq = seq // TQ
    n = batch * seq
    base = 3 * WIDTH_A // LANES
    pairs = N_HEADS_B // 2
    npair = PAIRS_PER_STEP
    bias_spec = lambda t: pl.BlockSpec((None, BAND, 2 * TQ), lambda b, p, i: (p * npair + t, 0, 0))
    q_spec = lambda t: pl.BlockSpec((None, TQ, LANES), lambda b, p, i: (base + p * npair + t, b * nq + i, 0))
    kv_spec = lambda t, off: pl.BlockSpec((None, seq, LANES), lambda b, p, i: (base + off + p * npair + t, b, 0))
    return pl.pallas_call(
        _band_attn_kernel,
        grid=(batch, pairs // npair, nq),
        in_specs=[
            *[bias_spec(t) for t in range(npair)],
            *[q_spec(t) for t in range(npair)],
            *[kv_spec(t, pairs) for t in range(npair)],
            *[kv_spec(t, 2 * pairs) for t in range(npair)],
        ],
        out_specs=pl.BlockSpec((TQ, npair * LANES), lambda b, p, i: (b * nq + i, p)),
        out_shape=jax.ShapeDtypeStruct((n, WIDTH_B), BF16),
        compiler_params=_params(("parallel", "parallel", "arbitrary")),
        name="band_attn",
    )(*([bias] * npair), *([qkv] * (3 * npair)))


def _pack_bf16_pairs(x):
    w = x.shape[1] // 2
    bits = pltpu.bitcast(x.astype(BF16).astype(F32), jnp.uint32)
    return (bits[:, :w] >> 16) | (bits[:, w:] & jnp.uint32(0xFFFF0000))


def _unpack_bf16_pairs(p):
    lo = pltpu.bitcast(p << 16, F32)
    hi = pltpu.bitcast(p & jnp.uint32(0xFFFF0000), F32)
    return jnp.concatenate([lo, hi], axis=1).astype(BF16)


def _columns(cols, dtype):
    lane = lax.broadcasted_iota(jnp.int32, (cols[0].shape[0], TOP_K), 1)
    out = jnp.zeros(lane.shape, dtype)
    for k, c in enumerate(cols):
        out = jnp.where(lane == k, c.astype(dtype), out)
    return out


def _merge_kernel(oa_ref, ob_ref, gates_ref, h_ref, pa_ref, pb_ref, wo_ref, g_ref, b_ref, wr_ref, br_ref,
                  h1_ref, h1p_ref, gw_ref, idx_ref, rank_ref, counts_ref, seen_ref):
    ma = jnp.dot(oa_ref[...], pa_ref[...], preferred_element_type=F32)
    mb = jnp.dot(ob_ref[...], pb_ref[...], preferred_element_type=F32)
    merged = gates_ref[:, :D_MODEL].astype(F32) * ma + gates_ref[:, D_MODEL:].astype(F32) * mb
    m = jnp.dot(merged.astype(BF16), wo_ref[...], preferred_element_type=F32)
    h1 = _layer_norm(DEEPNORM_ALPHA * h_ref[...] + m, g_ref[...], b_ref[...])
    h1_ref[...] = h1
    h1p_ref[...] = _pack_bf16_pairs(h1)
    logits = jnp.dot(h1, wr_ref[...], preferred_element_type=F32, precision=lax.Precision.HIGHEST) + br_ref[...]

    tm = logits.shape[0]
    expert = lax.broadcasted_iota(jnp.int32, (tm, N_EXPERTS), 1)
    vals = logits
    top_val, top_idx = [], []
    for _ in range(TOP_K):
        mx = jnp.max(vals, axis=-1, keepdims=True)
        sel = jnp.min(jnp.where(vals == mx, expert, N_EXPERTS), axis=-1, keepdims=True)
        top_val.append(mx)
        top_idx.append(sel)
        vals = jnp.where(expert == sel, -jnp.inf, vals)
    ex = [jnp.exp(v - top_val[0]) for v in top_val]
    denom = functools.reduce(jnp.add, ex)
    gw_ref[...] = _columns([e / denom for e in ex], F32)
    idx_ref[...] = _columns(top_idx, jnp.int32)

    @pl.when(pl.program_id(0) == 0)
    def _():
        seen_ref[...] = jnp.zeros_like(seen_ref)

    chosen = functools.reduce(jnp.logical_or, [expert == s for s in top_idx])
    onehot = jnp.where(chosen, 1.0, 0.0).astype(BF16)
    r_i = lax.broadcasted_iota(jnp.int32, (tm, tm), 0)
    c_i = lax.broadcasted_iota(jnp.int32, (tm, tm), 1)
    earlier = jnp.where(c_i < r_i, 1.0, 0.0).astype(BF16)
    before = jnp.dot(earlier, onehot, preferred_element_type=F32) + seen_ref[...]
    rank_ref[...] = _columns([jnp.sum(jnp.where(expert == s, before, 0.0), axis=-1, keepdims=True)
                              for s in top_idx], jnp.int32)
    seen_ref[...] = seen_ref[...] + jnp.sum(onehot.astype(F32), axis=0, keepdims=True)
    counts_ref[...] = seen_ref[...].astype(jnp.int32)


def _merge(oa, ob, gates, h, pa, pb, wo, g, b, wr, br):
    n = h.shape[0]
    tm = min(TM_MERGE, n)
    row = lambda w: pl.BlockSpec((tm, w), lambda i: (i, 0))
    full = lambda r, c: pl.BlockSpec((r, c), lambda i: (0, 0))
    return pl.pallas_call(
        _merge_kernel,
        grid=(n // tm,),
        in_specs=[row(WIDTH_A), row(WIDTH_B), row(GATE_COLS), row(D_MODEL),
                  full(WIDTH_A, D_MODEL), full(WIDTH_B, D_MODEL), full(D_MODEL, D_MODEL),
                  full(1, D_MODEL), full(1, D_MODEL), full(D_MODEL, N_EXPERTS), full(1, N_EXPERTS)],
        out_specs=[row(D_MODEL), row(D_MODEL // 2), row(TOP_K), row(TOP_K), row(TOP_K), full(1, N_EXPERTS)],
        out_shape=[jax.ShapeDtypeStruct((n, D_MODEL), F32), jax.ShapeDtypeStruct((n, D_MODEL // 2), jnp.uint32),
                   jax.ShapeDtypeStruct((n, TOP_K), F32), jax.ShapeDtypeStruct((n, TOP_K), jnp.int32),
                   jax.ShapeDtypeStruct((n, TOP_K), jnp.int32), jax.ShapeDtypeStruct((1, N_EXPERTS), jnp.int32)],
        scratch_shapes=[pltpu.VMEM((1, N_EXPERTS), F32)],
        compiler_params=_params(("arbitrary",)),
        name="merge_ln1_router",
    )(oa, ob, gates, h, pa, pb, wo, g, b, wr, br)


def _dispatch_kernel(dest_ref, h1p_ref, zeros_hbm, xs_hbm, sem):
    del zeros_hbm

    def start(t, _):
        for k in range(TOP_K):
            pltpu.make_async_copy(h1p_ref.at[pl.ds(t, 1), :],
                                  xs_hbm.at[pl.ds(dest_ref[0, 0, t * TOP_K + k], 1), :], sem).start(priority=k % 2)
        return 0
    lax.fori_loop(0, TM_DISPATCH, start, 0, unroll=2)

    for k in range(TOP_K):
        pltpu.make_async_copy(h1p_ref, xs_hbm.at[pl.ds(0, TM_DISPATCH), :], sem).wait()


def _dispatch(dest, h1p, n_rows):
    n = h1p.shape[0]
    tm = min(TM_DISPATCH, n)
    assert tm == TM_DISPATCH
    n_steps = n // tm
    zeros = jnp.zeros((n_rows, D_MODEL // 2), jnp.uint32)
    return pl.pallas_call(
        _dispatch_kernel,
        grid=(n_steps,),
        in_specs=[
            pl.BlockSpec((1, 1, tm * TOP_K), lambda i: (i, 0, 0), memory_space=pltpu.SMEM),
            pl.BlockSpec((tm, D_MODEL // 2), lambda i: (i, 0)),
            pl.BlockSpec(memory_space=pl.ANY),
        ],
        out_specs=pl.BlockSpec(memory_space=pl.ANY),
        out_shape=jax.ShapeDtypeStruct((n_rows, D_MODEL // 2), jnp.uint32),
        scratch_shapes=[pltpu.SemaphoreType.DMA(())],
        input_output_aliases={2: 0},
        compiler_params=_params(("arbitrary",)),
        name="dispatch_rows",
    )(dest.reshape(n_steps, 1, tm * TOP_K), h1p, zeros)


def _row_gather(src_hbm, idx_ref, dst_ref, sem, n_rows):
    def start(pair, _):
        for prio in range(2):
            r = 2 * pair + prio
            pltpu.make_async_copy(src_hbm.at[pl.ds(idx_ref[0, 0, r], 1), :], dst_ref.at[pl.ds(r, 1), :],
                                  sem).start(priority=prio)
        return 0
    lax.fori_loop(0, n_rows // 2, start, 0, unroll=4)


def _moe_kernel(bexp_ref, nused_ref, x_ref, wi_ref, bi_ref, wo_ref, bo_ref, y_ref):
    del bexp_ref
    i = pl.program_id(0)
    n_used = nused_ref[0]

    @pl.when(i < n_used)
    def _():
        x = _unpack_bf16_pairs(x_ref[...])
        hb = jnp.dot(x, wi_ref[...], preferred_element_type=F32) + bi_ref[...]
        g = jnp.minimum(hb[:, :D_FF], SWIGLU_LIMIT)
        u = jnp.clip(hb[:, D_FF:], -SWIGLU_LIMIT, SWIGLU_LIMIT)
        a = g * jax.nn.sigmoid(SWIGLU_ALPHA * g) * (u + 1.0)
        y_ref[...] = jnp.dot(a.astype(BF16), wo_ref[...], preferred_element_type=F32) + bo_ref[...]

    @pl.when(i >= n_used)
    def _():
        y_ref[...] = jnp.zeros_like(y_ref)


def _moe(block_exp, n_used, x_sorted, wi, bi, wo, bo):
    n_blocks = block_exp.shape[0]
    x_block = lambda i, be, nu: (jnp.minimum(i, nu[0] - 1), 0)
    grid_spec = pltpu.PrefetchScalarGridSpec(
        num_scalar_prefetch=2,
        grid=(n_blocks,),
        in_specs=[
            pl.BlockSpec((ROWS_MOE, D_MODEL // 2), x_block),
            pl.BlockSpec((None, D_MODEL, 2 * D_FF), lambda i, be, nu: (be[i], 0, 0)),
            pl.BlockSpec((None, 1, 2 * D_FF), lambda i, be, nu: (be[i], 0, 0)),
            pl.BlockSpec((None, D_FF, D_MODEL), lambda i, be, nu: (be[i], 0, 0)),
            pl.BlockSpec((None, 1, D_MODEL), lambda i, be, nu: (be[i], 0, 0)),
        ],
        out_specs=pl.BlockSpec((ROWS_MOE, D_MODEL), lambda i, be, nu: (i, 0)),
    )
    return pl.pallas_call(
        _moe_kernel,
        grid_spec=grid_spec,
        out_shape=jax.ShapeDtypeStruct((n_blocks * ROWS_MOE, D_MODEL), F32),
        compiler_params=_params(("arbitrary",)),
        name="moe_experts",
    )(block_exp, n_used, x_sorted, wi, bi, wo, bo)


def _combine_kernel(dest_cur_ref, dest_nxt_ref, gw_ref, h1_ref, g_ref, b_ref, y_hbm, o_ref, ybuf, sems):
    i = pl.program_id(0)
    n_steps = pl.num_programs(0)
    slot = i % 2
    rows = TOP_K * TM_COMB

    @pl.when(i == 0)
    def _():
        _row_gather(y_hbm, dest_cur_ref, ybuf.at[0], sems.at[0], rows)

    @pl.when(i + 1 < n_steps)
    def _():
        _row_gather(y_hbm, dest_nxt_ref, ybuf.at[1 - slot], sems.at[1 - slot], rows)

    pltpu.make_async_copy(y_hbm.at[pl.ds(0, rows), :], ybuf.at[slot], sems.at[slot]).wait()
    gw = gw_ref[...]
    f = jnp.zeros((TM_COMB, D_MODEL), F32)
    for k in range(TOP_K):
        f = f + gw[:, k:k + 1] * ybuf[slot, k * TM_COMB:(k + 1) * TM_COMB, :]
    o_ref[...] = _layer_norm(DEEPNORM_ALPHA * h1_ref[...] + f, g_ref[...], b_ref[...])


def _combine(dest_km, gate_w, h1, g, b, y_buf):
    n = h1.shape[0]
    n_steps = n // TM_COMB
    rows = TOP_K * TM_COMB
    return pl.pallas_call(
        _combine_kernel,
        grid=(n_steps,),
        in_specs=[
            pl.BlockSpec((1, 1, rows), lambda i: (i, 0, 0), memory_space=pltpu.SMEM),
            pl.BlockSpec((1, 1, rows), lambda i: (jnp.minimum(i + 1, n_steps - 1), 0, 0), memory_space=pltpu.SMEM),
            pl.BlockSpec((TM_COMB, TOP_K), lambda i: (i, 0)),
            pl.BlockSpec((TM_COMB, D_MODEL), lambda i: (i, 0)),
            pl.BlockSpec((1, D_MODEL), lambda i: (0, 0)),
            pl.BlockSpec((1, D_MODEL), lambda i: (0, 0)),
            pl.BlockSpec(memory_space=pl.ANY),
        ],
        out_specs=pl.BlockSpec((TM_COMB, D_MODEL), lambda i: (i, 0)),
        out_shape=jax.ShapeDtypeStruct((n, D_MODEL), F32),
        scratch_shapes=[pltpu.VMEM((2, rows, D_MODEL), F32), pltpu.SemaphoreType.DMA((2,))],
        compiler_params=_params(("arbitrary",)),
        name="combine_ln2",
    )(dest_km, dest_km, gate_w, h1, g, b, y_buf)


def _block_layout(top_idx, rank, counts):
    n_slots = top_idx.shape[0] * TOP_K
    padded = (counts + ROWS_MOE - 1) // ROWS_MOE * ROWS_MOE
    pend = jnp.cumsum(padded)
    pstart = pend - padded
    experts = jnp.arange(N_EXPERTS, dtype=jnp.int32)
    dest = rank + jnp.sum(jnp.where(top_idx[..., None] == experts, pstart, 0), axis=-1)
    n_blocks = n_slots // ROWS_MOE + N_EXPERTS
    block_start = jnp.arange(n_blocks, dtype=jnp.int32) * ROWS_MOE
    block_exp = jnp.minimum(jnp.sum(block_start[:, None] >= pend[None, :], axis=-1), N_EXPERTS - 1).astype(jnp.int32)
    n_used = (pend[-1] // ROWS_MOE).astype(jnp.int32).reshape(1)
    return dest.astype(jnp.int32), block_exp, n_used


def kernel(x, ln_in_g, ln_in_b, w_in, b_gate, lambda_q1, lambda_k1, lambda_q2, lambda_k2, subln_w, rel_bias,
           w_branch_a, w_branch_b, w_out, ln1_g, ln1_b, w_router, b_router, w_exp_in, b_exp_in, w_exp_out,
           b_exp_out, ln2_g, ln2_b):
    batch, seq, d = x.shape
    n = batch * seq
    row = lambda a: a.reshape(1, -1).astype(F32)
    l = 0
    h, qkv, gates = _ln_proj(x.reshape(n, d), row(ln_in_g), row(ln_in_b), w_in[l].astype(BF16), row(b_gate[l]))
    slopes = jnp.asarray([2.0 ** (-8.0 * (i + 1) / N_HEADS_A) for i in range(N_HEADS_A)], F32)
    out_a = _diff_attention(qkv, slopes, row(lambda_q1[l]), row(lambda_k1[l]), row(lambda_q2[l]),
                            row(lambda_k2[l]), subln_w[l].reshape(-1, 1).astype(F32), batch, seq)
    out_b = _band_attention(qkv, _band_bias(rel_bias[l]), batch, seq)
    h1, h1p, gate_w, top_idx, rank, counts = _merge(
        out_a, out_b, gates, h, w_branch_a[l].astype(BF16), w_branch_b[l].astype(BF16), w_out[l].astype(BF16),
        row(ln1_g[l]), row(ln1_b[l]), w_router[l].astype(F32), row(b_router[l]))
    dest, block_exp, n_used = _block_layout(top_idx, rank, counts[0])
    x_sorted = _dispatch(dest, h1p, block_exp.shape[0] * ROWS_MOE)
    y_buf = _moe(block_exp, n_used, x_sorted, w_exp_in[l].astype(BF16), b_exp_in[l].reshape(N_EXPERTS, 1, -1),
                 w_exp_out[l].astype(BF16), b_exp_out[l].reshape(N_EXPERTS, 1, -1))
    n_steps = n // TM_COMB
    dest_km = dest.reshape(n_steps, TM_COMB, TOP_K).transpose(0, 2, 1).reshape(n_steps, 1, TOP_K * TM_COMB)
    out = _combine(dest_km, gate_w, h1, row(ln2_g[l]), row(ln2_b[l]), y_buf)
    return out.reshape(batch, seq, d)
```

```python
import functools
import math

import jax
import jax.numpy as jnp
import numpy as np
from jax import lax
from jax.experimental import pallas as pl
from jax.experimental.pallas import tpu as pltpu

F32 = jnp.float32
BF16 = jnp.bfloat16

D_MODEL = 1024
CHUNK = 64
N_HEADS_A = 4
HEAD_DIM_A = 64
WIDTH_A = 512
N_HEADS_B = 8
HEAD_DIM_B = 64
WIDTH_B = 512
N_PREV_CHUNKS = 8
REL_CLIP = 128
N_EXPERTS = 32
TOP_K = 4
D_FF = 1024
SWIGLU_ALPHA = 1.702
SWIGLU_LIMIT = 7.0
DEEPNORM_ALPHA = 2.0 ** 0.25
LN_EPS = 1e-5
LAM_INIT = 0.8 - 0.6 * math.exp(-0.3 * 0)

LANES = 128
N_SLABS = (3 * WIDTH_A + 3 * WIDTH_B) // LANES
GATE_COLS = 2 * D_MODEL
IN_COLS = 3 * WIDTH_A + 3 * WIDTH_B + GATE_COLS
LOG2E = math.log2(math.e)
NEG = -1e30

TM_PROJ = 512
PROJ_CHUNK = 512
TQ = 256
TK = 256
KV_GROUP = 2
HEADS_PER_STEP = 2
PAIRS_PER_STEP = 2
BAND = 3 * TK
TM_MERGE = 256
TM_DISPATCH = 512
ROWS_MOE = 256
TM_COMB = 256
VMEM_LIMIT = 56 * 1024 * 1024


def _layer_norm(x, g, b):
    mu = jnp.mean(x, axis=-1, keepdims=True)
    xc = x - mu
    var = jnp.mean(xc * xc, axis=-1, keepdims=True)
    return xc * lax.rsqrt(var + LN_EPS) * g + b


def _params(sem):
    return pltpu.CompilerParams(dimension_semantics=sem, vmem_limit_bytes=VMEM_LIMIT)


def _ln_proj_kernel(x_ref, g_ref, b_ref, w_ref, bg_ref, h_ref, qkv_ref, gates_ref):
    h = _layer_norm(x_ref[...], g_ref[...], b_ref[...])
    h_ref[...] = h
    hb = h.astype(BF16)
    n_qkv_chunks = (N_SLABS * LANES) // PROJ_CHUNK
    slabs_per_chunk = PROJ_CHUNK // LANES
    q_scale = HEAD_DIM_A ** -0.5 * LOG2E
    for c in range(n_qkv_chunks):
        r = jnp.dot(hb, w_ref[:, c * PROJ_CHUNK:(c + 1) * PROJ_CHUNK], preferred_element_type=F32)
        first = c * slabs_per_chunk
        is_q = (first < WIDTH_A // LANES) or (3 * WIDTH_A // LANES <= first < (3 * WIDTH_A + WIDTH_B) // LANES)
        if is_q:
            r = r * q_scale
        for s in range(slabs_per_chunk):
            qkv_ref[first + s] = r[:, s * LANES:(s + 1) * LANES].astype(BF16)
    g0 = N_SLABS * LANES
    for c in range(GATE_COLS // PROJ_CHUNK):
        r = jnp.dot(hb, w_ref[:, g0 + c * PROJ_CHUNK:g0 + (c + 1) * PROJ_CHUNK], preferred_element_type=F32)
        r = r + bg_ref[:, c * PROJ_CHUNK:(c + 1) * PROJ_CHUNK]
        gates_ref[:, c * PROJ_CHUNK:(c + 1) * PROJ_CHUNK] = jax.nn.sigmoid(r).astype(BF16)


def _ln_proj(x2d, g, b, w_bf16, b_gate):
    n = x2d.shape[0]
    tm = min(TM_PROJ, n)
    const = lambda i: (0, 0)
    return pl.pallas_call(
        _ln_proj_kernel,
        grid=(n // tm,),
        in_specs=[
            pl.BlockSpec((tm, D_MODEL), lambda i: (i, 0)),
            pl.BlockSpec((1, D_MODEL), const),
            pl.BlockSpec((1, D_MODEL), const),
            pl.BlockSpec((D_MODEL, IN_COLS), const, pipeline_mode=pl.Buffered(1)),
            pl.BlockSpec((1, GATE_COLS), const),
        ],
        out_specs=[
            pl.BlockSpec((tm, D_MODEL), lambda i: (i, 0)),
            pl.BlockSpec((N_SLABS, tm, LANES), lambda i: (0, i, 0)),
            pl.BlockSpec((tm, GATE_COLS), lambda i: (i, 0)),
        ],
        out_shape=[
            jax.ShapeDtypeStruct((n, D_MODEL), F32),
            jax.ShapeDtypeStruct((N_SLABS, n, LANES), BF16),
            jax.ShapeDtypeStruct((n, GATE_COLS), BF16),
        ],
        compiler_params=_params(("parallel",)),
        name="ln_proj",
    )(x2d, g, b, w_bf16, b_gate)


def _stack_halves(q):
    lane = lax.broadcasted_iota(jnp.int32, q.shape, 1)
    zero = jnp.zeros_like(q)
    return jnp.concatenate([jnp.where(lane < 64, q, zero), jnp.where(lane >= 64, q, zero)], axis=0)


def _dot_nt(a, b):
    return lax.dot_general(a, b, (((1,), (1,)), ((), ())), preferred_element_type=F32)


def _split3_bf16(x):
    hi = x.astype(BF16)
    r1 = x - hi.astype(F32)
    mid = r1.astype(BF16)
    lo = (r1 - mid.astype(F32)).astype(BF16)
    return hi, mid, lo


class _DiffHead:
    def __init__(self, slope, i, q_ref, k_ref, v_ref, s_refs, m_ref, l_ref, acc_ref):
        self.slope, self.i = slope, i
        self.k_ref, self.v_ref, self.s_refs = k_ref, v_ref, s_refs
        self.m_ref, self.l_ref, self.acc_ref = m_ref, l_ref, acc_ref
        lane_q = lax.broadcasted_iota(jnp.int32, (2 * TQ, LANES), 1)
        self.q_aug = jnp.concatenate([_stack_halves(q_ref[...]), jnp.where(lane_q < 3, 1.0, 0.0).astype(BF16)],
                                     axis=1)
        lane_k = lax.broadcasted_iota(jnp.int32, (TK, LANES), 1)
        key_pos = lax.broadcasted_iota(jnp.int32, (TK, LANES), 0).astype(F32)
        hi, mid, lo = [t.astype(F32) for t in _split3_bf16(slope * key_pos)]
        self.k_bias = jnp.where(lane_k == 0, hi,
                                jnp.where(lane_k == 1, mid, jnp.where(lane_k == 2, lo, 0.0))).astype(BF16)
        m_ref[...] = jnp.full(m_ref.shape, NEG, F32)
        l_ref[...] = jnp.zeros(l_ref.shape, F32)
        acc_ref[...] = jnp.zeros(acc_ref.shape, F32)

    @staticmethod
    def _rows(jb):
        return pl.ds(pl.multiple_of(jb * TK, TK), TK)

    def scores_into(self, buf, a):
        for g in range(KV_GROUP):
            k_aug = jnp.concatenate([self.k_ref[self._rows(a * KV_GROUP + g), :], self.k_bias], axis=1)
            self.s_refs[buf][g * TK:(g + 1) * TK, :] = _dot_nt(k_aug, self.q_aug)

    def _block_offset(self, jb):
        return self.slope * ((jb - self.i) * TK).astype(F32)

    def _update(self, buf, a, tiles, offsets):
        s_ref = self.s_refs[buf]

        def score(g):
            s = s_ref[g * TK:(g + 1) * TK, :]
            return s if tiles is None else s + tiles[g]
        m_prev = self.m_ref[...]
        m_new = m_prev
        for g in range(KV_GROUP):
            m_new = jnp.maximum(m_new, jnp.max(score(g), axis=0, keepdims=True) + offsets[g])
        alpha = jnp.exp2(m_prev - m_new)
        l_new = alpha * self.l_ref[...]
        acc = alpha * self.acc_ref[...]
        for g in range(KV_GROUP):
            p = jnp.exp2(score(g) - (m_new - offsets[g]))
            l_new = l_new + jnp.sum(p, axis=0, keepdims=True)
            acc = acc + lax.dot_general(self.v_ref[self._rows(a * KV_GROUP + g), :], p.astype(BF16),
                                        (((0,), (0,)), ((), ())), preferred_element_type=F32)
        self.m_ref[...] = m_new
        self.l_ref[...] = l_new
        self.acc_ref[...] = acc

    def full_update(self, buf, a):
        self._update(buf, a, None, [self._block_offset(a * KV_GROUP + g) for g in range(KV_GROUP)])

    def last_update(self, buf, a):
        kk = lax.broadcasted_iota(jnp.int32, (TK, TQ), 0)
        qq = lax.broadcasted_iota(jnp.int32, (TK, TQ), 1)
        diag = jnp.where(kk // CHUNK <= qq // CHUNK, self.slope * (qq - jnp.abs(qq - kk) - kk).astype(F32), NEG)
        diag = jnp.concatenate([diag, diag], axis=1)
        j0 = a * KV_GROUP
        tiles = [jnp.where(j0 + g == self.i, diag, 0.0) for g in range(KV_GROUP)]
        offsets = [jnp.where(j0 + g > self.i, NEG, self._block_offset(j0 + g)) for g in range(KV_GROUP)]
        self._update(buf, a, tiles, offsets)

    def output(self, lam, subln):
        o = self.acc_ref[...] / self.l_ref[...]
        o = o[:, :TQ] - lam * o[:, TQ:]
        o = o * lax.rsqrt(jnp.mean(o * o, axis=0, keepdims=True) + LN_EPS) * subln
        return (o * (1.0 - LAM_INIT)).T.astype(BF16)


def _diff_attn_kernel(slopes_ref, lq1_ref, lk1_ref, lq2_ref, lk2_ref, subln_ref, *refs):
    hp = pl.program_id(1)
    i = pl.program_id(2)
    nh = HEADS_PER_STEP
    q_refs, k_refs, v_refs = refs[:nh], refs[nh:2 * nh], refs[2 * nh:3 * nh]
    o_ref = refs[3 * nh]
    scratch = refs[3 * nh + 1:]
    heads = []
    for t in range(nh):
        s0, s1, m, l, acc = scratch[5 * t:5 * t + 5]
        heads.append(_DiffHead(slopes_ref[hp * nh + t] * LOG2E, i, q_refs[t], k_refs[t], v_refs[t],
                               (s0, s1), m, l, acc))

    n_full = i // KV_GROUP
    odd = n_full % 2

    @pl.when(odd == 1)
    def _():
        for h in heads:
            h.scores_into(0, 0)
        for h in heads:
            h.full_update(0, 0)

    for h in heads:
        h.scores_into(0, odd)

    def pair(b, _):
        a = odd + 2 * b
        for h in heads:
            h.scores_into(1, a + 1)
        for h in heads:
            h.full_update(0, a)
        for h in heads:
            h.scores_into(0, a + 2)
        for h in heads:
            h.full_update(1, a + 1)
        return 0
    lax.fori_loop(0, (n_full - odd) // 2, pair, 0)

    for h in heads:
        h.last_update(0, n_full)

    lam = (jnp.exp(jnp.sum(lq1_ref[...] * lk1_ref[...], axis=-1, keepdims=True))
           - jnp.exp(jnp.sum(lq2_ref[...] * lk2_ref[...], axis=-1, keepdims=True)) + LAM_INIT)
    for t, h in enumerate(heads):
        o_ref[:, t * LANES:(t + 1) * LANES] = h.output(lam, subln_ref[...])


def _diff_attention(qkv, slopes, lq1, lk1, lq2, lk2, subln, batch, seq):
    nq = seq // TQ
    assert seq % (TQ * KV_GROUP) == 0, "the last key group of a query block must stay inside the sequence"
    n = batch * seq
    nh = HEADS_PER_STEP
    vec = lambda w: pl.BlockSpec((1, w), lambda b, h, i: (0, 0))
    q_spec = lambda t: pl.BlockSpec((None, TQ, LANES), lambda b, h, i: (h * nh + t, b * nq + i, 0))
    kv_spec = lambda t, base: pl.BlockSpec((None, seq, LANES), lambda b, h, i: (base + h * nh + t, b, 0))
    head_scratch = [pltpu.VMEM((KV_GROUP * TK, 2 * TQ), F32), pltpu.VMEM((KV_GROUP * TK, 2 * TQ), F32),
                    pltpu.VMEM((1, 2 * TQ), F32), pltpu.VMEM((1, 2 * TQ), F32), pltpu.VMEM((LANES, 2 * TQ), F32)]
    return pl.pallas_call(
        _diff_attn_kernel,
        grid=(batch, N_HEADS_A // nh, nq),
        in_specs=[
            pl.BlockSpec(memory_space=pltpu.SMEM),
            vec(HEAD_DIM_A), vec(HEAD_DIM_A), vec(HEAD_DIM_A), vec(HEAD_DIM_A),
            pl.BlockSpec((2 * HEAD_DIM_A, 1), lambda b, h, i: (0, 0)),
            *[q_spec(t) for t in range(nh)],
            *[kv_spec(t, N_HEADS_A) for t in range(nh)],
            *[kv_spec(t, 2 * N_HEADS_A) for t in range(nh)],
        ],
        out_specs=pl.BlockSpec((TQ, nh * LANES), lambda b, h, i: (b * nq + i, h)),
        out_shape=jax.ShapeDtypeStruct((n, WIDTH_A), BF16),
        scratch_shapes=head_scratch * nh,
        compiler_params=_params(("parallel", "parallel", "arbitrary")),
        name="diff_attn",
    )(slopes, lq1, lk1, lq2, lk2, subln, *([qkv] * (3 * nh)))


def _band_attn_kernel(*refs):
    i = pl.program_id(2)
    npair = PAIRS_PER_STEP
    bias_refs, q_refs = refs[:npair], refs[npair:2 * npair]
    k_refs, v_refs = refs[2 * npair:3 * npair], refs[3 * npair:4 * npair]
    o_ref = refs[4 * npair]
    n_sub = BAND // TK
    rows, offsets = [], []
    for jj in range(n_sub):
        start = i * TQ - N_PREV_CHUNKS * CHUNK + jj * TK
        offsets.append(jnp.where(start < 0, NEG, 0.0).astype(F32))
        rows.append(pl.ds(pl.multiple_of(jnp.maximum(start, 0), TK), TK))
    scores = [[_dot_nt(k_refs[t][rows[jj], :], _stack_halves(q_refs[t][...]))
               + bias_refs[t][jj * TK:(jj + 1) * TK, :] for jj in range(n_sub)] for t in range(npair)]
    feat = lax.broadcasted_iota(jnp.int32, (LANES, TQ), 0)
    for t in range(npair):
        m = functools.reduce(jnp.maximum, [jnp.max(s, axis=0, keepdims=True) + off
                                           for s, off in zip(scores[t], offsets)])
        l = jnp.zeros((1, 2 * TQ), F32)
        acc = jnp.zeros((LANES, 2 * TQ), F32)
        for jj in range(n_sub):
            p = jnp.exp2(scores[t][jj] - (m - offsets[jj]))
            l = l + jnp.sum(p, axis=0, keepdims=True)
            acc = acc + lax.dot_general(v_refs[t][rows[jj], :], p.astype(BF16), (((0,), (0,)), ((), ())),
                                        preferred_element_type=F32)
        o = acc / l
        o_ref[:, t * LANES:(t + 1) * LANES] = jnp.where(feat < 64, o[:, :TQ], o[:, TQ:]).T.astype(BF16)


BIAS_SPAN = 1024


def _band_bias_kernel(line_ref, o_ref):
    kj = lax.broadcasted_iota(jnp.int32, (BAND, TQ), 0)
    qi = lax.broadcasted_iota(jnp.int32, (BAND, TQ), 1)
    kc = kj // CHUNK - N_PREV_CHUNKS
    qc = qi // CHUNK
    allowed = jnp.logical_and(kc <= qc, kc >= qc - N_PREV_CHUNKS)
    for hh in range(2):
        line = jnp.broadcast_to(line_ref[hh:hh + 1, :], (BAND, BIAS_SPAN))
        rolled = pltpu.roll(line, BIAS_SPAN - (BAND - 1), 1, stride=1, stride_axis=0)
        o_ref[:, hh * TQ:(hh + 1) * TQ] = jnp.where(allowed, LOG2E * rolled[:, :TQ], NEG)


def _band_bias(rel_bias):
    rb = rel_bias.astype(F32)
    n_low = (BAND - 1) - REL_CLIP
    n_high = (BAND + TQ - 1) - (BAND - 1) - REL_CLIP - 1
    line = jnp.concatenate([jnp.broadcast_to(rb[:, :1], (N_HEADS_B, n_low)), rb,
                            jnp.broadcast_to(rb[:, -1:], (N_HEADS_B, n_high))], axis=1)
    line = jnp.pad(line[:, ::-1], ((0, 0), (0, BIAS_SPAN - line.shape[1])))
    pairs = N_HEADS_B // 2
    return pl.pallas_call(
        _band_bias_kernel,
        grid=(pairs,),
        in_specs=[pl.BlockSpec((None, 2, BIAS_SPAN), lambda p: (p, 0, 0))],
        out_specs=pl.BlockSpec((None, BAND, 2 * TQ), lambda p: (p, 0, 0)),
        out_shape=jax.ShapeDtypeStruct((pairs, BAND, 2 * TQ), F32),
        compiler_params=_params(("parallel",)),
        name="band_bias",
    )(line.reshape(pairs, 2, BIAS_SPAN))


def _band_attention(qkv, bias, batch, seq):
    nq = seq // TQ
    n = batch * seq
    base = 3 * WIDTH_A // LANES
    pairs = N_HEADS_B // 2
    npair = PAIRS_PER_STEP
    bias_spec = lambda t: pl.BlockSpec((None, BAND, 2 * TQ), lambda b, p, i: (p * npair + t, 0, 0))
    q_spec = lambda t: pl.BlockSpec((None, TQ, LANES), lambda b, p, i: (base + p * npair + t, b * nq + i, 0))
    kv_spec = lambda t, off: pl.BlockSpec((None, seq, LANES), lambda b, p, i: (base + off + p * npair + t, b, 0))
    return pl.pallas_call(
        _band_attn_kernel,
        grid=(batch, pairs // npair, nq),
        in_specs=[
            *[bias_spec(t) for t in range(npair)],
            *[q_spec(t) for t in range(npair)],
            *[kv_spec(t, pairs) for t in range(npair)],
            *[kv_spec(t, 2 * pairs) for t in range(npair)],
        ],
        out_specs=pl.BlockSpec((TQ, npair * LANES), lambda b, p, i: (b * nq + i, p)),
        out_shape=jax.ShapeDtypeStruct((n, WIDTH_B), BF16),
        compiler_params=_params(("parallel", "parallel", "arbitrary")),
        name="band_attn",
    )(*([bias] * npair), *([qkv] * (3 * npair)))


def _pack_bf16_pairs(x):
    w = x.shape[1] // 2
    bits = pltpu.bitcast(x.astype(BF16).astype(F32), jnp.uint32)
    return (bits[:, :w] >> 16) | (bits[:, w:] & jnp.uint32(0xFFFF0000))


def _unpack_bf16_pairs(p):
    lo = pltpu.bitcast(p << 16, F32)
    hi = pltpu.bitcast(p & jnp.uint32(0xFFFF0000), F32)
    return jnp.concatenate([lo, hi], axis=1).astype(BF16)


def _merge_kernel(oa_ref, ob_ref, gates_ref, h_ref, pa_ref, pb_ref, wo_ref, g_ref, b_ref, wra_ref, wrb_ref, br_ref,
                  h1_ref, h1p_ref, gw_ref, idx_ref, rank_ref, counts_ref, seen_ref):
    ma = jnp.dot(oa_ref[...], pa_ref[...], preferred_element_type=F32)
    mb = jnp.dot(ob_ref[...], pb_ref[...], preferred_element_type=F32)
    merged = gates_ref[:, :D_MODEL].astype(F32) * ma + gates_ref[:, D_MODEL:].astype(F32) * mb
    m = jnp.dot(merged.astype(BF16), wo_ref[...], preferred_element_type=F32)
    h1 = _layer_norm(DEEPNORM_ALPHA * h_ref[...] + m, g_ref[...], b_ref[...])
    h1_ref[...] = h1
    h1p_ref[...] = _pack_bf16_pairs(h1)

    h_hi = h1.astype(BF16)
    h_lo = (h1 - h_hi.astype(F32)).astype(BF16)
    part_a = _dot_nt(wra_ref[...], h_hi)
    part_b = _dot_nt(wrb_ref[...], h_lo)
    logits = part_a[:N_EXPERTS] + part_a[N_EXPERTS:2 * N_EXPERTS] + part_b[:N_EXPERTS] + br_ref[...]

    tm = logits.shape[1]
    expert = lax.broadcasted_iota(jnp.int32, (N_EXPERTS, tm), 0)
    vals = logits
    top_val, top_idx = [], []
    for _ in range(TOP_K):
        mx = jnp.max(vals, axis=0, keepdims=True)
        sel = jnp.min(jnp.where(vals == mx, expert, N_EXPERTS), axis=0, keepdims=True)
        top_val.append(mx)
        top_idx.append(sel)
        vals = jnp.where(expert == sel, -jnp.inf, vals)
    ex = [jnp.exp(v - top_val[0]) for v in top_val]
    denom = functools.reduce(jnp.add, ex)
    gw_ref[...] = jnp.concatenate([e / denom for e in ex], axis=0)
    idx_ref[...] = jnp.concatenate(top_idx, axis=0)

    @pl.when(pl.program_id(0) == 0)
    def _():
        seen_ref[...] = jnp.zeros_like(seen_ref)

    chosen = functools.reduce(jnp.logical_or, [expert == s for s in top_idx])
    onehot = jnp.where(chosen, 1.0, 0.0)
    r_i = lax.broadcasted_iota(jnp.int32, (tm, tm), 0)
    c_i = lax.broadcasted_iota(jnp.int32, (tm, tm), 1)
    earlier = jnp.where(r_i < c_i, 1.0, 0.0)
    before = jnp.dot(onehot, earlier, preferred_element_type=F32) + seen_ref[...]
    rank_ref[...] = jnp.concatenate([jnp.sum(jnp.where(expert == s, before, 0.0), axis=0, keepdims=True)
                                     for s in top_idx], axis=0).astype(jnp.int32)
    seen_ref[...] = seen_ref[...] + jnp.sum(onehot, axis=1, keepdims=True)
    counts_ref[...] = seen_ref[...].astype(jnp.int32)


def _merge(oa, ob, gates, h, pa, pb, wo, g, b, wr, br):
    n = h.shape[0]
    tm = min(TM_MERGE, n)
    wr_hi = wr.astype(BF16)
    wr_lo = (wr - wr_hi.astype(F32)).astype(BF16)
    pad = lambda a: jnp.pad(a, ((0, 0), (0, LANES - a.shape[1]))).T
    row = lambda w: pl.BlockSpec((tm, w), lambda i: (i, 0))
    full = lambda r, c: pl.BlockSpec((r, c), lambda i: (0, 0))
    slot_row = pl.BlockSpec((TOP_K, tm), lambda i: (0, i))
    return pl.pallas_call(
        _merge_kernel,
        grid=(n // tm,),
        in_specs=[row(WIDTH_A), row(WIDTH_B), row(GATE_COLS), row(D_MODEL),
                  full(WIDTH_A, D_MODEL), full(WIDTH_B, D_MODEL), full(D_MODEL, D_MODEL),
                  full(1, D_MODEL), full(1, D_MODEL), full(LANES, D_MODEL), full(LANES, D_MODEL),
                  full(N_EXPERTS, 1)],
        out_specs=[row(D_MODEL), row(D_MODEL // 2), slot_row, slot_row, slot_row, full(N_EXPERTS, 1)],
        out_shape=[jax.ShapeDtypeStruct((n, D_MODEL), F32), jax.ShapeDtypeStruct((n, D_MODEL // 2), jnp.uint32),
                   jax.ShapeDtypeStruct((TOP_K, n), F32), jax.ShapeDtypeStruct((TOP_K, n), jnp.int32),
                   jax.ShapeDtypeStruct((TOP_K, n), jnp.int32), jax.ShapeDtypeStruct((N_EXPERTS, 1), jnp.int32)],
        scratch_shapes=[pltpu.VMEM((N_EXPERTS, 1), F32)],
        compiler_params=_params(("arbitrary",)),
        name="merge_ln1_router",
    )(oa, ob, gates, h, pa, pb, wo, g, b, pad(jnp.concatenate([wr_hi, wr_lo], axis=1)), pad(wr_hi),
      br.reshape(N_EXPERTS, 1))


def _dispatch_kernel(pend_ref, dest_ref, h1p_ref, xs_hbm, zero_ref, sem, zero_sem):
    @pl.when(pl.program_id(0) == 0)
    def _():
        zero_ref[...] = jnp.zeros_like(zero_ref)

        def fill(row0):
            block = pl.ds(pl.multiple_of(row0, ROWS_MOE), ROWS_MOE)
            return pltpu.make_async_copy(zero_ref, xs_hbm.at[block, :], zero_sem)
        for e in range(N_EXPERTS):
            fill(jnp.maximum(pend_ref[e] - ROWS_MOE, 0)).start()
        for e in range(N_EXPERTS):
            fill(0).wait()

        first_unused = pend_ref[N_EXPERTS - 1] // ROWS_MOE
        n_blocks = xs_hbm.shape[0] // ROWS_MOE

        @pl.loop(first_unused, n_blocks)
        def _(blk):
            fill(blk * ROWS_MOE).start()

        @pl.loop(first_unused, n_blocks)
        def _(blk):
            fill(0).wait()

    def start(t, _):
        for k in range(TOP_K):
            pltpu.make_async_copy(h1p_ref.at[pl.ds(t, 1), :],
                                  xs_hbm.at[pl.ds(dest_ref[0, 0, t * TOP_K + k], 1), :], sem).start(priority=k % 2)
        return 0
    lax.fori_loop(0, TM_DISPATCH, start, 0, unroll=2)

    for k in range(TOP_K):
        pltpu.make_async_copy(h1p_ref, xs_hbm.at[pl.ds(0, TM_DISPATCH), :], sem).wait()


def _dispatch(dest, pend, h1p, n_rows):
    n = h1p.shape[0]
    tm = min(TM_DISPATCH, n)
    assert tm == TM_DISPATCH
    n_steps = n // tm
    grid_spec = pltpu.PrefetchScalarGridSpec(
        num_scalar_prefetch=1,
        grid=(n_steps,),
        in_specs=[
            pl.BlockSpec((1, 1, tm * TOP_K), lambda i, pe: (i, 0, 0), memory_space=pltpu.SMEM),
            pl.BlockSpec((tm, D_MODEL // 2), lambda i, pe: (i, 0)),
        ],
        out_specs=pl.BlockSpec(memory_space=pl.ANY),
        scratch_shapes=[pltpu.VMEM((ROWS_MOE, D_MODEL // 2), jnp.uint32), pltpu.SemaphoreType.DMA(()),
                        pltpu.SemaphoreType.DMA(())],
    )
    return pl.pallas_call(
        _dispatch_kernel,
        grid_spec=grid_spec,
        out_shape=jax.ShapeDtypeStruct((n_rows, D_MODEL // 2), jnp.uint32),
        compiler_params=_params(("arbitrary",)),
        name="dispatch_rows",
    )(pend, dest.reshape(n_steps, 1, tm * TOP_K), h1p)


def _row_gather(src_hbm, idx_ref, dst_ref, sem, n_rows):
    def start(pair, _):
        for prio in range(2):
            r = 2 * pair + prio
            pltpu.make_async_copy(src_hbm.at[pl.ds(idx_ref[0, 0, r], 1), :], dst_ref.at[pl.ds(r, 1), :],
                                  sem).start(priority=prio)
        return 0
    lax.fori_loop(0, n_rows // 2, start, 0, unroll=4)


def _moe_kernel(bexp_ref, nused_ref, x_ref, wi_ref, bi_ref, wo_ref, bo_ref, y_ref, wi_bf, wo_bf):
    i = pl.program_id(0)
    n_used = nused_ref[0]
    new_expert = jnp.logical_or(i == 0, bexp_ref[i] != bexp_ref[jnp.maximum(i - 1, 0)])

    @pl.when(jnp.logical_and(new_expert, i < n_used))
    def _():
        wi_bf[...] = wi_ref[...].astype(BF16)
        wo_bf[...] = wo_ref[...].astype(BF16)

    @pl.when(i < n_used)
    def _():
        x = _unpack_bf16_pairs(x_ref[...])
        hb = jnp.dot(x, wi_bf[...], preferred_element_type=F32) + bi_ref[...]
        g = jnp.minimum(hb[:, :D_FF], SWIGLU_LIMIT)
        u = jnp.clip(hb[:, D_FF:], -SWIGLU_LIMIT, SWIGLU_LIMIT)
        a = g * jax.nn.sigmoid(SWIGLU_ALPHA * g) * (u + 1.0)
        y_ref[...] = jnp.dot(a.astype(BF16), wo_bf[...], preferred_element_type=F32) + bo_ref[...]

    @pl.when(i >= n_used)
    def _():
        y_ref[...] = jnp.zeros_like(y_ref)


def _moe(block_exp, n_used, x_sorted, wi, bi, wo, bo):
    n_blocks = block_exp.shape[0]
    x_block = lambda i, be, nu: (jnp.minimum(i, nu[0] - 1), 0)
    grid_spec = pltpu.PrefetchScalarGridSpec(
        num_scalar_prefetch=2,
        grid=(n_blocks,),
        in_specs=[
            pl.BlockSpec((ROWS_MOE, D_MODEL // 2), x_block),
            pl.BlockSpec((None, D_MODEL, 2 * D_FF), lambda i, be, nu: (be[i], 0, 0)),
            pl.BlockSpec((None, 1, 2 * D_FF), lambda i, be, nu: (be[i], 0, 0)),
            pl.BlockSpec((None, D_FF, D_MODEL), lambda i, be, nu: (be[i], 0, 0)),
            pl.BlockSpec((None, 1, D_MODEL), lambda i, be, nu: (be[i], 0, 0)),
        ],
        out_specs=pl.BlockSpec((ROWS_MOE, D_MODEL), lambda i, be, nu: (i, 0)),
        scratch_shapes=[pltpu.VMEM((D_MODEL, 2 * D_FF), BF16), pltpu.VMEM((D_FF, D_MODEL), BF16)],
    )
    return pl.pallas_call(
        _moe_kernel,
        grid_spec=grid_spec,
        out_shape=jax.ShapeDtypeStruct((n_blocks * ROWS_MOE, D_MODEL), F32),
        compiler_params=_params(("arbitrary",)),
        name="moe_experts",
    )(block_exp, n_used, x_sorted, wi, bi, wo, bo)


def _combine_kernel(dest_cur_ref, dest_nxt_ref, gw_ref, h1_ref, g_ref, b_ref, y_hbm, o_ref, ybuf, sems):
    i = pl.program_id(0)
    n_steps = pl.num_programs(0)
    slot = i % 2
    rows = TOP_K * TM_COMB

    @pl.when(i == 0)
    def _():
        _row_gather(y_hbm, dest_cur_ref, ybuf.at[0], sems.at[0], rows)

    @pl.when(i + 1 < n_steps)
    def _():
        _row_gather(y_hbm, dest_nxt_ref, ybuf.at[1 - slot], sems.at[1 - slot], rows)

    pltpu.make_async_copy(y_hbm.at[pl.ds(0, rows), :], ybuf.at[slot], sems.at[slot]).wait()
    gw = gw_ref[...]
    f = jnp.zeros((TM_COMB, D_MODEL), F32)
    for k in range(TOP_K):
        f = f + gw[:, k:k + 1] * ybuf[slot, k * TM_COMB:(k + 1) * TM_COMB, :]
    o_ref[...] = _layer_norm(DEEPNORM_ALPHA * h1_ref[...] + f, g_ref[...], b_ref[...])


def _combine(dest_km, gate_w, h1, g, b, y_buf):
    n = h1.shape[0]
    n_steps = n // TM_COMB
    rows = TOP_K * TM_COMB
    return pl.pallas_call(
        _combine_kernel,
        grid=(n_steps,),
        in_specs=[
            pl.BlockSpec((1, 1, rows), lambda i: (i, 0, 0), memory_space=pltpu.SMEM),
            pl.BlockSpec((1, 1, rows), lambda i: (jnp.minimum(i + 1, n_steps - 1), 0, 0), memory_space=pltpu.SMEM),
            pl.BlockSpec((TM_COMB, TOP_K), lambda i: (i, 0)),
            pl.BlockSpec((TM_COMB, D_MODEL), lambda i: (i, 0)),
            pl.BlockSpec((1, D_MODEL), lambda i: (0, 0)),
            pl.BlockSpec((1, D_MODEL), lambda i: (0, 0)),
            pl.BlockSpec(memory_space=pl.ANY),
        ],
        out_specs=pl.BlockSpec((TM_COMB, D_MODEL), lambda i: (i, 0)),
        out_shape=jax.ShapeDtypeStruct((n, D_MODEL), F32),
        scratch_shapes=[pltpu.VMEM((2, rows, D_MODEL), F32), pltpu.SemaphoreType.DMA((2,))],
        compiler_params=_params(("arbitrary",)),
        name="combine_ln2",
    )(dest_km, dest_km, gate_w, h1, g, b, y_buf)


def _block_layout(top_idx, rank, counts):
    n_slots = top_idx.shape[1] * TOP_K
    padded = (counts + ROWS_MOE - 1) // ROWS_MOE * ROWS_MOE
    pend = jnp.cumsum(padded)
    pstart = pend - padded
    experts = jnp.arange(N_EXPERTS, dtype=jnp.int32)[:, None, None]
    dest = rank + jnp.sum(jnp.where(top_idx[None] == experts, pstart[:, None, None], 0), axis=0)
    n_blocks = n_slots // ROWS_MOE + N_EXPERTS
    block_start = jnp.arange(n_blocks, dtype=jnp.int32) * ROWS_MOE
    block_exp = jnp.minimum(jnp.sum(block_start[:, None] >= pend[None, :], axis=-1), N_EXPERTS - 1).astype(jnp.int32)
    n_used = (pend[-1] // ROWS_MOE).astype(jnp.int32).reshape(1)
    return dest.astype(jnp.int32), pend.astype(jnp.int32), block_exp, n_used


def kernel(x, ln_in_g, ln_in_b, w_in, b_gate, lambda_q1, lambda_k1, lambda_q2, lambda_k2, subln_w, rel_bias,
           w_branch_a, w_branch_b, w_out, ln1_g, ln1_b, w_router, b_router, w_exp_in, b_exp_in, w_exp_out,
           b_exp_out, ln2_g, ln2_b):
    batch, seq, d = x.shape
    n = batch * seq
    row = lambda a: a.reshape(1, -1).astype(F32)
    l = 0
    h, qkv, gates = _ln_proj(x.reshape(n, d), row(ln_in_g), row(ln_in_b), w_in[l].astype(BF16), row(b_gate[l]))
    slopes = jnp.asarray([2.0 ** (-8.0 * (i + 1) / N_HEADS_A) for i in range(N_HEADS_A)], F32)
    out_a = _diff_attention(qkv, slopes, row(lambda_q1[l]), row(lambda_k1[l]), row(lambda_q2[l]),
                            row(lambda_k2[l]), subln_w[l].reshape(-1, 1).astype(F32), batch, seq)
    out_b = _band_attention(qkv, _band_bias(rel_bias[l]), batch, seq)
    h1, h1p, gate_w, top_idx, rank, counts = _merge(
        out_a, out_b, gates, h, w_branch_a[l].astype(BF16), w_branch_b[l].astype(BF16), w_out[l].astype(BF16),
        row(ln1_g[l]), row(ln1_b[l]), w_router[l].astype(F32), row(b_router[l]))
    dest, pend, block_exp, n_used = _block_layout(top_idx, rank, counts[:, 0])
    x_sorted = _dispatch(dest.T, pend, h1p, block_exp.shape[0] * ROWS_MOE)
    y_buf = _moe(block_exp, n_used, x_sorted, w_exp_in[l].astype(F32), b_exp_in[l].reshape(N_EXPERTS, 1, -1),
                 w_exp_out[l].astype(F32), b_exp_out[l].reshape(N_EXPERTS, 1, -1))
    n_steps = n // TM_COMB
    dest_km = dest.reshape(TOP_K, n_steps, TM_COMB).transpose(1, 0, 2).reshape(n_steps, 1, TOP_K * TM_COMB)
    out = _combine(dest_km, gate_w.T, h1, row(ln2_g[l]), row(ln2_b[l]), y_buf)
    return out.reshape(batch, seq, d)
```

```python
import functools
import math

import jax
import jax.numpy as jnp
import numpy as np
from jax import lax
from jax.experimental import pallas as pl
from jax.experimental.pallas import tpu as pltpu

F32 = jnp.float32
BF16 = jnp.bfloat16

D_MODEL = 1024
CHUNK = 64
N_HEADS_A = 4
HEAD_DIM_A = 64
WIDTH_A = 512
N_HEADS_B = 8
HEAD_DIM_B = 64
WIDTH_B = 512
N_PREV_CHUNKS = 8
REL_CLIP = 128
N_EXPERTS = 32
TOP_K = 4
D_FF = 1024
SWIGLU_ALPHA = 1.702
SWIGLU_LIMIT = 7.0
DEEPNORM_ALPHA = 2.0 ** 0.25
LN_EPS = 1e-5
LAM_INIT = 0.8 - 0.6 * math.exp(-0.3 * 0)

LANES = 128
SUBLANES = 8
N_SLABS = (3 * WIDTH_A + 3 * WIDTH_B) // LANES
GATE_COLS = 2 * D_MODEL
IN_COLS = 3 * WIDTH_A + 3 * WIDTH_B + GATE_COLS
LOG2E = math.log2(math.e)
NEG = -1e30

TM_PROJ = 512
PROJ_CHUNK = 512
TQ = 256
TK = 256
KV_GROUP = 2
HEADS_PER_STEP = 2
PAIRS_PER_STEP = 2
BAND = 3 * TK
TM_MERGE = 512
TM_DISPATCH = 512
ROWS_MOE = 512
TM_COMB = 256
VMEM_LIMIT = 56 * 1024 * 1024


def _layer_norm(x, g, b):
    mu = jnp.mean(x, axis=-1, keepdims=True)
    xc = x - mu
    var = jnp.mean(xc * xc, axis=-1, keepdims=True)
    return xc * lax.rsqrt(var + LN_EPS) * g + b


def _params(sem):
    return pltpu.CompilerParams(dimension_semantics=sem, vmem_limit_bytes=VMEM_LIMIT)


def _ln_proj_kernel(x_ref, g_ref, b_ref, w_ref, bg_ref, h_ref, qkv_ref, gates_ref):
    h = _layer_norm(x_ref[...], g_ref[...], b_ref[...])
    h_ref[...] = h
    hb = h.astype(BF16)
    n_qkv_chunks = (N_SLABS * LANES) // PROJ_CHUNK
    slabs_per_chunk = PROJ_CHUNK // LANES
    q_scale = HEAD_DIM_A ** -0.5 * LOG2E
    for c in range(n_qkv_chunks):
        r = jnp.dot(hb, w_ref[:, c * PROJ_CHUNK:(c + 1) * PROJ_CHUNK], preferred_element_type=F32)
        first = c * slabs_per_chunk
        is_q = (first < WIDTH_A // LANES) or (3 * WIDTH_A // LANES <= first < (3 * WIDTH_A + WIDTH_B) // LANES)
        if is_q:
            r = r * q_scale
        for s in range(slabs_per_chunk):
            qkv_ref[first + s] = r[:, s * LANES:(s + 1) * LANES].astype(BF16)
    g0 = N_SLABS * LANES
    for c in range(GATE_COLS // PROJ_CHUNK):
        r = jnp.dot(hb, w_ref[:, g0 + c * PROJ_CHUNK:g0 + (c + 1) * PROJ_CHUNK], preferred_element_type=F32)
        r = r + bg_ref[:, c * PROJ_CHUNK:(c + 1) * PROJ_CHUNK]
        gates_ref[:, c * PROJ_CHUNK:(c + 1) * PROJ_CHUNK] = jax.nn.sigmoid(r).astype(BF16)


def _ln_proj(x2d, g, b, w_bf16, b_gate):
    n = x2d.shape[0]
    tm = min(TM_PROJ, n)
    const = lambda i: (0, 0)
    return pl.pallas_call(
        _ln_proj_kernel,
        grid=(n // tm,),
        in_specs=[
            pl.BlockSpec((tm, D_MODEL), lambda i: (i, 0)),
            pl.BlockSpec((1, D_MODEL), const),
            pl.BlockSpec((1, D_MODEL), const),
            pl.BlockSpec((D_MODEL, IN_COLS), const, pipeline_mode=pl.Buffered(1)),
            pl.BlockSpec((1, GATE_COLS), const),
        ],
        out_specs=[
            pl.BlockSpec((tm, D_MODEL), lambda i: (i, 0)),
            pl.BlockSpec((N_SLABS, tm, LANES), lambda i: (0, i, 0)),
            pl.BlockSpec((tm, GATE_COLS), lambda i: (i, 0)),
        ],
        out_shape=[
            jax.ShapeDtypeStruct((n, D_MODEL), F32),
            jax.ShapeDtypeStruct((N_SLABS, n, LANES), BF16),
            jax.ShapeDtypeStruct((n, GATE_COLS), BF16),
        ],
        compiler_params=_params(("parallel",)),
        name="ln_proj",
    )(x2d, g, b, w_bf16, b_gate)


def _stack_halves(q):
    lane = lax.broadcasted_iota(jnp.int32, q.shape, 1)
    zero = jnp.zeros_like(q)
    return jnp.concatenate([jnp.where(lane < 64, q, zero), jnp.where(lane >= 64, q, zero)], axis=0)


def _dot_nt(a, b):
    return lax.dot_general(a, b, (((1,), (1,)), ((), ())), preferred_element_type=F32)


def _split3_bf16(x):
    hi = x.astype(BF16)
    r1 = x - hi.astype(F32)
    mid = r1.astype(BF16)
    lo = (r1 - mid.astype(F32)).astype(BF16)
    return hi, mid, lo


class _DiffHead:
    def __init__(self, slope, i, q_ref, k_ref, v_ref, s_refs, mx_refs, m_ref, l_ref, acc_ref):
        self.slope, self.i = slope, i
        self.k_ref, self.v_ref, self.s_refs, self.mx_refs = k_ref, v_ref, s_refs, mx_refs
        self.m_ref, self.l_ref, self.acc_ref = m_ref, l_ref, acc_ref
        lane_q = lax.broadcasted_iota(jnp.int32, (2 * TQ, LANES), 1)
        self.q_aug = jnp.concatenate([_stack_halves(q_ref[...]), jnp.where(lane_q < 3, 1.0, 0.0).astype(BF16)],
                                     axis=1)
        lane_k = lax.broadcasted_iota(jnp.int32, (TK, LANES), 1)
        key_pos = lax.broadcasted_iota(jnp.int32, (TK, LANES), 0)
        self.k_bias = []
        for g in range(KV_GROUP):
            hi, mid, lo = [t.astype(F32) for t in _split3_bf16(slope * (key_pos + g * TK).astype(F32))]
            self.k_bias.append(jnp.where(lane_k == 0, hi, jnp.where(lane_k == 1, mid, jnp.where(lane_k == 2, lo, 0.0))
                                         ).astype(BF16))
        m_ref[...] = jnp.full(m_ref.shape, NEG, F32)
        l_ref[...] = jnp.zeros(l_ref.shape, F32)
        acc_ref[...] = jnp.zeros(acc_ref.shape, F32)

    @staticmethod
    def _group_rows(a):
        return pl.ds(pl.multiple_of(a * (KV_GROUP * TK), KV_GROUP * TK), KV_GROUP * TK)

    def scores_into(self, buf, a):
        mx = None
        for g in range(KV_GROUP):
            rows = pl.ds(pl.multiple_of((a * KV_GROUP + g) * TK, TK), TK)
            s = _dot_nt(jnp.concatenate([self.k_ref[rows, :], self.k_bias[g]], axis=1), self.q_aug)
            self.s_refs[buf][g * TK:(g + 1) * TK, :] = s
            mg = jnp.max(s, axis=0, keepdims=True)
            mx = mg if mx is None else jnp.maximum(mx, mg)
        self.mx_refs[buf][...] = mx

    def _group_offset(self, a):
        return self.slope * ((a * KV_GROUP - self.i) * TK).astype(F32)

    def _update(self, a, s, mx):
        off = self._group_offset(a)
        m_prev = self.m_ref[...]
        m_new = jnp.maximum(m_prev, mx + off)
        alpha = jnp.exp2(m_prev - m_new)
        p = jnp.exp2(s - (m_new - off))
        self.m_ref[...] = m_new
        self.l_ref[...] = alpha * self.l_ref[...] + jnp.sum(p, axis=0, keepdims=True)
        self.acc_ref[...] = alpha * self.acc_ref[...] + lax.dot_general(
            self.v_ref[self._group_rows(a), :], p.astype(BF16), (((0,), (0,)), ((), ())),
            preferred_element_type=F32)

    def full_update(self, buf, a):
        self._update(a, self.s_refs[buf][...], self.mx_refs[buf][...])

    def last_update(self, buf, a):
        kk = lax.broadcasted_iota(jnp.int32, (TK, TQ), 0)
        qq = lax.broadcasted_iota(jnp.int32, (TK, TQ), 1)
        diag = jnp.where(kk // CHUNK <= qq // CHUNK, self.slope * (qq - jnp.abs(qq - kk) - kk).astype(F32), NEG)
        tiles = []
        for g in range(KV_GROUP):
            jb = a * KV_GROUP + g
            tile = jnp.where(jb == self.i, diag, jnp.where(jb > self.i, NEG, 0.0))
            tiles.append(jnp.concatenate([tile, tile], axis=1))
        s = self.s_refs[buf][...] + jnp.concatenate(tiles, axis=0)
        self._update(a, s, jnp.max(s, axis=0, keepdims=True))

    def output(self, lam, subln):
        o = self.acc_ref[...] / self.l_ref[...]
        o = o[:, :TQ] - lam * o[:, TQ:]
        o = o * lax.rsqrt(jnp.mean(o * o, axis=0, keepdims=True) + LN_EPS) * subln
        return (o * (1.0 - LAM_INIT)).T.astype(BF16)


def _diff_attn_kernel(slopes_ref, lq1_ref, lk1_ref, lq2_ref, lk2_ref, subln_ref, *refs):
    hp = pl.program_id(1)
    i = pl.program_id(2)
    nh = HEADS_PER_STEP
    q_refs, k_refs, v_refs = refs[:nh], refs[nh:2 * nh], refs[2 * nh:3 * nh]
    o_ref = refs[3 * nh]
    scratch = refs[3 * nh + 1:]
    heads = []
    for t in range(nh):
        s0, s1, mx0, mx1, m, l, acc = scratch[7 * t:7 * t + 7]
        heads.append(_DiffHead(slopes_ref[hp * nh + t] * LOG2E, i, q_refs[t], k_refs[t], v_refs[t],
                               (s0, s1), (mx0, mx1), m, l, acc))

    n_full = i // KV_GROUP
    odd = n_full % 2

    @pl.when(odd == 1)
    def _():
        for h in heads:
            h.scores_into(0, 0)
        for h in heads:
            h.full_update(0, 0)

    for h in heads:
        h.scores_into(0, odd)

    def pair(b, _):
        a = odd + 2 * b
        for h in heads:
            h.scores_into(1, a + 1)
        for h in heads:
            h.full_update(0, a)
        for h in heads:
            h.scores_into(0, a + 2)
        for h in heads:
            h.full_update(1, a + 1)
        return 0
    lax.fori_loop(0, (n_full - odd) // 2, pair, 0)

    for h in heads:
        h.last_update(0, n_full)

    lam = (jnp.exp(jnp.sum(lq1_ref[...] * lk1_ref[...], axis=-1, keepdims=True))
           - jnp.exp(jnp.sum(lq2_ref[...] * lk2_ref[...], axis=-1, keepdims=True)) + LAM_INIT)
    for t, h in enumerate(heads):
        o_ref[:, t * LANES:(t + 1) * LANES] = h.output(lam, subln_ref[...])


def _diff_attention(qkv, slopes, lq1, lk1, lq2, lk2, subln, batch, seq):
    nq = seq // TQ
    assert seq % (TQ * KV_GROUP) == 0, "the last key group of a query block must stay inside the sequence"
    n = batch * seq
    nh = HEADS_PER_STEP
    vec = lambda w: pl.BlockSpec((1, w), lambda b, h, i: (0, 0))
    q_spec = lambda t: pl.BlockSpec((None, TQ, LANES), lambda b, h, i: (h * nh + t, b * nq + i, 0))
    kv_spec = lambda t, base: pl.BlockSpec((None, seq, LANES), lambda b, h, i: (base + h * nh + t, b, 0))
    stat = pltpu.VMEM((1, 2 * TQ), F32)
    head_scratch = [pltpu.VMEM((KV_GROUP * TK, 2 * TQ), F32), pltpu.VMEM((KV_GROUP * TK, 2 * TQ), F32),
                    stat, stat, stat, stat, pltpu.VMEM((LANES, 2 * TQ), F32)]
    return pl.pallas_call(
        _diff_attn_kernel,
        grid=(batch, N_HEADS_A // nh, nq),
        in_specs=[
            pl.BlockSpec(memory_space=pltpu.SMEM),
            vec(HEAD_DIM_A), vec(HEAD_DIM_A), vec(HEAD_DIM_A), vec(HEAD_DIM_A),
            pl.BlockSpec((2 * HEAD_DIM_A, 1), lambda b, h, i: (0, 0)),
            *[q_spec(t) for t in range(nh)],
            *[kv_spec(t, N_HEADS_A) for t in range(nh)],
            *[kv_spec(t, 2 * N_HEADS_A) for t in range(nh)],
        ],
        out_specs=pl.BlockSpec((TQ, nh * LANES), lambda b, h, i: (b * nq + i, h)),
        out_shape=jax.ShapeDtypeStruct((n, WIDTH_A), BF16),
        scratch_shapes=head_scratch * nh,
        compiler_params=_params(("parallel", "parallel", "arbitrary")),
        name="diff_attn",
    )(slopes, lq1, lk1, lq2, lk2, subln, *([qkv] * (3 * nh)))


def _band_attn_kernel(*refs):
    i = pl.program_id(2)
    npair = PAIRS_PER_STEP
    bias_refs, q_refs = refs[:npair], refs[npair:2 * npair]
    k_refs, v_refs = refs[2 * npair:3 * npair], refs[3 * npair:4 * npair]
    o_ref = refs[4 * npair]
    n_sub = BAND // TK
    rows, offsets = [], []
    for jj in range(n_sub):
        start = i * TQ - N_PREV_CHUNKS * CHUNK + jj * TK
        offsets.append(jnp.where(start < 0, NEG, 0.0).astype(F32))
        rows.append(pl.ds(pl.multiple_of(jnp.maximum(start, 0), TK), TK))
    scores = [[_dot_nt(k_refs[t][rows[jj], :], _stack_halves(q_refs[t][...]))
               + bias_refs[t][jj * TK:(jj + 1) * TK, :] for jj in range(n_sub)] for t in range(npair)]
    feat = lax.broadcasted_iota(jnp.int32, (LANES, TQ), 0)
    for t in range(npair):
        m = functools.reduce(jnp.maximum, [jnp.max(s, axis=0, keepdims=True) + off
                                           for s, off in zip(scores[t], offsets)])
        l = jnp.zeros((1, 2 * TQ), F32)
        acc = jnp.zeros((LANES, 2 * TQ), F32)
        for jj in range(n_sub):
            p = jnp.exp2(scores[t][jj] - (m - offsets[jj]))
            l = l + jnp.sum(p, axis=0, keepdims=True)
            acc = acc + lax.dot_general(v_refs[t][rows[jj], :], p.astype(BF16), (((0,), (0,)), ((), ())),
                                        preferred_element_type=F32)
        o = acc / l
        o_ref[:, t * LANES:(t + 1) * LANES] = jnp.where(feat < 64, o[:, :TQ], o[:, TQ:]).T.astype(BF16)


BIAS_SPAN = 1024


def _band_bias_kernel(line_ref, o_ref):
    kj = lax.broadcasted_iota(jnp.int32, (BAND, TQ), 0)
    qi = lax.broadcasted_iota(jnp.int32, (BAND, TQ), 1)
    kc = kj // CHUNK - N_PREV_CHUNKS
    qc = qi // CHUNK
    allowed = jnp.logical_and(kc <= qc, kc >= qc - N_PREV_CHUNKS)
    for hh in range(2):
        line = jnp.broadcast_to(line_ref[hh:hh + 1, :], (BAND, BIAS_SPAN))
        rolled = pltpu.roll(line, BIAS_SPAN - (BAND - 1), 1, stride=1, stride_axis=0)
        o_ref[:, hh * TQ:(hh + 1) * TQ] = jnp.where(allowed, LOG2E * rolled[:, :TQ], NEG)


def _band_bias(rel_bias):
    rb = rel_bias.astype(F32)
    n_low = (BAND - 1) - REL_CLIP
    n_high = (BAND + TQ - 1) - (BAND - 1) - REL_CLIP - 1
    line = jnp.concatenate([jnp.broadcast_to(rb[:, :1], (N_HEADS_B, n_low)), rb,
                            jnp.broadcast_to(rb[:, -1:], (N_HEADS_B, n_high))], axis=1)
    line = jnp.pad(line[:, ::-1], ((0, 0), (0, BIAS_SPAN - line.shape[1])))
    pairs = N_HEADS_B // 2
    return pl.pallas_call(
        _band_bias_kernel,
        grid=(pairs,),
        in_specs=[pl.BlockSpec((None, 2, BIAS_SPAN), lambda p: (p, 0, 0))],
        out_specs=pl.BlockSpec((None, BAND, 2 * TQ), lambda p: (p, 0, 0)),
        out_shape=jax.ShapeDtypeStruct((pairs, BAND, 2 * TQ), F32),
        compiler_params=_params(("parallel",)),
        name="band_bias",
    )(line.reshape(pairs, 2, BIAS_SPAN))


def _band_attention(qkv, bias, batch, seq):
    nq = seq // TQ
    n = batch * seq
    base = 3 * WIDTH_A // LANES
    pairs = N_HEADS_B // 2
    npair = PAIRS_PER_STEP
    bias_spec = lambda t: pl.BlockSpec((None, BAND, 2 * TQ), lambda b, p, i: (p * npair + t, 0, 0))
    q_spec = lambda t: pl.BlockSpec((None, TQ, LANES), lambda b, p, i: (base + p * npair + t, b * nq + i, 0))
    kv_spec = lambda t, off: pl.BlockSpec((None, seq, LANES), lambda b, p, i: (base + off + p * npair + t, b, 0))
    return pl.pallas_call(
        _band_attn_kernel,
        grid=(batch, pairs // npair, nq),
        in_specs=[
            *[bias_spec(t) for t in range(npair)],
            *[q_spec(t) for t in range(npair)],
            *[kv_spec(t, pairs) for t in range(npair)],
            *[kv_spec(t, 2 * pairs) for t in range(npair)],
        ],
        out_specs=pl.BlockSpec((TQ, npair * LANES), lambda b, p, i: (b * nq + i, p)),
        out_shape=jax.ShapeDtypeStruct((n, WIDTH_B), BF16),
        compiler_params=_params(("parallel", "parallel", "arbitrary")),
        name="band_attn",
    )(*([bias] * npair), *([qkv] * (3 * npair)))


def _pack_bf16_pairs(x):
    w = x.shape[1] // 2
    bits = pltpu.bitcast(x.astype(BF16).astype(F32), jnp.uint32)
    return (bits[:, :w] >> 16) | (bits[:, w:] & jnp.uint32(0xFFFF0000))


def _unpack_bf16_pairs(p):
    lo = pltpu.bitcast(p << 16, F32)
    hi = pltpu.bitcast(p & jnp.uint32(0xFFFF0000), F32)
    return jnp.concatenate([lo, hi], axis=1).astype(BF16)


def _merge_kernel(oa_ref, ob_ref, gates_ref, h_ref, pa_ref, pb_ref, wo_ref, g_ref, b_ref, wra_ref, wrb_ref, br_ref,
                  h1_ref, h1p_ref, gw_ref, idx_ref, rank_ref, counts_ref, seen_ref):
    ma = jnp.dot(oa_ref[...], pa_ref[...], preferred_element_type=F32)
    mb = jnp.dot(ob_ref[...], pb_ref[...], preferred_element_type=F32)
    merged = gates_ref[:, :D_MODEL].astype(F32) * ma + gates_ref[:, D_MODEL:].astype(F32) * mb
    m = jnp.dot(merged.astype(BF16), wo_ref[...], preferred_element_type=F32)
    h1 = _layer_norm(DEEPNORM_ALPHA * h_ref[...] + m, g_ref[...], b_ref[...])
    h1_ref[...] = h1
    h1p_ref[...] = _pack_bf16_pairs(h1)

    h_hi = h1.astype(BF16)
    h_lo = (h1 - h_hi.astype(F32)).astype(BF16)
    part_a = _dot_nt(wra_ref[...], h_hi)
    part_b = _dot_nt(wrb_ref[...], h_lo)
    logits = part_a[:N_EXPERTS] + part_a[N_EXPERTS:2 * N_EXPERTS] + part_b[:N_EXPERTS] + br_ref[...]

    tm = logits.shape[1]
    expert = lax.broadcasted_iota(jnp.int32, (N_EXPERTS, tm), 0)
    vals = logits
    top_val, top_idx = [], []
    for _ in range(TOP_K):
        mx = jnp.max(vals, axis=0, keepdims=True)
        sel = jnp.min(jnp.where(vals == mx, expert, N_EXPERTS), axis=0, keepdims=True)
        top_val.append(mx)
        top_idx.append(sel)
        vals = jnp.where(expert == sel, -jnp.inf, vals)
    ex = [jnp.exp(v - top_val[0]) for v in top_val]
    denom = functools.reduce(jnp.add, ex)
    gw_ref[...] = jnp.concatenate([e / denom for e in ex], axis=0)
    idx_ref[...] = jnp.concatenate(top_idx, axis=0)

    @pl.when(pl.program_id(0) == 0)
    def _():
        seen_ref[...] = jnp.zeros_like(seen_ref)

    chosen = functools.reduce(jnp.logical_or, [expert == s for s in top_idx])
    onehot = jnp.where(chosen, 1.0, 0.0)
    r_i = lax.broadcasted_iota(jnp.int32, (tm, tm), 0)
    c_i = lax.broadcasted_iota(jnp.int32, (tm, tm), 1)
    earlier = jnp.where(r_i < c_i, 1.0, 0.0)
    before = jnp.dot(onehot, earlier, preferred_element_type=F32) + seen_ref[...]
    rank_ref[...] = jnp.concatenate([jnp.sum(jnp.where(expert == s, before, 0.0), axis=0, keepdims=True)
                                     for s in top_idx], axis=0).astype(jnp.int32)
    seen_ref[...] = seen_ref[...] + jnp.sum(onehot, axis=1, keepdims=True)
    counts_ref[...] = seen_ref[...].astype(jnp.int32)


def _merge(oa, ob, gates, h, pa, pb, wo, g, b, wr, br):
    n = h.shape[0]
    tm = min(TM_MERGE, n)
    wr_hi = wr.astype(BF16)
    wr_lo = (wr - wr_hi.astype(F32)).astype(BF16)
    pad = lambda a: jnp.pad(a, ((0, 0), (0, LANES - a.shape[1]))).T
    row = lambda w: pl.BlockSpec((tm, w), lambda i: (i, 0))
    full = lambda r, c: pl.BlockSpec((r, c), lambda i: (0, 0))
    slot_row = pl.BlockSpec((TOP_K, tm), lambda i: (0, i))
    return pl.pallas_call(
        _merge_kernel,
        grid=(n // tm,),
        in_specs=[row(WIDTH_A), row(WIDTH_B), row(GATE_COLS), row(D_MODEL),
                  full(WIDTH_A, D_MODEL), full(WIDTH_B, D_MODEL), full(D_MODEL, D_MODEL),
                  full(1, D_MODEL), full(1, D_MODEL), full(LANES, D_MODEL), full(LANES, D_MODEL),
                  full(N_EXPERTS, 1)],
        out_specs=[row(D_MODEL), row(D_MODEL // 2), slot_row, slot_row, slot_row, full(N_EXPERTS, 1)],
        out_shape=[jax.ShapeDtypeStruct((n, D_MODEL), F32), jax.ShapeDtypeStruct((n, D_MODEL // 2), jnp.uint32),
                   jax.ShapeDtypeStruct((TOP_K, n), F32), jax.ShapeDtypeStruct((TOP_K, n), jnp.int32),
                   jax.ShapeDtypeStruct((TOP_K, n), jnp.int32), jax.ShapeDtypeStruct((N_EXPERTS, 1), jnp.int32)],
        scratch_shapes=[pltpu.VMEM((N_EXPERTS, 1), F32)],
        compiler_params=_params(("arbitrary",)),
        name="merge_ln1_router",
    )(oa, ob, gates, h, pa, pb, wo, g, b, pad(jnp.concatenate([wr_hi, wr_lo], axis=1)), pad(wr_hi),
      br.reshape(N_EXPERTS, 1))


def _dispatch_kernel(pend_ref, dest_ref, h1p_ref, xs_hbm, zero_ref, sem, zero_sem):
    @pl.when(pl.program_id(0) == 0)
    def _():
        zero_ref[...] = jnp.zeros_like(zero_ref)

        def fill(row0):
            block = pl.ds(pl.multiple_of(row0, ROWS_MOE), ROWS_MOE)
            return pltpu.make_async_copy(zero_ref, xs_hbm.at[block, :], zero_sem)
        for e in range(N_EXPERTS):
            fill(jnp.maximum(pend_ref[e] - ROWS_MOE, 0)).start()
        for e in range(N_EXPERTS):
            fill(0).wait()

        first_unused = pend_ref[N_EXPERTS - 1] // ROWS_MOE
        n_blocks = xs_hbm.shape[0] // ROWS_MOE

        @pl.loop(first_unused, n_blocks)
        def _(blk):
            fill(blk * ROWS_MOE).start()

        @pl.loop(first_unused, n_blocks)
        def _(blk):
            fill(0).wait()

    def start(group, _):
        t0 = pl.multiple_of(group * SUBLANES, SUBLANES)
        for j in range(SUBLANES):
            for k in range(TOP_K):
                pltpu.make_async_copy(h1p_ref.at[pl.ds(t0 + j, 1), :],
                                      xs_hbm.at[pl.ds(dest_ref[0, 0, (t0 + j) * TOP_K + k], 1), :],
                                      sem).start(priority=k % 2)
        return 0
    lax.fori_loop(0, TM_DISPATCH // SUBLANES, start, 0)

    for k in range(TOP_K):
        pltpu.make_async_copy(h1p_ref, xs_hbm.at[pl.ds(0, TM_DISPATCH), :], sem).wait()


def _dispatch(dest, pend, h1p, n_rows):
    n = h1p.shape[0]
    tm = min(TM_DISPATCH, n)
    assert tm == TM_DISPATCH
    n_steps = n // tm
    grid_spec = pltpu.PrefetchScalarGridSpec(
        num_scalar_prefetch=1,
        grid=(n_steps,),
        in_specs=[
            pl.BlockSpec((1, 1, tm * TOP_K), lambda i, pe: (i, 0, 0), memory_space=pltpu.SMEM),
            pl.BlockSpec((tm, D_MODEL // 2), lambda i, pe: (i, 0)),
        ],
        out_specs=pl.BlockSpec(memory_space=pl.ANY),
        scratch_shapes=[pltpu.VMEM((ROWS_MOE, D_MODEL // 2), jnp.uint32), pltpu.SemaphoreType.DMA(()),
                        pltpu.SemaphoreType.DMA(())],
    )
    return pl.pallas_call(
        _dispatch_kernel,
        grid_spec=grid_spec,
        out_shape=jax.ShapeDtypeStruct((n_rows, D_MODEL // 2), jnp.uint32),
        compiler_params=_params(("arbitrary",)),
        name="dispatch_rows",
    )(pend, dest.reshape(n_steps, 1, tm * TOP_K), h1p)


def _row_gather(src_hbm, idx_ref, dst_ref, sem, n_rows):
    def start(group, _):
        for j in range(SUBLANES):
            pltpu.make_async_copy(src_hbm.at[pl.ds(idx_ref[0, 0, group * SUBLANES + j], 1), :],
                                  dst_ref.at[group, pl.ds(j, 1), :], sem).start(priority=j % 2)
        return 0
    lax.fori_loop(0, n_rows // SUBLANES, start, 0)


def _moe_kernel(bexp_ref, nused_ref, x_ref, wi_ref, bi_ref, wo_ref, bo_ref, y_ref, wi_bf, wo_bf):
    i = pl.program_id(0)
    n_used = nused_ref[0]
    new_expert = jnp.logical_or(i == 0, bexp_ref[i] != bexp_ref[jnp.maximum(i - 1, 0)])

    @pl.when(jnp.logical_and(new_expert, i < n_used))
    def _():
        wi_bf[...] = wi_ref[...].astype(BF16)
        wo_bf[...] = wo_ref[...].astype(BF16)

    @pl.when(i < n_used)
    def _():
        x = _unpack_bf16_pairs(x_ref[...])
        hb = jnp.dot(x, wi_bf[...], preferred_element_type=F32) + bi_ref[...]
        g = jnp.minimum(hb[:, :D_FF], SWIGLU_LIMIT)
        u = jnp.clip(hb[:, D_FF:], -SWIGLU_LIMIT, SWIGLU_LIMIT)
        a = g * jax.nn.sigmoid(SWIGLU_ALPHA * g) * (u + 1.0)
        y_ref[...] = jnp.dot(a.astype(BF16), wo_bf[...], preferred_element_type=F32) + bo_ref[...]

    @pl.when(i >= n_used)
    def _():
        y_ref[...] = jnp.zeros_like(y_ref)


def _moe(block_exp, n_used, x_sorted, wi, bi, wo, bo):
    n_blocks = block_exp.shape[0]
    x_block = lambda i, be, nu: (jnp.maximum(jnp.minimum(i, nu[0] - 1), 0), 0)
    grid_spec = pltpu.PrefetchScalarGridSpec(
        num_scalar_prefetch=2,
        grid=(n_blocks,),
        in_specs=[
            pl.BlockSpec((ROWS_MOE, D_MODEL // 2), x_block),
            pl.BlockSpec((None, D_MODEL, 2 * D_FF), lambda i, be, nu: (be[i], 0, 0)),
            pl.BlockSpec((None, 1, 2 * D_FF), lambda i, be, nu: (be[i], 0, 0)),
            pl.BlockSpec((None, D_FF, D_MODEL), lambda i, be, nu: (be[i], 0, 0)),
            pl.BlockSpec((None, 1, D_MODEL), lambda i, be, nu: (be[i], 0, 0)),
        ],
        out_specs=pl.BlockSpec((ROWS_MOE, D_MODEL), lambda i, be, nu: (i, 0)),
        scratch_shapes=[pltpu.VMEM((D_MODEL, 2 * D_FF), BF16), pltpu.VMEM((D_FF, D_MODEL), BF16)],
    )
    return pl.pallas_call(
        _moe_kernel,
        grid_spec=grid_spec,
        out_shape=jax.ShapeDtypeStruct((n_blocks * ROWS_MOE, D_MODEL), F32),
        compiler_params=_params(("arbitrary",)),
        name="moe_experts",
    )(block_exp, n_used, x_sorted, wi, bi, wo, bo)


def _combine_kernel(dest_cur_ref, dest_nxt_ref, gw_ref, h1_ref, g_ref, b_ref, y_hbm, o_ref, ybuf, sems):
    i = pl.program_id(0)
    n_steps = pl.num_programs(0)
    slot = i % 2
    rows = TOP_K * TM_COMB

    @pl.when(i == 0)
    def _():
        _row_gather(y_hbm, dest_cur_ref, ybuf.at[0], sems.at[0], rows)

    @pl.when(i + 1 < n_steps)
    def _():
        _row_gather(y_hbm, dest_nxt_ref, ybuf.at[1 - slot], sems.at[1 - slot], rows)

    pltpu.make_async_copy(ybuf.at[slot], ybuf.at[slot], sems.at[slot]).wait()
    gw = gw_ref[...]
    f = jnp.zeros((TM_COMB, D_MODEL), F32)
    tiles = TM_COMB // SUBLANES
    for k in range(TOP_K):
        y_k = ybuf[slot, k * tiles:(k + 1) * tiles].reshape(TM_COMB, D_MODEL)
        f = f + gw[:, k:k + 1] * y_k
    o_ref[...] = _layer_norm(DEEPNORM_ALPHA * h1_ref[...] + f, g_ref[...], b_ref[...])


def _combine(dest_km, gate_w, h1, g, b, y_buf):
    n = h1.shape[0]
    n_steps = n // TM_COMB
    rows = TOP_K * TM_COMB
    return pl.pallas_call(
        _combine_kernel,
        grid=(n_steps,),
        in_specs=[
            pl.BlockSpec((1, 1, rows), lambda i: (i, 0, 0), memory_space=pltpu.SMEM),
            pl.BlockSpec((1, 1, rows), lambda i: (jnp.minimum(i + 1, n_steps - 1), 0, 0), memory_space=pltpu.SMEM),
            pl.BlockSpec((TM_COMB, TOP_K), lambda i: (i, 0)),
            pl.BlockSpec((TM_COMB, D_MODEL), lambda i: (i, 0)),
            pl.BlockSpec((1, D_MODEL), lambda i: (0, 0)),
            pl.BlockSpec((1, D_MODEL), lambda i: (0, 0)),
            pl.BlockSpec(memory_space=pl.ANY),
        ],
        out_specs=pl.BlockSpec((TM_COMB, D_MODEL), lambda i: (i, 0)),
        out_shape=jax.ShapeDtypeStruct((n, D_MODEL), F32),
        scratch_shapes=[pltpu.VMEM((2, rows // SUBLANES, SUBLANES, D_MODEL), F32), pltpu.SemaphoreType.DMA((2,))],
        compiler_params=_params(("arbitrary",)),
        name="combine_ln2",
    )(dest_km, dest_km, gate_w, h1, g, b, y_buf)


def _block_layout(top_idx, rank, counts):
    n_slots = top_idx.shape[1] * TOP_K
    padded = (counts + ROWS_MOE - 1) // ROWS_MOE * ROWS_MOE
    pend = jnp.cumsum(padded)
    pstart = pend - padded
    experts = jnp.arange(N_EXPERTS, dtype=jnp.int32)[:, None, None]
    dest = rank + jnp.sum(jnp.where(top_idx[None] == experts, pstart[:, None, None], 0), axis=0)
    n_blocks = n_slots // ROWS_MOE + N_EXPERTS
    block_start = jnp.arange(n_blocks, dtype=jnp.int32) * ROWS_MOE
    block_exp = jnp.minimum(jnp.sum(block_start[:, None] >= pend[None, :], axis=-1), N_EXPERTS - 1).astype(jnp.int32)
    n_used = (pend[-1] // ROWS_MOE).astype(jnp.int32).reshape(1)
    return dest.astype(jnp.int32), pend.astype(jnp.int32), block_exp, n_used


def kernel(x, ln_in_g, ln_in_b, w_in, b_gate, lambda_q1, lambda_k1, lambda_q2, lambda_k2, subln_w, rel_bias,
           w_branch_a, w_branch_b, w_out, ln1_g, ln1_b, w_router, b_router, w_exp_in, b_exp_in, w_exp_out,
           b_exp_out, ln2_g, ln2_b):
    batch, seq, d = x.shape
    n = batch * seq
    row = lambda a: a.reshape(1, -1).astype(F32)
    l = 0
    h, qkv, gates = _ln_proj(x.reshape(n, d), row(ln_in_g), row(ln_in_b), w_in[l].astype(BF16), row(b_gate[l]))
    slopes = jnp.asarray([2.0 ** (-8.0 * (i + 1) / N_HEADS_A) for i in range(N_HEADS_A)], F32)
    out_a = _diff_attention(qkv, slopes, row(lambda_q1[l]), row(lambda_k1[l]), row(lambda_q2[l]),
                            row(lambda_k2[l]), subln_w[l].reshape(-1, 1).astype(F32), batch, seq)
    out_b = _band_attention(qkv, _band_bias(rel_bias[l]), batch, seq)
    h1, h1p, gate_w, top_idx, rank, counts = _merge(
        out_a, out_b, gates, h, w_branch_a[l].astype(BF16), w_branch_b[l].astype(BF16), w_out[l].astype(BF16),
        row(ln1_g[l]), row(ln1_b[l]), w_router[l].astype(F32), row(b_router[l]))
    dest, pend, block_exp, n_used = _block_layout(top_idx, rank, counts[:, 0])
    x_sorted = _dispatch(dest.T, pend, h1p, block_exp.shape[0] * ROWS_MOE)
    y_buf = _moe(block_exp, n_used, x_sorted, w_exp_in[l].astype(F32), b_exp_in[l].reshape(N_EXPERTS, 1, -1),
                 w_exp_out[l].astype(F32), b_exp_out[l].reshape(N_EXPERTS, 1, -1))
    n_steps = n // TM_COMB
    dest_km = dest.reshape(TOP_K, n_steps, TM_COMB).transpose(1, 0, 2).reshape(n_steps, 1, TOP_K * TM_COMB)
    out = _combine(dest_km, gate_w.T, h1, row(ln2_g[l]), row(ln2_b[l]), y_buf)
    return out.reshape(batch, seq, d)
```

```python
import functools
import math

import jax
import jax.numpy as jnp
import numpy as np
from jax import lax
from jax.experimental import pallas as pl
from jax.experimental.pallas import tpu as pltpu

F32 = jnp.float32
BF16 = jnp.bfloat16

D_MODEL = 1024
CHUNK = 64
N_HEADS_A = 4
HEAD_DIM_A = 64
WIDTH_A = 512
N_HEADS_B = 8
HEAD_DIM_B = 64
WIDTH_B = 512
N_PREV_CHUNKS = 8
REL_CLIP = 128
N_EXPERTS = 32
TOP_K = 4
D_FF = 1024
SWIGLU_ALPHA = 1.702
SWIGLU_LIMIT = 7.0
DEEPNORM_ALPHA = 2.0 ** 0.25
LN_EPS = 1e-5
LAM_INIT = 0.8 - 0.6 * math.exp(-0.3 * 0)

LANES = 128
SUBLANES = 8
N_SLABS = (3 * WIDTH_A + 3 * WIDTH_B) // LANES
GATE_COLS = 2 * D_MODEL
IN_COLS = 3 * WIDTH_A + 3 * WIDTH_B + GATE_COLS
LOG2E = math.log2(math.e)
NEG = -1e30

TM_PROJ = 512
PROJ_CHUNK = 512
TQ = 256
TK = 256
KV_GROUP = 2
HEADS_PER_STEP = 2
PAIRS_PER_STEP = 2
BAND = 3 * TK
TM_MERGE = 512
TM_DISPATCH = 512
ROWS_MOE = 512
TM_COMB = 256
VMEM_LIMIT = 56 * 1024 * 1024


def _layer_norm(x, g, b):
    mu = jnp.mean(x, axis=-1, keepdims=True)
    xc = x - mu
    var = jnp.mean(xc * xc, axis=-1, keepdims=True)
    return xc * lax.rsqrt(var + LN_EPS) * g + b


def _params(sem):
    return pltpu.CompilerParams(dimension_semantics=sem, vmem_limit_bytes=VMEM_LIMIT)


def _ln_proj_kernel(x_ref, g_ref, b_ref, w_ref, bg_ref, h_ref, qkv_ref, gates_ref):
    h = _layer_norm(x_ref[...], g_ref[...], b_ref[...])
    h_ref[...] = h
    hb = h.astype(BF16)
    n_qkv_chunks = (N_SLABS * LANES) // PROJ_CHUNK
    slabs_per_chunk = PROJ_CHUNK // LANES
    q_scale = HEAD_DIM_A ** -0.5 * LOG2E
    for c in range(n_qkv_chunks):
        r = jnp.dot(hb, w_ref[:, c * PROJ_CHUNK:(c + 1) * PROJ_CHUNK], preferred_element_type=F32)
        first = c * slabs_per_chunk
        is_q = (first < WIDTH_A // LANES) or (3 * WIDTH_A // LANES <= first < (3 * WIDTH_A + WIDTH_B) // LANES)
        if is_q:
            r = r * q_scale
        for s in range(slabs_per_chunk):
            qkv_ref[first + s] = r[:, s * LANES:(s + 1) * LANES].astype(BF16)
    g0 = N_SLABS * LANES
    for c in range(GATE_COLS // PROJ_CHUNK):
        r = jnp.dot(hb, w_ref[:, g0 + c * PROJ_CHUNK:g0 + (c + 1) * PROJ_CHUNK], preferred_element_type=F32)
        r = r + bg_ref[:, c * PROJ_CHUNK:(c + 1) * PROJ_CHUNK]
        gates_ref[:, c * PROJ_CHUNK:(c + 1) * PROJ_CHUNK] = jax.nn.sigmoid(r).astype(BF16)


def _ln_proj(x2d, g, b, w_bf16, b_gate):
    n = x2d.shape[0]
    tm = min(TM_PROJ, n)
    const = lambda i: (0, 0)
    return pl.pallas_call(
        _ln_proj_kernel,
        grid=(n // tm,),
        in_specs=[
            pl.BlockSpec((tm, D_MODEL), lambda i: (i, 0)),
            pl.BlockSpec((1, D_MODEL), const),
            pl.BlockSpec((1, D_MODEL), const),
            pl.BlockSpec((D_MODEL, IN_COLS), const, pipeline_mode=pl.Buffered(1)),
            pl.BlockSpec((1, GATE_COLS), const),
        ],
        out_specs=[
            pl.BlockSpec((tm, D_MODEL), lambda i: (i, 0)),
            pl.BlockSpec((N_SLABS, tm, LANES), lambda i: (0, i, 0)),
            pl.BlockSpec((tm, GATE_COLS), lambda i: (i, 0)),
        ],
        out_shape=[
            jax.ShapeDtypeStruct((n, D_MODEL), F32),
            jax.ShapeDtypeStruct((N_SLABS, n, LANES), BF16),
            jax.ShapeDtypeStruct((n, GATE_COLS), BF16),
        ],
        compiler_params=_params(("parallel",)),
        name="ln_proj",
    )(x2d, g, b, w_bf16, b_gate)


def _stack_halves(q):
    lane = lax.broadcasted_iota(jnp.int32, q.shape, 1)
    zero = jnp.zeros_like(q)
    return jnp.concatenate([jnp.where(lane < 64, q, zero), jnp.where(lane >= 64, q, zero)], axis=0)


def _dot_nt(a, b):
    return lax.dot_general(a, b, (((1,), (1,)), ((), ())), preferred_element_type=F32)


def _split3_bf16(x):
    hi = x.astype(BF16)
    r1 = x - hi.astype(F32)
    mid = r1.astype(BF16)
    lo = (r1 - mid.astype(F32)).astype(BF16)
    return hi, mid, lo


class _DiffHead:
    def __init__(self, slope, i, q_ref, k_ref, v_ref, s_refs, mx_refs, m_ref, l_ref, acc_ref):
        self.slope, self.i = slope, i
        self.k_ref, self.v_ref, self.s_refs, self.mx_refs = k_ref, v_ref, s_refs, mx_refs
        self.m_ref, self.l_ref, self.acc_ref = m_ref, l_ref, acc_ref
        lane_q = lax.broadcasted_iota(jnp.int32, (2 * TQ, LANES), 1)
        self.q_aug = jnp.concatenate([_stack_halves(q_ref[...]), jnp.where(lane_q < 3, 1.0, 0.0).astype(BF16)],
                                     axis=1)
        lane_k = lax.broadcasted_iota(jnp.int32, (TK, LANES), 1)
        key_pos = lax.broadcasted_iota(jnp.int32, (TK, LANES), 0)
        self.k_bias = []
        for g in range(KV_GROUP):
            hi, mid, lo = [t.astype(F32) for t in _split3_bf16(slope * (key_pos + g * TK).astype(F32))]
            self.k_bias.append(jnp.where(lane_k == 0, hi, jnp.where(lane_k == 1, mid, jnp.where(lane_k == 2, lo, 0.0))
                                         ).astype(BF16))
        m_ref[...] = jnp.full(m_ref.shape, NEG, F32)
        l_ref[...] = jnp.zeros(l_ref.shape, F32)
        acc_ref[...] = jnp.zeros(acc_ref.shape, F32)

    @staticmethod
    def _group_rows(a):
        return pl.ds(pl.multiple_of(a * (KV_GROUP * TK), KV_GROUP * TK), KV_GROUP * TK)

    def scores_into(self, buf, a):
        mx = None
        for g in range(KV_GROUP):
            rows = pl.ds(pl.multiple_of((a * KV_GROUP + g) * TK, TK), TK)
            s = _dot_nt(jnp.concatenate([self.k_ref[rows, :], self.k_bias[g]], axis=1), self.q_aug)
            self.s_refs[buf][g * TK:(g + 1) * TK, :] = s
            mg = jnp.max(s, axis=0, keepdims=True)
            mx = mg if mx is None else jnp.maximum(mx, mg)
        self.mx_refs[buf][...] = mx

    def _group_offset(self, a):
        return self.slope * ((a * KV_GROUP - self.i) * TK).astype(F32)

    def _update(self, a, s, mx):
        off = self._group_offset(a)
        m_prev = self.m_ref[...]
        m_new = jnp.maximum(m_prev, mx + off)
        alpha = jnp.exp2(m_prev - m_new)
        p = jnp.exp2(s - (m_new - off))
        self.m_ref[...] = m_new
        self.l_ref[...] = alpha * self.l_ref[...] + jnp.sum(p, axis=0, keepdims=True)
        self.acc_ref[...] = alpha * self.acc_ref[...] + lax.dot_general(
            self.v_ref[self._group_rows(a), :], p.astype(BF16), (((0,), (0,)), ((), ())),
            preferred_element_type=F32)

    def full_update(self, buf, a):
        self._update(a, self.s_refs[buf][...], self.mx_refs[buf][...])

    def last_update(self, buf, a):
        kk = lax.broadcasted_iota(jnp.int32, (TK, TQ), 0)
        qq = lax.broadcasted_iota(jnp.int32, (TK, TQ), 1)
        diag = jnp.where(kk // CHUNK <= qq // CHUNK, self.slope * (qq - jnp.abs(qq - kk) - kk).astype(F32), NEG)
        diag = jnp.concatenate([diag, diag], axis=1)
        off = self._group_offset(a)
        scores, offsets = [], []
        for g in range(KV_GROUP):
            jb = a * KV_GROUP + g
            scores.append(self.s_refs[buf][g * TK:(g + 1) * TK, :] + jnp.where(jb == self.i, diag, 0.0))
            offsets.append(jnp.where(jb > self.i, NEG, off))
        m_prev = self.m_ref[...]
        m_new = m_prev
        for s, o in zip(scores, offsets):
            m_new = jnp.maximum(m_new, jnp.max(s, axis=0, keepdims=True) + o)
        alpha = jnp.exp2(m_prev - m_new)
        p = jnp.concatenate([jnp.exp2(s - (m_new - o)) for s, o in zip(scores, offsets)], axis=0)
        self.m_ref[...] = m_new
        self.l_ref[...] = alpha * self.l_ref[...] + jnp.sum(p, axis=0, keepdims=True)
        self.acc_ref[...] = alpha * self.acc_ref[...] + lax.dot_general(
            self.v_ref[self._group_rows(a), :], p.astype(BF16), (((0,), (0,)), ((), ())),
            preferred_element_type=F32)

    def output(self, lam, subln):
        o = self.acc_ref[...] / self.l_ref[...]
        o = o[:, :TQ] - lam * o[:, TQ:]
        o = o * lax.rsqrt(jnp.mean(o * o, axis=0, keepdims=True) + LN_EPS) * subln
        return (o * (1.0 - LAM_INIT)).T.astype(BF16)


def _diff_attn_kernel(slopes_ref, lq1_ref, lk1_ref, lq2_ref, lk2_ref, subln_ref, *refs):
    hp = pl.program_id(1)
    i = pl.program_id(2)
    nh = HEADS_PER_STEP
    q_refs, k_refs, v_refs = refs[:nh], refs[nh:2 * nh], refs[2 * nh:3 * nh]
    o_ref = refs[3 * nh]
    scratch = refs[3 * nh + 1:]
    heads = []
    for t in range(nh):
        s0, s1, mx0, mx1, m, l, acc = scratch[7 * t:7 * t + 7]
        heads.append(_DiffHead(slopes_ref[hp * nh + t] * LOG2E, i, q_refs[t], k_refs[t], v_refs[t],
                               (s0, s1), (mx0, mx1), m, l, acc))

    n_full = i // KV_GROUP
    odd = n_full % 2

    @pl.when(odd == 1)
    def _():
        for h in heads:
            h.scores_into(1, 0)
        for h in heads:
            h.scores_into(0, 1)
        for h in heads:
            h.full_update(1, 0)

    @pl.when(odd == 0)
    def _():
        for h in heads:
            h.scores_into(0, 0)

    def pair(b, _):
        a = odd + 2 * b
        for h in heads:
            h.scores_into(1, a + 1)
        for h in heads:
            h.full_update(0, a)
        for h in heads:
            h.scores_into(0, a + 2)
        for h in heads:
            h.full_update(1, a + 1)
        return 0
    lax.fori_loop(0, (n_full - odd) // 2, pair, 0)

    for h in heads:
        h.last_update(0, n_full)

    lam = (jnp.exp(jnp.sum(lq1_ref[...] * lk1_ref[...], axis=-1, keepdims=True))
           - jnp.exp(jnp.sum(lq2_ref[...] * lk2_ref[...], axis=-1, keepdims=True)) + LAM_INIT)
    for t, h in enumerate(heads):
        o_ref[:, t * LANES:(t + 1) * LANES] = h.output(lam, subln_ref[...])


def _diff_attention(qkv, slopes, lq1, lk1, lq2, lk2, subln, batch, seq):
    nq = seq // TQ
    assert seq % (TQ * KV_GROUP) == 0, "the last key group of a query block must stay inside the sequence"
    n = batch * seq
    nh = HEADS_PER_STEP
    vec = lambda w: pl.BlockSpec((1, w), lambda b, h, i: (0, 0))
    q_spec = lambda t: pl.BlockSpec((None, TQ, LANES), lambda b, h, i: (h * nh + t, b * nq + i, 0))
    kv_spec = lambda t, base: pl.BlockSpec((None, seq, LANES), lambda b, h, i: (base + h * nh + t, b, 0))
    stat = pltpu.VMEM((1, 2 * TQ), F32)
    head_scratch = [pltpu.VMEM((KV_GROUP * TK, 2 * TQ), F32), pltpu.VMEM((KV_GROUP * TK, 2 * TQ), F32),
                    stat, stat, stat, stat, pltpu.VMEM((LANES, 2 * TQ), F32)]
    return pl.pallas_call(
        _diff_attn_kernel,
        grid=(batch, N_HEADS_A // nh, nq),
        in_specs=[
            pl.BlockSpec(memory_space=pltpu.SMEM),
            vec(HEAD_DIM_A), vec(HEAD_DIM_A), vec(HEAD_DIM_A), vec(HEAD_DIM_A),
            pl.BlockSpec((2 * HEAD_DIM_A, 1), lambda b, h, i: (0, 0)),
            *[q_spec(t) for t in range(nh)],
            *[kv_spec(t, N_HEADS_A) for t in range(nh)],
            *[kv_spec(t, 2 * N_HEADS_A) for t in range(nh)],
        ],
        out_specs=pl.BlockSpec((TQ, nh * LANES), lambda b, h, i: (b * nq + i, h)),
        out_shape=jax.ShapeDtypeStruct((n, WIDTH_A), BF16),
        scratch_shapes=head_scratch * nh,
        compiler_params=_params(("parallel", "parallel", "arbitrary")),
        name="diff_attn",
    )(slopes, lq1, lk1, lq2, lk2, subln, *([qkv] * (3 * nh)))


def _band_attn_kernel(*refs):
    i = pl.program_id(2)
    npair = PAIRS_PER_STEP
    bias_refs, q_refs = refs[:npair], refs[npair:2 * npair]
    k_refs, v_refs = refs[2 * npair:3 * npair], refs[3 * npair:4 * npair]
    o_ref = refs[4 * npair]
    n_sub = BAND // TK
    rows, offsets = [], []
    for jj in range(n_sub):
        start = i * TQ - N_PREV_CHUNKS * CHUNK + jj * TK
        offsets.append(jnp.where(start < 0, NEG, 0.0).astype(F32))
        rows.append(pl.ds(pl.multiple_of(jnp.maximum(start, 0), TK), TK))
    scores = [[_dot_nt(k_refs[t][rows[jj], :], _stack_halves(q_refs[t][...]))
               + bias_refs[t][jj * TK:(jj + 1) * TK, :] for jj in range(n_sub)] for t in range(npair)]
    feat = lax.broadcasted_iota(jnp.int32, (LANES, TQ), 0)
    for t in range(npair):
        m = functools.reduce(jnp.maximum, [jnp.max(s, axis=0, keepdims=True) + off
                                           for s, off in zip(scores[t], offsets)])
        l = jnp.zeros((1, 2 * TQ), F32)
        acc = jnp.zeros((LANES, 2 * TQ), F32)
        for jj in range(n_sub):
            p = jnp.exp2(scores[t][jj] - (m - offsets[jj]))
            l = l + jnp.sum(p, axis=0, keepdims=True)
            acc = acc + lax.dot_general(v_refs[t][rows[jj], :], p.astype(BF16), (((0,), (0,)), ((), ())),
                                        preferred_element_type=F32)
        o = acc / l
        o_ref[:, t * LANES:(t + 1) * LANES] = jnp.where(feat < 64, o[:, :TQ], o[:, TQ:]).T.astype(BF16)


BIAS_SPAN = 1024


def _band_bias_kernel(line_ref, o_ref):
    kj = lax.broadcasted_iota(jnp.int32, (BAND, TQ), 0)
    qi = lax.broadcasted_iota(jnp.int32, (BAND, TQ), 1)
    kc = kj // CHUNK - N_PREV_CHUNKS
    qc = qi // CHUNK
    allowed = jnp.logical_and(kc <= qc, kc >= qc - N_PREV_CHUNKS)
    for hh in range(2):
        line = jnp.broadcast_to(line_ref[hh:hh + 1, :], (BAND, BIAS_SPAN))
        rolled = pltpu.roll(line, BIAS_SPAN - (BAND - 1), 1, stride=1, stride_axis=0)
        o_ref[:, hh * TQ:(hh + 1) * TQ] = jnp.where(allowed, LOG2E * rolled[:, :TQ], NEG)


def _band_bias(rel_bias):
    rb = rel_bias.astype(F32)
    n_low = (BAND - 1) - REL_CLIP
    n_high = (BAND + TQ - 1) - (BAND - 1) - REL_CLIP - 1
    line = jnp.concatenate([jnp.broadcast_to(rb[:, :1], (N_HEADS_B, n_low)), rb,
                            jnp.broadcast_to(rb[:, -1:], (N_HEADS_B, n_high))], axis=1)
    line = jnp.pad(line[:, ::-1], ((0, 0), (0, BIAS_SPAN - line.shape[1])))
    pairs = N_HEADS_B // 2
    return pl.pallas_call(
        _band_bias_kernel,
        grid=(pairs,),
        in_specs=[pl.BlockSpec((None, 2, BIAS_SPAN), lambda p: (p, 0, 0))],
        out_specs=pl.BlockSpec((None, BAND, 2 * TQ), lambda p: (p, 0, 0)),
        out_shape=jax.ShapeDtypeStruct((pairs, BAND, 2 * TQ), F32),
        compiler_params=_params(("parallel",)),
        name="band_bias",
    )(line.reshape(pairs, 2, BIAS_SPAN))


def _band_attention(qkv, bias, batch, seq):
    nq = seq // TQ
    n = batch * seq
    base = 3 * WIDTH_A // LANES
    pairs = N_HEADS_B // 2
    npair = PAIRS_PER_STEP
    bias_spec = lambda t: pl.BlockSpec((None, BAND, 2 * TQ), lambda b, p, i: (p * npair + t, 0, 0))
    q_spec = lambda t: pl.BlockSpec((None, TQ, LANES), lambda b, p, i: (base + p * npair + t, b * nq + i, 0))
    kv_spec = lambda t, off: pl.BlockSpec((None, seq, LANES), lambda b, p, i: (base + off + p * npair + t, b, 0))
    return pl.pallas_call(
        _band_attn_kernel,
        grid=(batch, pairs // npair, nq),
        in_specs=[
            *[bias_spec(t) for t in range(npair)],
            *[q_spec(t) for t in range(npair)],
            *[kv_spec(t, pairs) for t in range(npair)],
            *[kv_spec(t, 2 * pairs) for t in range(npair)],
        ],
        out_specs=pl.BlockSpec((TQ, npair * LANES), lambda b, p, i: (b * nq + i, p)),
        out_shape=jax.ShapeDtypeStruct((n, WIDTH_B), BF16),
        compiler_params=_params(("parallel", "parallel", "arbitrary")),
        name="band_attn",
    )(*([bias] * npair), *([qkv] * (3 * npair)))


def _pack_bf16_pairs(x):
    w = x.shape[1] // 2
    bits = pltpu.bitcast(x.astype(BF16).astype(F32), jnp.uint32)
    return (bits[:, :w] >> 16) | (bits[:, w:] & jnp.uint32(0xFFFF0000))


def _unpack_bf16_pairs(p):
    lo = pltpu.bitcast(p << 16, F32)
    hi = pltpu.bitcast(p & jnp.uint32(0xFFFF0000), F32)
    return jnp.concatenate([lo, hi], axis=1).astype(BF16)


def _merge_kernel(oa_ref, ob_ref, gates_ref, h_ref, pa_ref, pb_ref, wo_ref, g_ref, b_ref, wra_ref, wrb_ref, br_ref,
                  h1_ref, h1p_ref, gw_ref, idx_ref, rank_ref, counts_ref, seen_ref):
    ma = jnp.dot(oa_ref[...], pa_ref[...], preferred_element_type=F32)
    mb = jnp.dot(ob_ref[...], pb_ref[...], preferred_element_type=F32)
    merged = gates_ref[:, :D_MODEL].astype(F32) * ma + gates_ref[:, D_MODEL:].astype(F32) * mb
    m = jnp.dot(merged.astype(BF16), wo_ref[...], preferred_element_type=F32)
    h1 = _layer_norm(DEEPNORM_ALPHA * h_ref[...] + m, g_ref[...], b_ref[...])
    h1_ref[...] = h1
    h1p_ref[...] = _pack_bf16_pairs(h1)

    h_hi = h1.astype(BF16)
    h_lo = (h1 - h_hi.astype(F32)).astype(BF16)
    part_a = _dot_nt(wra_ref[...], h_hi)
    part_b = _dot_nt(wrb_ref[...], h_lo)
    logits = part_a[:N_EXPERTS] + part_a[N_EXPERTS:2 * N_EXPERTS] + part_b[:N_EXPERTS] + br_ref[...]

    tm = logits.shape[1]
    expert = lax.broadcasted_iota(jnp.int32, (N_EXPERTS, tm), 0)
    vals = logits
    top_val, top_idx = [], []
    for _ in range(TOP_K):
        mx = jnp.max(vals, axis=0, keepdims=True)
        sel = jnp.min(jnp.where(vals == mx, expert, N_EXPERTS), axis=0, keepdims=True)
        top_val.append(mx)
        top_idx.append(sel)
        vals = jnp.where(expert == sel, -jnp.inf, vals)
    ex = [jnp.exp(v - top_val[0]) for v in top_val]
    denom = functools.reduce(jnp.add, ex)
    gw_ref[...] = jnp.concatenate([e / denom for e in ex], axis=0)
    idx_ref[...] = jnp.concatenate(top_idx, axis=0)

    @pl.when(pl.program_id(0) == 0)
    def _():
        seen_ref[...] = jnp.zeros_like(seen_ref)

    chosen = functools.reduce(jnp.logical_or, [expert == s for s in top_idx])
    onehot = jnp.where(chosen, 1.0, 0.0)
    r_i = lax.broadcasted_iota(jnp.int32, (tm, tm), 0)
    c_i = lax.broadcasted_iota(jnp.int32, (tm, tm), 1)
    earlier = jnp.where(r_i < c_i, 1.0, 0.0)
    before = jnp.dot(onehot, earlier, preferred_element_type=F32) + seen_ref[...]
    rank_ref[...] = jnp.concatenate([jnp.sum(jnp.where(expert == s, before, 0.0), axis=0, keepdims=True)
                                     for s in top_idx], axis=0).astype(jnp.int32)
    seen_ref[...] = seen_ref[...] + jnp.sum(onehot, axis=1, keepdims=True)
    counts_ref[...] = seen_ref[...].astype(jnp.int32)


def _merge(oa, ob, gates, h, pa, pb, wo, g, b, wr, br):
    n = h.shape[0]
    tm = min(TM_MERGE, n)
    wr_hi = wr.astype(BF16)
    wr_lo = (wr - wr_hi.astype(F32)).astype(BF16)
    pad = lambda a: jnp.pad(a, ((0, 0), (0, LANES - a.shape[1]))).T
    row = lambda w: pl.BlockSpec((tm, w), lambda i: (i, 0))
    full = lambda r, c: pl.BlockSpec((r, c), lambda i: (0, 0))
    slot_row = pl.BlockSpec((TOP_K, tm), lambda i: (0, i))
    return pl.pallas_call(
        _merge_kernel,
        grid=(n // tm,),
        in_specs=[row(WIDTH_A), row(WIDTH_B), row(GATE_COLS), row(D_MODEL),
                  full(WIDTH_A, D_MODEL), full(WIDTH_B, D_MODEL), full(D_MODEL, D_MODEL),
                  full(1, D_MODEL), full(1, D_MODEL), full(LANES, D_MODEL), full(LANES, D_MODEL),
                  full(N_EXPERTS, 1)],
        out_specs=[row(D_MODEL), row(D_MODEL // 2), slot_row, slot_row, slot_row, full(N_EXPERTS, 1)],
        out_shape=[jax.ShapeDtypeStruct((n, D_MODEL), F32), jax.ShapeDtypeStruct((n, D_MODEL // 2), jnp.uint32),
                   jax.ShapeDtypeStruct((TOP_K, n), F32), jax.ShapeDtypeStruct((TOP_K, n), jnp.int32),
                   jax.ShapeDtypeStruct((TOP_K, n), jnp.int32), jax.ShapeDtypeStruct((N_EXPERTS, 1), jnp.int32)],
        scratch_shapes=[pltpu.VMEM((N_EXPERTS, 1), F32)],
        compiler_params=_params(("arbitrary",)),
        name="merge_ln1_router",
    )(oa, ob, gates, h, pa, pb, wo, g, b, pad(jnp.concatenate([wr_hi, wr_lo], axis=1)), pad(wr_hi),
      br.reshape(N_EXPERTS, 1))


def _dispatch_kernel(pend_ref, dest_ref, h1p_ref, xs_hbm, zero_ref, sem, zero_sem):
    @pl.when(pl.program_id(0) == 0)
    def _():
        zero_ref[...] = jnp.zeros_like(zero_ref)

        def fill(row0):
            block = pl.ds(pl.multiple_of(row0, ROWS_MOE), ROWS_MOE)
            return pltpu.make_async_copy(zero_ref, xs_hbm.at[block, :], zero_sem)
        for e in range(N_EXPERTS):
            fill(jnp.maximum(pend_ref[e] - ROWS_MOE, 0)).start()
        for e in range(N_EXPERTS):
            fill(0).wait()

        first_unused = pend_ref[N_EXPERTS - 1] // ROWS_MOE
        n_blocks = xs_hbm.shape[0] // ROWS_MOE

        @pl.loop(first_unused, n_blocks)
        def _(blk):
            fill(blk * ROWS_MOE).start()

        @pl.loop(first_unused, n_blocks)
        def _(blk):
            fill(0).wait()

    def start(group, _):
        t0 = pl.multiple_of(group * SUBLANES, SUBLANES)
        for j in range(SUBLANES):
            for k in range(TOP_K):
                pltpu.make_async_copy(h1p_ref.at[pl.ds(t0 + j, 1), :],
                                      xs_hbm.at[pl.ds(dest_ref[0, 0, (t0 + j) * TOP_K + k], 1), :],
                                      sem).start(priority=k % 2)
        return 0
    lax.fori_loop(0, TM_DISPATCH // SUBLANES, start, 0)

    for k in range(TOP_K):
        pltpu.make_async_copy(h1p_ref, xs_hbm.at[pl.ds(0, TM_DISPATCH), :], sem).wait()


def _dispatch(dest, pend, h1p, n_rows):
    n = h1p.shape[0]
    tm = min(TM_DISPATCH, n)
    assert tm == TM_DISPATCH
    n_steps = n // tm
    grid_spec = pltpu.PrefetchScalarGridSpec(
        num_scalar_prefetch=1,
        grid=(n_steps,),
        in_specs=[
            pl.BlockSpec((1, 1, tm * TOP_K), lambda i, pe: (i, 0, 0), memory_space=pltpu.SMEM),
            pl.BlockSpec((tm, D_MODEL // 2), lambda i, pe: (i, 0)),
        ],
        out_specs=pl.BlockSpec(memory_space=pl.ANY),
        scratch_shapes=[pltpu.VMEM((ROWS_MOE, D_MODEL // 2), jnp.uint32), pltpu.SemaphoreType.DMA(()),
                        pltpu.SemaphoreType.DMA(())],
    )
    return pl.pallas_call(
        _dispatch_kernel,
        grid_spec=grid_spec,
        out_shape=jax.ShapeDtypeStruct((n_rows, D_MODEL // 2), jnp.uint32),
        compiler_params=_params(("arbitrary",)),
        name="dispatch_rows",
    )(pend, dest.reshape(n_steps, 1, tm * TOP_K), h1p)


def _row_gather(src_hbm, idx_ref, dst_ref, sem, n_rows):
    def start(group, _):
        for j in range(SUBLANES):
            pltpu.make_async_copy(src_hbm.at[pl.ds(idx_ref[0, 0, group * SUBLANES + j], 1), :],
                                  dst_ref.at[group, pl.ds(j, 1), :], sem).start(priority=j % 2)
        return 0
    lax.fori_loop(0, n_rows // SUBLANES, start, 0)


def _moe_kernel(bexp_ref, nused_ref, x_ref, wi_ref, bi_ref, wo_ref, bo_ref, y_ref, wi_bf, wo_bf):
    i = pl.program_id(0)
    n_used = nused_ref[0]
    new_expert = jnp.logical_or(i == 0, bexp_ref[i] != bexp_ref[jnp.maximum(i - 1, 0)])

    @pl.when(jnp.logical_and(new_expert, i < n_used))
    def _():
        wi_bf[...] = wi_ref[...].astype(BF16)
        wo_bf[...] = wo_ref[...].astype(BF16)

    @pl.when(i < n_used)
    def _():
        x = _unpack_bf16_pairs(x_ref[...])
        hb = jnp.dot(x, wi_bf[...], preferred_element_type=F32) + bi_ref[...]
        g = jnp.minimum(hb[:, :D_FF], SWIGLU_LIMIT)
        u = jnp.clip(hb[:, D_FF:], -SWIGLU_LIMIT, SWIGLU_LIMIT)
        a = g * jax.nn.sigmoid(SWIGLU_ALPHA * g) * (u + 1.0)
        y_ref[...] = jnp.dot(a.astype(BF16), wo_bf[...], preferred_element_type=F32) + bo_ref[...]

    @pl.when(i >= n_used)
    def _():
        y_ref[...] = jnp.zeros_like(y_ref)


def _moe(block_exp, n_used, x_sorted, wi, bi, wo, bo):
    n_blocks = block_exp.shape[0]
    x_block = lambda i, be, nu: (jnp.maximum(jnp.minimum(i, nu[0] - 1), 0), 0)
    grid_spec = pltpu.PrefetchScalarGridSpec(
        num_scalar_prefetch=2,
        grid=(n_blocks,),
        in_specs=[
            pl.BlockSpec((ROWS_MOE, D_MODEL // 2), x_block),
            pl.BlockSpec((None, D_MODEL, 2 * D_FF), lambda i, be, nu: (be[i], 0, 0)),
            pl.BlockSpec((None, 1, 2 * D_FF), lambda i, be, nu: (be[i], 0, 0)),
            pl.BlockSpec((None, D_FF, D_MODEL), lambda i, be, nu: (be[i], 0, 0)),
            pl.BlockSpec((None, 1, D_MODEL), lambda i, be, nu: (be[i], 0, 0)),
        ],
        out_specs=pl.BlockSpec((ROWS_MOE, D_MODEL), lambda i, be, nu: (i, 0)),
        scratch_shapes=[pltpu.VMEM((D_MODEL, 2 * D_FF), BF16), pltpu.VMEM((D_FF, D_MODEL), BF16)],
    )
    return pl.pallas_call(
        _moe_kernel,
        grid_spec=grid_spec,
        out_shape=jax.ShapeDtypeStruct((n_blocks * ROWS_MOE, D_MODEL), F32),
        compiler_params=_params(("arbitrary",)),
        name="moe_experts",
    )(block_exp, n_used, x_sorted, wi, bi, wo, bo)


def _combine_kernel(dest_cur_ref, dest_nxt_ref, gw_ref, h1_ref, g_ref, b_ref, y_hbm, o_ref, ybuf, sems):
    i = pl.program_id(0)
    n_steps = pl.num_programs(0)
    slot = i % 2
    rows = TOP_K * TM_COMB

    @pl.when(i == 0)
    def _():
        _row_gather(y_hbm, dest_cur_ref, ybuf.at[0], sems.at[0], rows)

    @pl.when(i + 1 < n_steps)
    def _():
        _row_gather(y_hbm, dest_nxt_ref, ybuf.at[1 - slot], sems.at[1 - slot], rows)

    pltpu.make_async_copy(ybuf.at[slot], ybuf.at[slot], sems.at[slot]).wait()
    gw = gw_ref[...]
    f = jnp.zeros((TM_COMB, D_MODEL), F32)
    tiles = TM_COMB // SUBLANES
    for k in range(TOP_K):
        y_k = ybuf[slot, k * tiles:(k + 1) * tiles].reshape(TM_COMB, D_MODEL)
        f = f + gw[:, k:k + 1] * y_k
    o_ref[...] = _layer_norm(DEEPNORM_ALPHA * h1_ref[...] + f, g_ref[...], b_ref[...])


def _combine(dest_km, gate_w, h1, g, b, y_buf):
    n = h1.shape[0]
    n_steps = n // TM_COMB
    rows = TOP_K * TM_COMB
    return pl.pallas_call(
        _combine_kernel,
        grid=(n_steps,),
        in_specs=[
            pl.BlockSpec((1, 1, rows), lambda i: (i, 0, 0), memory_space=pltpu.SMEM),
            pl.BlockSpec((1, 1, rows), lambda i: (jnp.minimum(i + 1, n_steps - 1), 0, 0), memory_space=pltpu.SMEM),
            pl.BlockSpec((TM_COMB, TOP_K), lambda i: (i, 0)),
            pl.BlockSpec((TM_COMB, D_MODEL), lambda i: (i, 0)),
            pl.BlockSpec((1, D_MODEL), lambda i: (0, 0)),
            pl.BlockSpec((1, D_MODEL), lambda i: (0, 0)),
            pl.BlockSpec(memory_space=pl.ANY),
        ],
        out_specs=pl.BlockSpec((TM_COMB, D_MODEL), lambda i: (i, 0)),
        out_shape=jax.ShapeDtypeStruct((n, D_MODEL), F32),
        scratch_shapes=[pltpu.VMEM((2, rows // SUBLANES, SUBLANES, D_MODEL), F32), pltpu.SemaphoreType.DMA((2,))],
        compiler_params=_params(("arbitrary",)),
        name="combine_ln2",
    )(dest_km, dest_km, gate_w, h1, g, b, y_buf)


def _block_layout(top_idx, rank, counts):
    n_slots = top_idx.shape[1] * TOP_K
    padded = (counts + ROWS_MOE - 1) // ROWS_MOE * ROWS_MOE
    pend = jnp.cumsum(padded)
    pstart = pend - padded
    experts = jnp.arange(N_EXPERTS, dtype=jnp.int32)[:, None, None]
    dest = rank + jnp.sum(jnp.where(top_idx[None] == experts, pstart[:, None, None], 0), axis=0)
    n_blocks = n_slots // ROWS_MOE + N_EXPERTS
    block_start = jnp.arange(n_blocks, dtype=jnp.int32) * ROWS_MOE
    block_exp = jnp.minimum(jnp.sum(block_start[:, None] >= pend[None, :], axis=-1), N_EXPERTS - 1).astype(jnp.int32)
    n_used = (pend[-1] // ROWS_MOE).astype(jnp.int32).reshape(1)
    return dest.astype(jnp.int32), pend.astype(jnp.int32), block_exp, n_used


def kernel(x, ln_in_g, ln_in_b, w_in, b_gate, lambda_q1, lambda_k1, lambda_q2, lambda_k2, subln_w, rel_bias,
           w_branch_a, w_branch_b, w_out, ln1_g, ln1_b, w_router, b_router, w_exp_in, b_exp_in, w_exp_out,
           b_exp_out, ln2_g, ln2_b):
    batch, seq, d = x.shape
    n = batch * seq
    row = lambda a: a.reshape(1, -1).astype(F32)
    l = 0
    h, qkv, gates = _ln_proj(x.reshape(n, d), row(ln_in_g), row(ln_in_b), w_in[l].astype(BF16), row(b_gate[l]))
    slopes = jnp.asarray([2.0 ** (-8.0 * (i + 1) / N_HEADS_A) for i in range(N_HEADS_A)], F32)
    out_a = _diff_attention(qkv, slopes, row(lambda_q1[l]), row(lambda_k1[l]), row(lambda_q2[l]),
                            row(lambda_k2[l]), subln_w[l].reshape(-1, 1).astype(F32), batch, seq)
    out_b = _band_attention(qkv, _band_bias(rel_bias[l]), batch, seq)
    h1, h1p, gate_w, top_idx, rank, counts = _merge(
        out_a, out_b, gates, h, w_branch_a[l].astype(BF16), w_branch_b[l].astype(BF16), w_out[l].astype(BF16),
        row(ln1_g[l]), row(ln1_b[l]), w_router[l].astype(F32), row(b_router[l]))
    dest, pend, block_exp, n_used = _block_layout(top_idx, rank, counts[:, 0])
    x_sorted = _dispatch(dest.T, pend, h1p, block_exp.shape[0] * ROWS_MOE)
    y_buf = _moe(block_exp, n_used, x_sorted, w_exp_in[l].astype(F32), b_exp_in[l].reshape(N_EXPERTS, 1, -1),
                 w_exp_out[l].astype(F32), b_exp_out[l].reshape(N_EXPERTS, 1, -1))
    n_steps = n // TM_COMB
    dest_km = dest.reshape(TOP_K, n_steps, TM_COMB).transpose(1, 0, 2).reshape(n_steps, 1, TOP_K * TM_COMB)
    out = _combine(dest_km, gate_w.T, h1, row(ln2_g[l]), row(ln2_b[l]), y_buf)
    return out.reshape(batch, seq, d)
```

```python
import functools
import math

import jax
import jax.numpy as jnp
import numpy as np
from jax import lax
from jax.experimental import pallas as pl
from jax.experimental.pallas import tpu as pltpu

F32 = jnp.float32
BF16 = jnp.bfloat16

D_MODEL = 1024
CHUNK = 64
N_HEADS_A = 4
HEAD_DIM_A = 64
WIDTH_A = 512
N_HEADS_B = 8
HEAD_DIM_B = 64
WIDTH_B = 512
N_PREV_CHUNKS = 8
REL_CLIP = 128
N_EXPERTS = 32
TOP_K = 4
D_FF = 1024
SWIGLU_ALPHA = 1.702
SWIGLU_LIMIT = 7.0
DEEPNORM_ALPHA = 2.0 ** 0.25
LN_EPS = 1e-5
LAM_INIT = 0.8 - 0.6 * math.exp(-0.3 * 0)

LANES = 128
SUBLANES = 8
N_SLABS = (3 * WIDTH_A + 3 * WIDTH_B) // LANES
GATE_COLS = 2 * D_MODEL
IN_COLS = 3 * WIDTH_A + 3 * WIDTH_B + GATE_COLS
LOG2E = math.log2(math.e)
NEG = -1e30

TM_PROJ = 512
PROJ_CHUNK = 512
TQ = 256
TK = 256
KV_GROUP = 2
HEADS_PER_STEP = 2
PAIRS_PER_STEP = 2
BAND = 3 * TK
TM_MERGE = 512
TM_DISPATCH = 512
ROWS_MOE = 512
TM_COMB = 256
VMEM_LIMIT = 56 * 1024 * 1024


def _layer_norm(x, g, b):
    mu = jnp.mean(x, axis=-1, keepdims=True)
    xc = x - mu
    var = jnp.mean(xc * xc, axis=-1, keepdims=True)
    return xc * lax.rsqrt(var + LN_EPS) * g + b


def _params(sem):
    return pltpu.CompilerParams(dimension_semantics=sem, vmem_limit_bytes=VMEM_LIMIT)


def _ln_proj_kernel(x_ref, g_ref, b_ref, w_ref, bg_ref, h_ref, qkv_ref, gates_ref):
    h = _layer_norm(x_ref[...], g_ref[...], b_ref[...])
    h_ref[...] = h
    hb = h.astype(BF16)
    n_qkv_chunks = (N_SLABS * LANES) // PROJ_CHUNK
    slabs_per_chunk = PROJ_CHUNK // LANES
    q_scale = HEAD_DIM_A ** -0.5 * LOG2E
    for c in range(n_qkv_chunks):
        r = jnp.dot(hb, w_ref[:, c * PROJ_CHUNK:(c + 1) * PROJ_CHUNK], preferred_element_type=F32)
        first = c * slabs_per_chunk
        is_q = (first < WIDTH_A // LANES) or (3 * WIDTH_A // LANES <= first < (3 * WIDTH_A + WIDTH_B) // LANES)
        if is_q:
            r = r * q_scale
        for s in range(slabs_per_chunk):
            qkv_ref[first + s] = r[:, s * LANES:(s + 1) * LANES].astype(BF16)
    g0 = N_SLABS * LANES
    for c in range(GATE_COLS // PROJ_CHUNK):
        r = jnp.dot(hb, w_ref[:, g0 + c * PROJ_CHUNK:g0 + (c + 1) * PROJ_CHUNK], preferred_element_type=F32)
        r = r + bg_ref[:, c * PROJ_CHUNK:(c + 1) * PROJ_CHUNK]
        gates_ref[:, c * PROJ_CHUNK:(c + 1) * PROJ_CHUNK] = jax.nn.sigmoid(r).astype(BF16)


def _ln_proj(x2d, g, b, w_bf16, b_gate):
    n = x2d.shape[0]
    tm = min(TM_PROJ, n)
    const = lambda i: (0, 0)
    return pl.pallas_call(
        _ln_proj_kernel,
        grid=(n // tm,),
        in_specs=[
            pl.BlockSpec((tm, D_MODEL), lambda i: (i, 0)),
            pl.BlockSpec((1, D_MODEL), const),
            pl.BlockSpec((1, D_MODEL), const),
            pl.BlockSpec((D_MODEL, IN_COLS), const, pipeline_mode=pl.Buffered(1)),
            pl.BlockSpec((1, GATE_COLS), const),
        ],
        out_specs=[
            pl.BlockSpec((tm, D_MODEL), lambda i: (i, 0)),
            pl.BlockSpec((N_SLABS, tm, LANES), lambda i: (0, i, 0)),
            pl.BlockSpec((tm, GATE_COLS), lambda i: (i, 0)),
        ],
        out_shape=[
            jax.ShapeDtypeStruct((n, D_MODEL), F32),
            jax.ShapeDtypeStruct((N_SLABS, n, LANES), BF16),
            jax.ShapeDtypeStruct((n, GATE_COLS), BF16),
        ],
        compiler_params=_params(("parallel",)),
        name="ln_proj",
    )(x2d, g, b, w_bf16, b_gate)


def _stack_halves(q):
    lane = lax.broadcasted_iota(jnp.int32, q.shape, 1)
    zero = jnp.zeros_like(q)
    return jnp.concatenate([jnp.where(lane < 64, q, zero), jnp.where(lane >= 64, q, zero)], axis=0)


def _dot_nt(a, b):
    return lax.dot_general(a, b, (((1,), (1,)), ((), ())), preferred_element_type=F32)


def _split3_bf16(x):
    hi = x.astype(BF16)
    r1 = x - hi.astype(F32)
    mid = r1.astype(BF16)
    lo = (r1 - mid.astype(F32)).astype(BF16)
    return hi, mid, lo


class _DiffHead:
    def __init__(self, slope, i, q_ref, k_ref, v_ref, s_refs, mx_refs, m_ref, l_ref, acc_ref):
        self.slope, self.i = slope, i
        self.k_ref, self.v_ref, self.s_refs, self.mx_refs = k_ref, v_ref, s_refs, mx_refs
        self.m_ref, self.l_ref, self.acc_ref = m_ref, l_ref, acc_ref
        lane_q = lax.broadcasted_iota(jnp.int32, (2 * TQ, LANES), 1)
        self.q_aug = jnp.concatenate([_stack_halves(q_ref[...]), jnp.where(lane_q < 3, 1.0, 0.0).astype(BF16)],
                                     axis=1)
        lane_k = lax.broadcasted_iota(jnp.int32, (TK, LANES), 1)
        key_pos = lax.broadcasted_iota(jnp.int32, (TK, LANES), 0)
        self.k_bias = []
        for g in range(KV_GROUP):
            hi, mid, lo = [t.astype(F32) for t in _split3_bf16(slope * (key_pos + g * TK).astype(F32))]
            self.k_bias.append(jnp.where(lane_k == 0, hi, jnp.where(lane_k == 1, mid, jnp.where(lane_k == 2, lo, 0.0))
                                         ).astype(BF16))
        m_ref[...] = jnp.full(m_ref.shape, NEG, F32)
        l_ref[...] = jnp.zeros(l_ref.shape, F32)
        acc_ref[...] = jnp.zeros(acc_ref.shape, F32)

    @staticmethod
    def _group_rows(a):
        return pl.ds(pl.multiple_of(a * (KV_GROUP * TK), KV_GROUP * TK), KV_GROUP * TK)

    def scores_into(self, buf, a):
        mx = None
        for g in range(KV_GROUP):
            rows = pl.ds(pl.multiple_of((a * KV_GROUP + g) * TK, TK), TK)
            s = _dot_nt(jnp.concatenate([self.k_ref[rows, :], self.k_bias[g]], axis=1), self.q_aug)
            self.s_refs[buf][g * TK:(g + 1) * TK, :] = s
            mg = jnp.max(s, axis=0, keepdims=True)
            mx = mg if mx is None else jnp.maximum(mx, mg)
        self.mx_refs[buf][...] = mx

    def _group_offset(self, a):
        return self.slope * ((a * KV_GROUP - self.i) * TK).astype(F32)

    def _update(self, a, s, mx):
        off = self._group_offset(a)
        m_prev = self.m_ref[...]
        m_new = jnp.maximum(m_prev, mx + off)
        alpha = jnp.exp2(m_prev - m_new)
        p = jnp.exp2(s - (m_new - off))
        self.m_ref[...] = m_new
        self.l_ref[...] = alpha * self.l_ref[...] + jnp.sum(p, axis=0, keepdims=True)
        self.acc_ref[...] = alpha * self.acc_ref[...] + lax.dot_general(
            self.v_ref[self._group_rows(a), :], p.astype(BF16), (((0,), (0,)), ((), ())),
            preferred_element_type=F32)

    def full_update(self, buf, a):
        self._update(a, self.s_refs[buf][...], self.mx_refs[buf][...])

    def last_update(self, buf, a):
        kk = lax.broadcasted_iota(jnp.int32, (TK, TQ), 0)
        qq = lax.broadcasted_iota(jnp.int32, (TK, TQ), 1)
        diag = jnp.where(kk // CHUNK <= qq // CHUNK, self.slope * (qq - jnp.abs(qq - kk) - kk).astype(F32), NEG)
        diag = jnp.concatenate([diag, diag], axis=1)
        off = self._group_offset(a)
        scores, offsets = [], []
        for g in range(KV_GROUP):
            jb = a * KV_GROUP + g
            scores.append(self.s_refs[buf][g * TK:(g + 1) * TK, :] + jnp.where(jb == self.i, diag, 0.0))
            offsets.append(jnp.where(jb > self.i, NEG, off))
        m_prev = self.m_ref[...]
        m_new = m_prev
        for s, o in zip(scores, offsets):
            m_new = jnp.maximum(m_new, jnp.max(s, axis=0, keepdims=True) + o)
        alpha = jnp.exp2(m_prev - m_new)
        p = jnp.concatenate([jnp.exp2(s - (m_new - o)) for s, o in zip(scores, offsets)], axis=0)
        self.m_ref[...] = m_new
        self.l_ref[...] = alpha * self.l_ref[...] + jnp.sum(p, axis=0, keepdims=True)
        self.acc_ref[...] = alpha * self.acc_ref[...] + lax.dot_general(
            self.v_ref[self._group_rows(a), :], p.astype(BF16), (((0,), (0,)), ((), ())),
            preferred_element_type=F32)

    def output(self, lam, subln):
        o = self.acc_ref[...] / self.l_ref[...]
        o = o[:, :TQ] - lam * o[:, TQ:]
        o = o * lax.rsqrt(jnp.mean(o * o, axis=0, keepdims=True) + LN_EPS) * subln
        return (o * (1.0 - LAM_INIT)).T.astype(BF16)


def _diff_attn_kernel(slopes_ref, lq1_ref, lk1_ref, lq2_ref, lk2_ref, subln_ref, *refs):
    hp = pl.program_id(1)
    i = pl.program_id(2)
    nh = HEADS_PER_STEP
    q_refs, k_refs, v_refs = refs[:nh], refs[nh:2 * nh], refs[2 * nh:3 * nh]
    o_ref = refs[3 * nh]
    scratch = refs[3 * nh + 1:]
    heads = []
    for t in range(nh):
        s0, s1, mx0, mx1, m, l, acc = scratch[7 * t:7 * t + 7]
        heads.append(_DiffHead(slopes_ref[hp * nh + t] * LOG2E, i, q_refs[t], k_refs[t], v_refs[t],
                               (s0, s1), (mx0, mx1), m, l, acc))

    n_full = i // KV_GROUP
    odd = n_full % 2

    @pl.when(odd == 1)
    def _():
        for h in heads:
            h.scores_into(1, 0)
        for h in heads:
            h.scores_into(0, 1)
        for h in heads:
            h.full_update(1, 0)

    @pl.when(odd == 0)
    def _():
        for h in heads:
            h.scores_into(0, 0)

    def pair(b, _):
        a = odd + 2 * b
        for h in heads:
            h.scores_into(1, a + 1)
        for h in heads:
            h.full_update(0, a)
        for h in heads:
            h.scores_into(0, a + 2)
        for h in heads:
            h.full_update(1, a + 1)
        return 0
    lax.fori_loop(0, (n_full - odd) // 2, pair, 0)

    for h in heads:
        h.last_update(0, n_full)

    lam = (jnp.exp(jnp.sum(lq1_ref[...] * lk1_ref[...], axis=-1, keepdims=True))
           - jnp.exp(jnp.sum(lq2_ref[...] * lk2_ref[...], axis=-1, keepdims=True)) + LAM_INIT)
    for t, h in enumerate(heads):
        o_ref[:, t * LANES:(t + 1) * LANES] = h.output(lam, subln_ref[...])


def _diff_attention(qkv, slopes, lq1, lk1, lq2, lk2, subln, batch, seq):
    nq = seq // TQ
    assert seq % (TQ * KV_GROUP) == 0, "the last key group of a query block must stay inside the sequence"
    n = batch * seq
    nh = HEADS_PER_STEP
    vec = lambda w: pl.BlockSpec((1, w), lambda b, h, i: (0, 0))
    q_spec = lambda t: pl.BlockSpec((None, TQ, LANES), lambda b, h, i: (h * nh + t, b * nq + i, 0))
    kv_spec = lambda t, base: pl.BlockSpec((None, seq, LANES), lambda b, h, i: (base + h * nh + t, b, 0))
    stat = pltpu.VMEM((1, 2 * TQ), F32)
    head_scratch = [pltpu.VMEM((KV_GROUP * TK, 2 * TQ), F32), pltpu.VMEM((KV_GROUP * TK, 2 * TQ), F32),
                    stat, stat, stat, stat, pltpu.VMEM((LANES, 2 * TQ), F32)]
    return pl.pallas_call(
        _diff_attn_kernel,
        grid=(batch, N_HEADS_A // nh, nq),
        in_specs=[
            pl.BlockSpec(memory_space=pltpu.SMEM),
            vec(HEAD_DIM_A), vec(HEAD_DIM_A), vec(HEAD_DIM_A), vec(HEAD_DIM_A),
            pl.BlockSpec((2 * HEAD_DIM_A, 1), lambda b, h, i: (0, 0)),
            *[q_spec(t) for t in range(nh)],
            *[kv_spec(t, N_HEADS_A) for t in range(nh)],
            *[kv_spec(t, 2 * N_HEADS_A) for t in range(nh)],
        ],
        out_specs=pl.BlockSpec((TQ, nh * LANES), lambda b, h, i: (b * nq + i, h)),
        out_shape=jax.ShapeDtypeStruct((n, WIDTH_A), BF16),
        scratch_shapes=head_scratch * nh,
        compiler_params=_params(("parallel", "parallel", "arbitrary")),
        name="diff_attn",
    )(slopes, lq1, lk1, lq2, lk2, subln, *([qkv] * (3 * nh)))


def _band_attn_kernel(*refs):
    npair = PAIRS_PER_STEP
    bias_refs, q_refs = refs[:npair], refs[npair:2 * npair]
    k_refs, v_refs = refs[2 * npair:3 * npair], refs[3 * npair:4 * npair]
    o_ref = refs[4 * npair]
    s_refs = refs[4 * npair + 1:]
    n_sub = BAND // TK
    nq = o_ref.shape[0] // TQ
    feat = lax.broadcasted_iota(jnp.int32, (LANES, TQ), 0)

    def key_rows(i, jj):
        start = i * TQ - N_PREV_CHUNKS * CHUNK + jj * TK
        return start, pl.ds(pl.multiple_of(jnp.maximum(start, 0), TK), TK)

    def scores_into(buf, i):
        for t in range(npair):
            q2 = _stack_halves(q_refs[t][pl.ds(pl.multiple_of(i * TQ, TQ), TQ), :])
            for jj in range(n_sub):
                _, rows = key_rows(i, jj)
                s_refs[2 * t + buf][jj * TK:(jj + 1) * TK, :] = (
                    _dot_nt(k_refs[t][rows, :], q2) + bias_refs[t][jj * TK:(jj + 1) * TK, :])

    def softmax_out(buf, i):
        starts_rows = [key_rows(i, jj) for jj in range(n_sub)]
        offsets = [jnp.where(start < 0, NEG, 0.0).astype(F32) for start, _ in starts_rows]
        for t in range(npair):
            scores = [s_refs[2 * t + buf][jj * TK:(jj + 1) * TK, :] for jj in range(n_sub)]
            m = functools.reduce(jnp.maximum, [jnp.max(s, axis=0, keepdims=True) + off
                                               for s, off in zip(scores, offsets)])
            l = jnp.zeros((1, 2 * TQ), F32)
            acc = jnp.zeros((LANES, 2 * TQ), F32)
            for jj in range(n_sub):
                p = jnp.exp2(scores[jj] - (m - offsets[jj]))
                l = l + jnp.sum(p, axis=0, keepdims=True)
                acc = acc + lax.dot_general(v_refs[t][starts_rows[jj][1], :], p.astype(BF16),
                                            (((0,), (0,)), ((), ())), preferred_element_type=F32)
            o = acc / l
            o_ref[pl.ds(pl.multiple_of(i * TQ, TQ), TQ), t * LANES:(t + 1) * LANES] = (
                jnp.where(feat < 64, o[:, :TQ], o[:, TQ:]).T.astype(BF16))

    scores_into(0, 0)

    def two_blocks(b, _):
        i = 2 * b
        scores_into(1, i + 1)
        softmax_out(0, i)
        scores_into(0, jnp.minimum(i + 2, nq - 1))
        softmax_out(1, i + 1)
        return 0
    lax.fori_loop(0, nq // 2, two_blocks, 0)


BIAS_SPAN = 1024


def _band_bias_kernel(line_ref, o_ref):
    kj = lax.broadcasted_iota(jnp.int32, (BAND, TQ), 0)
    qi = lax.broadcasted_iota(jnp.int32, (BAND, TQ), 1)
    kc = kj // CHUNK - N_PREV_CHUNKS
    qc = qi // CHUNK
    allowed = jnp.logical_and(kc <= qc, kc >= qc - N_PREV_CHUNKS)
    for hh in range(2):
        line = jnp.broadcast_to(line_ref[hh:hh + 1, :], (BAND, BIAS_SPAN))
        rolled = pltpu.roll(line, BIAS_SPAN - (BAND - 1), 1, stride=1, stride_axis=0)
        o_ref[:, hh * TQ:(hh + 1) * TQ] = jnp.where(allowed, LOG2E * rolled[:, :TQ], NEG)


def _band_bias(rel_bias):
    rb = rel_bias.astype(F32)
    n_low = (BAND - 1) - REL_CLIP
    n_high = (BAND + TQ - 1) - (BAND - 1) - REL_CLIP - 1
    line = jnp.concatenate([jnp.broadcast_to(rb[:, :1], (N_HEADS_B, n_low)), rb,
                            jnp.broadcast_to(rb[:, -1:], (N_HEADS_B, n_high))], axis=1)
    line = jnp.pad(line[:, ::-1], ((0, 0), (0, BIAS_SPAN - line.shape[1])))
    pairs = N_HEADS_B // 2
    return pl.pallas_call(
        _band_bias_kernel,
        grid=(pairs,),
        in_specs=[pl.BlockSpec((None, 2, BIAS_SPAN), lambda p: (p, 0, 0))],
        out_specs=pl.BlockSpec((None, BAND, 2 * TQ), lambda p: (p, 0, 0)),
        out_shape=jax.ShapeDtypeStruct((pairs, BAND, 2 * TQ), F32),
        compiler_params=_params(("parallel",)),
        name="band_bias",
    )(line.reshape(pairs, 2, BIAS_SPAN))


def _band_attention(qkv, bias, batch, seq):
    assert (seq // TQ) % 2 == 0, "query blocks are walked two at a time"
    n = batch * seq
    base = 3 * WIDTH_A // LANES
    pairs = N_HEADS_B // 2
    npair = PAIRS_PER_STEP
    bias_spec = lambda t: pl.BlockSpec((None, BAND, 2 * TQ), lambda b, p: (p * npair + t, 0, 0))
    slab_spec = lambda t, off: pl.BlockSpec((None, seq, LANES), lambda b, p: (base + off + p * npair + t, b, 0))
    return pl.pallas_call(
        _band_attn_kernel,
        grid=(batch, pairs // npair),
        in_specs=[
            *[bias_spec(t) for t in range(npair)],
            *[slab_spec(t, 0) for t in range(npair)],
            *[slab_spec(t, pairs) for t in range(npair)],
            *[slab_spec(t, 2 * pairs) for t in range(npair)],
        ],
        out_specs=pl.BlockSpec((seq, npair * LANES), lambda b, p: (b, p)),
        out_shape=jax.ShapeDtypeStruct((n, WIDTH_B), BF16),
        scratch_shapes=[pltpu.VMEM((BAND, 2 * TQ), F32)] * (2 * npair),
        compiler_params=_params(("parallel", "parallel")),
        name="band_attn",
    )(*([bias] * npair), *([qkv] * (3 * npair)))


def _pack_bf16_pairs(x):
    w = x.shape[1] // 2
    bits = pltpu.bitcast(x.astype(BF16).astype(F32), jnp.uint32)
    return (bits[:, :w] >> 16) | (bits[:, w:] & jnp.uint32(0xFFFF0000))


def _unpack_bf16_pairs(p):
    lo = pltpu.bitcast(p << 16, F32)
    hi = pltpu.bitcast(p & jnp.uint32(0xFFFF0000), F32)
    return jnp.concatenate([lo, hi], axis=1).astype(BF16)


def _merge_kernel(oa_ref, ob_ref, gates_ref, h_ref, pa_ref, pb_ref, wo_ref, g_ref, b_ref, wra_ref, wrb_ref, br_ref,
                  h1_ref, h1p_ref, gw_ref, idx_ref, rank_ref, counts_ref, seen_ref):
    ma = jnp.dot(oa_ref[...], pa_ref[...], preferred_element_type=F32)
    mb = jnp.dot(ob_ref[...], pb_ref[...], preferred_element_type=F32)
    merged = gates_ref[:, :D_MODEL].astype(F32) * ma + gates_ref[:, D_MODEL:].astype(F32) * mb
    m = jnp.dot(merged.astype(BF16), wo_ref[...], preferred_element_type=F32)
    h1 = _layer_norm(DEEPNORM_ALPHA * h_ref[...] + m, g_ref[...], b_ref[...])
    h1_ref[...] = h1
    h1p_ref[...] = _pack_bf16_pairs(h1)

    h_hi = h1.astype(BF16)
    h_lo = (h1 - h_hi.astype(F32)).astype(BF16)
    part_a = _dot_nt(wra_ref[...], h_hi)
    part_b = _dot_nt(wrb_ref[...], h_lo)
    logits = part_a[:N_EXPERTS] + part_a[N_EXPERTS:2 * N_EXPERTS] + part_b[:N_EXPERTS] + br_ref[...]

    tm = logits.shape[1]
    expert = lax.broadcasted_iota(jnp.int32, (N_EXPERTS, tm), 0)
    vals = logits
    top_val, top_idx = [], []
    for _ in range(TOP_K):
        mx = jnp.max(vals, axis=0, keepdims=True)
        sel = jnp.min(jnp.where(vals == mx, expert, N_EXPERTS), axis=0, keepdims=True)
        top_val.append(mx)
        top_idx.append(sel)
        vals = jnp.where(expert == sel, -jnp.inf, vals)
    ex = [jnp.exp(v - top_val[0]) for v in top_val]
    denom = functools.reduce(jnp.add, ex)
    gw_ref[...] = jnp.concatenate([e / denom for e in ex], axis=0)
    idx_ref[...] = jnp.concatenate(top_idx, axis=0)

    @pl.when(pl.program_id(0) == 0)
    def _():
        seen_ref[...] = jnp.zeros_like(seen_ref)

    chosen = functools.reduce(jnp.logical_or, [expert == s for s in top_idx])
    onehot = jnp.where(chosen, 1.0, 0.0)
    r_i = lax.broadcasted_iota(jnp.int32, (tm, tm), 0)
    c_i = lax.broadcasted_iota(jnp.int32, (tm, tm), 1)
    earlier = jnp.where(r_i < c_i, 1.0, 0.0)
    before = jnp.dot(onehot, earlier, preferred_element_type=F32) + seen_ref[...]
    rank_ref[...] = jnp.concatenate([jnp.sum(jnp.where(expert == s, before, 0.0), axis=0, keepdims=True)
                                     for s in top_idx], axis=0).astype(jnp.int32)
    seen_ref[...] = seen_ref[...] + jnp.sum(onehot, axis=1, keepdims=True)
    counts_ref[...] = seen_ref[...].astype(jnp.int32)


def _merge(oa, ob, gates, h, pa, pb, wo, g, b, wr, br):
    n = h.shape[0]
    tm = min(TM_MERGE, n)
    wr_hi = wr.astype(BF16)
    wr_lo = (wr - wr_hi.astype(F32)).astype(BF16)
    pad = lambda a: jnp.pad(a, ((0, 0), (0, LANES - a.shape[1]))).T
    row = lambda w: pl.BlockSpec((tm, w), lambda i: (i, 0))
    full = lambda r, c: pl.BlockSpec((r, c), lambda i: (0, 0))
    slot_row = pl.BlockSpec((TOP_K, tm), lambda i: (0, i))
    return pl.pallas_call(
        _merge_kernel,
        grid=(n // tm,),
        in_specs=[row(WIDTH_A), row(WIDTH_B), row(GATE_COLS), row(D_MODEL),
                  full(WIDTH_A, D_MODEL), full(WIDTH_B, D_MODEL), full(D_MODEL, D_MODEL),
                  full(1, D_MODEL), full(1, D_MODEL), full(LANES, D_MODEL), full(LANES, D_MODEL),
                  full(N_EXPERTS, 1)],
        out_specs=[row(D_MODEL), row(D_MODEL // 2), slot_row, slot_row, slot_row, full(N_EXPERTS, 1)],
        out_shape=[jax.ShapeDtypeStruct((n, D_MODEL), F32), jax.ShapeDtypeStruct((n, D_MODEL // 2), jnp.uint32),
                   jax.ShapeDtypeStruct((TOP_K, n), F32), jax.ShapeDtypeStruct((TOP_K, n), jnp.int32),
                   jax.ShapeDtypeStruct((TOP_K, n), jnp.int32), jax.ShapeDtypeStruct((N_EXPERTS, 1), jnp.int32)],
        scratch_shapes=[pltpu.VMEM((N_EXPERTS, 1), F32)],
        compiler_params=_params(("arbitrary",)),
        name="merge_ln1_router",
    )(oa, ob, gates, h, pa, pb, wo, g, b, pad(jnp.concatenate([wr_hi, wr_lo], axis=1)), pad(wr_hi),
      br.reshape(N_EXPERTS, 1))


def _dispatch_kernel(pend_ref, dest_ref, h1p_ref, xs_hbm, zero_ref, sem, zero_sem):
    @pl.when(pl.program_id(0) == 0)
    def _():
        zero_ref[...] = jnp.zeros_like(zero_ref)

        def fill(row0):
            block = pl.ds(pl.multiple_of(row0, ROWS_MOE), ROWS_MOE)
            return pltpu.make_async_copy(zero_ref, xs_hbm.at[block, :], zero_sem)
        for e in range(N_EXPERTS):
            fill(jnp.maximum(pend_ref[e] - ROWS_MOE, 0)).start()
        for e in range(N_EXPERTS):
            fill(0).wait()

        first_unused = pend_ref[N_EXPERTS - 1] // ROWS_MOE
        n_blocks = xs_hbm.shape[0] // ROWS_MOE

        @pl.loop(first_unused, n_blocks)
        def _(blk):
            fill(blk * ROWS_MOE).start()

        @pl.loop(first_unused, n_blocks)
        def _(blk):
            fill(0).wait()

    def start(group, _):
        t0 = pl.multiple_of(group * SUBLANES, SUBLANES)
        for j in range(SUBLANES):
            for k in range(TOP_K):
                pltpu.make_async_copy(h1p_ref.at[pl.ds(t0 + j, 1), :],
                                      xs_hbm.at[pl.ds(dest_ref[0, 0, (t0 + j) * TOP_K + k], 1), :],
                                      sem).start(priority=k % 2)
        return 0
    lax.fori_loop(0, TM_DISPATCH // SUBLANES, start, 0)

    for k in range(TOP_K):
        pltpu.make_async_copy(h1p_ref, xs_hbm.at[pl.ds(0, TM_DISPATCH), :], sem).wait()


def _dispatch(dest, pend, h1p, n_rows):
    n = h1p.shape[0]
    tm = min(TM_DISPATCH, n)
    assert tm == TM_DISPATCH
    n_steps = n // tm
    grid_spec = pltpu.PrefetchScalarGridSpec(
        num_scalar_prefetch=1,
        grid=(n_steps,),
        in_specs=[
            pl.BlockSpec((1, 1, tm * TOP_K), lambda i, pe: (i, 0, 0), memory_space=pltpu.SMEM),
            pl.BlockSpec((tm, D_MODEL // 2), lambda i, pe: (i, 0)),
        ],
        out_specs=pl.BlockSpec(memory_space=pl.ANY),
        scratch_shapes=[pltpu.VMEM((ROWS_MOE, D_MODEL // 2), jnp.uint32), pltpu.SemaphoreType.DMA(()),
                        pltpu.SemaphoreType.DMA(())],
    )
    return pl.pallas_call(
        _dispatch_kernel,
        grid_spec=grid_spec,
        out_shape=jax.ShapeDtypeStruct((n_rows, D_MODEL // 2), jnp.uint32),
        compiler_params=_params(("arbitrary",)),
        name="dispatch_rows",
    )(pend, dest.reshape(n_steps, 1, tm * TOP_K), h1p)


def _row_gather(src_hbm, idx_ref, dst_ref, sem, n_rows):
    def start(group, _):
        for j in range(SUBLANES):
            pltpu.make_async_copy(src_hbm.at[pl.ds(idx_ref[0, 0, group * SUBLANES + j], 1), :],
                                  dst_ref.at[group, pl.ds(j, 1), :], sem).start(priority=j % 2)
        return 0
    lax.fori_loop(0, n_rows // SUBLANES, start, 0)


def _moe_kernel(bexp_ref, nused_ref, x_ref, wi_ref, bi_ref, wo_ref, bo_ref, y_ref, wi_bf, wo_bf):
    i = pl.program_id(0)
    n_used = nused_ref[0]
    new_expert = jnp.logical_or(i == 0, bexp_ref[i] != bexp_ref[jnp.maximum(i - 1, 0)])

    @pl.when(jnp.logical_and(new_expert, i < n_used))
    def _():
        wi_bf[...] = wi_ref[...].astype(BF16)
        wo_bf[...] = wo_ref[...].astype(BF16)

    @pl.when(i < n_used)
    def _():
        x = _unpack_bf16_pairs(x_ref[...])
        hb = jnp.dot(x, wi_bf[...], preferred_element_type=F32) + bi_ref[...]
        g = jnp.minimum(hb[:, :D_FF], SWIGLU_LIMIT)
        u = jnp.clip(hb[:, D_FF:], -SWIGLU_LIMIT, SWIGLU_LIMIT)
        a = g * jax.nn.sigmoid(SWIGLU_ALPHA * g) * (u + 1.0)
        y_ref[...] = jnp.dot(a.astype(BF16), wo_bf[...], preferred_element_type=F32) + bo_ref[...]

    @pl.when(i >= n_used)
    def _():
        y_ref[...] = jnp.zeros_like(y_ref)


def _moe(block_exp, n_used, x_sorted, wi, bi, wo, bo):
    n_blocks = block_exp.shape[0]
    x_block = lambda i, be, nu: (jnp.maximum(jnp.minimum(i, nu[0] - 1), 0), 0)
    grid_spec = pltpu.PrefetchScalarGridSpec(
        num_scalar_prefetch=2,
        grid=(n_blocks,),
        in_specs=[
            pl.BlockSpec((ROWS_MOE, D_MODEL // 2), x_block),
            pl.BlockSpec((None, D_MODEL, 2 * D_FF), lambda i, be, nu: (be[i], 0, 0)),
            pl.BlockSpec((None, 1, 2 * D_FF), lambda i, be, nu: (be[i], 0, 0)),
            pl.BlockSpec((None, D_FF, D_MODEL), lambda i, be, nu: (be[i], 0, 0)),
            pl.BlockSpec((None, 1, D_MODEL), lambda i, be, nu: (be[i], 0, 0)),
        ],
        out_specs=pl.BlockSpec((ROWS_MOE, D_MODEL), lambda i, be, nu: (i, 0)),
        scratch_shapes=[pltpu.VMEM((D_MODEL, 2 * D_FF), BF16), pltpu.VMEM((D_FF, D_MODEL), BF16)],
    )
    return pl.pallas_call(
        _moe_kernel,
        grid_spec=grid_spec,
        out_shape=jax.ShapeDtypeStruct((n_blocks * ROWS_MOE, D_MODEL), F32),
        compiler_params=_params(("arbitrary",)),
        name="moe_experts",
    )(block_exp, n_used, x_sorted, wi, bi, wo, bo)


def _combine_kernel(dest_cur_ref, dest_nxt_ref, gw_ref, h1_ref, g_ref, b_ref, y_hbm, o_ref, ybuf, sems):
    i = pl.program_id(0)
    n_steps = pl.num_programs(0)
    slot = i % 2
    rows = TOP_K * TM_COMB

    @pl.when(i == 0)
    def _():
        _row_gather(y_hbm, dest_cur_ref, ybuf.at[0], sems.at[0], rows)

    @pl.when(i + 1 < n_steps)
    def _():
        _row_gather(y_hbm, dest_nxt_ref, ybuf.at[1 - slot], sems.at[1 - slot], rows)

    pltpu.make_async_copy(ybuf.at[slot], ybuf.at[slot], sems.at[slot]).wait()
    gw = gw_ref[...]
    f = jnp.zeros((TM_COMB, D_MODEL), F32)
    tiles = TM_COMB // SUBLANES
    for k in range(TOP_K):
        y_k = ybuf[slot, k * tiles:(k + 1) * tiles].reshape(TM_COMB, D_MODEL)
        f = f + gw[:, k:k + 1] * y_k
    o_ref[...] = _layer_norm(DEEPNORM_ALPHA * h1_ref[...] + f, g_ref[...], b_ref[...])


def _combine(dest_km, gate_w, h1, g, b, y_buf):
    n = h1.shape[0]
    n_steps = n // TM_COMB
    rows = TOP_K * TM_COMB
    return pl.pallas_call(
        _combine_kernel,
        grid=(n_steps,),
        in_specs=[
            pl.BlockSpec((1, 1, rows), lambda i: (i, 0, 0), memory_space=pltpu.SMEM),
            pl.BlockSpec((1, 1, rows), lambda i: (jnp.minimum(i + 1, n_steps - 1), 0, 0), memory_space=pltpu.SMEM),
            pl.BlockSpec((TM_COMB, TOP_K), lambda i: (i, 0)),
            pl.BlockSpec((TM_COMB, D_MODEL), lambda i: (i, 0)),
            pl.BlockSpec((1, D_MODEL), lambda i: (0, 0)),
            pl.BlockSpec((1, D_MODEL), lambda i: (0, 0)),
            pl.BlockSpec(memory_space=pl.ANY),
        ],
        out_specs=pl.BlockSpec((TM_COMB, D_MODEL), lambda i: (i, 0)),
        out_shape=jax.ShapeDtypeStruct((n, D_MODEL), F32),
        scratch_shapes=[pltpu.VMEM((2, rows // SUBLANES, SUBLANES, D_MODEL), F32), pltpu.SemaphoreType.DMA((2,))],
        compiler_params=_params(("arbitrary",)),
        name="combine_ln2",
    )(dest_km, dest_km, gate_w, h1, g, b, y_buf)


def _block_layout(top_idx, rank, counts):
    n_slots = top_idx.shape[1] * TOP_K
    padded = (counts + ROWS_MOE - 1) // ROWS_MOE * ROWS_MOE
    pend = jnp.cumsum(padded)
    pstart = pend - padded
    experts = jnp.arange(N_EXPERTS, dtype=jnp.int32)[:, None, None]
    dest = rank + jnp.sum(jnp.where(top_idx[None] == experts, pstart[:, None, None], 0), axis=0)
    n_blocks = n_slots // ROWS_MOE + N_EXPERTS
    block_start = jnp.arange(n_blocks, dtype=jnp.int32) * ROWS_MOE
    block_exp = jnp.minimum(jnp.sum(block_start[:, None] >= pend[None, :], axis=-1), N_EXPERTS - 1).astype(jnp.int32)
    n_used = (pend[-1] // ROWS_MOE).astype(jnp.int32).reshape(1)
    return dest.astype(jnp.int32), pend.astype(jnp.int32), block_exp, n_used


def kernel(x, ln_in_g, ln_in_b, w_in, b_gate, lambda_q1, lambda_k1, lambda_q2, lambda_k2, subln_w, rel_bias,
           w_branch_a, w_branch_b, w_out, ln1_g, ln1_b, w_router, b_router, w_exp_in, b_exp_in, w_exp_out,
           b_exp_out, ln2_g, ln2_b):
    batch, seq, d = x.shape
    n = batch * seq
    row = lambda a: a.reshape(1, -1).astype(F32)
    l = 0
    h, qkv, gates = _ln_proj(x.reshape(n, d), row(ln_in_g), row(ln_in_b), w_in[l].astype(BF16), row(b_gate[l]))
    slopes = jnp.asarray([2.0 ** (-8.0 * (i + 1) / N_HEADS_A) for i in range(N_HEADS_A)], F32)
    out_a = _diff_attention(qkv, slopes, row(lambda_q1[l]), row(lambda_k1[l]), row(lambda_q2[l]),
                            row(lambda_k2[l]), subln_w[l].reshape(-1, 1).astype(F32), batch, seq)
    out_b = _band_attention(qkv, _band_bias(rel_bias[l]), batch, seq)
    h1, h1p, gate_w, top_idx, rank, counts = _merge(
        out_a, out_b, gates, h, w_branch_a[l].astype(BF16), w_branch_b[l].astype(BF16), w_out[l].astype(BF16),
        row(ln1_g[l]), row(ln1_b[l]), w_router[l].astype(F32), row(b_router[l]))
    dest, pend, block_exp, n_used = _block_layout(top_idx, rank, counts[:, 0])
    x_sorted = _dispatch(dest.T, pend, h1p, block_exp.shape[0] * ROWS_MOE)
    y_buf = _moe(block_exp, n_used, x_sorted, w_exp_in[l].astype(F32), b_exp_in[l].reshape(N_EXPERTS, 1, -1),
                 w_exp_out[l].astype(F32), b_exp_out[l].reshape(N_EXPERTS, 1, -1))
    n_steps = n // TM_COMB
    dest_km = dest.reshape(TOP_K, n_steps, TM_COMB).transpose(1, 0, 2).reshape(n_steps, 1, TOP_K * TM_COMB)
    out = _combine(dest_km, gate_w.T, h1, row(ln2_g[l]), row(ln2_b[l]), y_buf)
    return out.reshape(batch, seq, d)
```

```python
import functools
import math

import jax
import jax.numpy as jnp
import numpy as np
from jax import lax
from jax.experimental import pallas as pl
from jax.experimental.pallas import tpu as pltpu

F32 = jnp.float32
BF16 = jnp.bfloat16

D_MODEL = 1024
CHUNK = 64
N_HEADS_A = 4
HEAD_DIM_A = 64
WIDTH_A = 512
N_HEADS_B = 8
HEAD_DIM_B = 64
WIDTH_B = 512
N_PREV_CHUNKS = 8
REL_CLIP = 128
N_EXPERTS = 32
TOP_K = 4
D_FF = 1024
SWIGLU_ALPHA = 1.702
SWIGLU_LIMIT = 7.0
DEEPNORM_ALPHA = 2.0 ** 0.25
LN_EPS = 1e-5
LAM_INIT = 0.8 - 0.6 * math.exp(-0.3 * 0)

LANES = 128
SUBLANES = 8
N_SLABS = (3 * WIDTH_A + 3 * WIDTH_B) // LANES
GATE_COLS = 2 * D_MODEL
IN_COLS = 3 * WIDTH_A + 3 * WIDTH_B + GATE_COLS
LOG2E = math.log2(math.e)
NEG = -1e30

TM_PROJ = 512
PROJ_CHUNK = 512
TQ = 256
TK = 256
KV_GROUP = 2
HEADS_PER_STEP = 4
PAIRS_PER_STEP = 2
BAND = 3 * TK
TM_MERGE = 512
TM_DISPATCH = 512
ROWS_MOE = 512
TM_COMB = 256
VMEM_LIMIT = 56 * 1024 * 1024


def _layer_norm(x, g, b):
    mu = jnp.mean(x, axis=-1, keepdims=True)
    xc = x - mu
    var = jnp.mean(xc * xc, axis=-1, keepdims=True)
    return xc * lax.rsqrt(var + LN_EPS) * g + b


def _params(sem):
    return pltpu.CompilerParams(dimension_semantics=sem, vmem_limit_bytes=VMEM_LIMIT)


def _ln_proj_kernel(x_ref, g_ref, b_ref, w_ref, bg_ref, h_ref, qkv_ref, gates_ref):
    h = _layer_norm(x_ref[...], g_ref[...], b_ref[...])
    h_ref[...] = h
    hb = h.astype(BF16)
    n_qkv_chunks = (N_SLABS * LANES) // PROJ_CHUNK
    slabs_per_chunk = PROJ_CHUNK // LANES
    q_scale = HEAD_DIM_A ** -0.5 * LOG2E
    for c in range(n_qkv_chunks):
        r = jnp.dot(hb, w_ref[:, c * PROJ_CHUNK:(c + 1) * PROJ_CHUNK], preferred_element_type=F32)
        first = c * slabs_per_chunk
        is_q = (first < WIDTH_A // LANES) or (3 * WIDTH_A // LANES <= first < (3 * WIDTH_A + WIDTH_B) // LANES)
        if is_q:
            r = r * q_scale
        for s in range(slabs_per_chunk):
            qkv_ref[first + s] = r[:, s * LANES:(s + 1) * LANES].astype(BF16)
    g0 = N_SLABS * LANES
    for c in range(GATE_COLS // PROJ_CHUNK):
        r = jnp.dot(hb, w_ref[:, g0 + c * PROJ_CHUNK:g0 + (c + 1) * PROJ_CHUNK], preferred_element_type=F32)
        r = r + bg_ref[:, c * PROJ_CHUNK:(c + 1) * PROJ_CHUNK]
        gates_ref[:, c * PROJ_CHUNK:(c + 1) * PROJ_CHUNK] = jax.nn.sigmoid(r).astype(BF16)


def _ln_proj(x2d, g, b, w_bf16, b_gate):
    n = x2d.shape[0]
    tm = min(TM_PROJ, n)
    const = lambda i: (0, 0)
    return pl.pallas_call(
        _ln_proj_kernel,
        grid=(n // tm,),
        in_specs=[
            pl.BlockSpec((tm, D_MODEL), lambda i: (i, 0)),
            pl.BlockSpec((1, D_MODEL), const),
            pl.BlockSpec((1, D_MODEL), const),
            pl.BlockSpec((D_MODEL, IN_COLS), const, pipeline_mode=pl.Buffered(1)),
            pl.BlockSpec((1, GATE_COLS), const),
        ],
        out_specs=[
            pl.BlockSpec((tm, D_MODEL), lambda i: (i, 0)),
            pl.BlockSpec((N_SLABS, tm, LANES), lambda i: (0, i, 0)),
            pl.BlockSpec((tm, GATE_COLS), lambda i: (i, 0)),
        ],
        out_shape=[
            jax.ShapeDtypeStruct((n, D_MODEL), F32),
            jax.ShapeDtypeStruct((N_SLABS, n, LANES), BF16),
            jax.ShapeDtypeStruct((n, GATE_COLS), BF16),
        ],
        compiler_params=_params(("parallel",)),
        name="ln_proj",
    )(x2d, g, b, w_bf16, b_gate)


def _stack_halves(q):
    lane = lax.broadcasted_iota(jnp.int32, q.shape, 1)
    zero = jnp.zeros_like(q)
    return jnp.concatenate([jnp.where(lane < 64, q, zero), jnp.where(lane >= 64, q, zero)], axis=0)


def _dot_nt(a, b):
    return lax.dot_general(a, b, (((1,), (1,)), ((), ())), preferred_element_type=F32)


def _split3_bf16(x):
    hi = x.astype(BF16)
    r1 = x - hi.astype(F32)
    mid = r1.astype(BF16)
    lo = (r1 - mid.astype(F32)).astype(BF16)
    return hi, mid, lo


class _DiffHead:
    def __init__(self, slope, i, q_ref, k_ref, v_ref, s_refs, mx_refs, m_ref, l_ref, acc_ref):
        self.slope, self.i = slope, i
        self.k_ref, self.v_ref, self.s_refs, self.mx_refs = k_ref, v_ref, s_refs, mx_refs
        self.m_ref, self.l_ref, self.acc_ref = m_ref, l_ref, acc_ref
        lane_q = lax.broadcasted_iota(jnp.int32, (2 * TQ, LANES), 1)
        self.q_aug = jnp.concatenate([_stack_halves(q_ref[...]), jnp.where(lane_q < 3, 1.0, 0.0).astype(BF16)],
                                     axis=1)
        lane_k = lax.broadcasted_iota(jnp.int32, (TK, LANES), 1)
        key_pos = lax.broadcasted_iota(jnp.int32, (TK, LANES), 0)
        self.k_bias = []
        for g in range(KV_GROUP):
            hi, mid, lo = [t.astype(F32) for t in _split3_bf16(slope * (key_pos + g * TK).astype(F32))]
            self.k_bias.append(jnp.where(lane_k == 0, hi, jnp.where(lane_k == 1, mid, jnp.where(lane_k == 2, lo, 0.0))
                                         ).astype(BF16))
        m_ref[...] = jnp.full(m_ref.shape, NEG, F32)
        l_ref[...] = jnp.zeros(l_ref.shape, F32)
        acc_ref[...] = jnp.zeros(acc_ref.shape, F32)

    @staticmethod
    def _group_rows(a):
        return pl.ds(pl.multiple_of(a * (KV_GROUP * TK), KV_GROUP * TK), KV_GROUP * TK)

    def scores_into(self, buf, a):
        mx = None
        for g in range(KV_GROUP):
            rows = pl.ds(pl.multiple_of((a * KV_GROUP + g) * TK, TK), TK)
            s = _dot_nt(jnp.concatenate([self.k_ref[rows, :], self.k_bias[g]], axis=1), self.q_aug)
            self.s_refs[buf][g * TK:(g + 1) * TK, :] = s
            mg = jnp.max(s, axis=0, keepdims=True)
            mx = mg if mx is None else jnp.maximum(mx, mg)
        self.mx_refs[buf][...] = mx

    def _group_offset(self, a):
        return self.slope * ((a * KV_GROUP - self.i) * TK).astype(F32)

    def _update(self, a, s, mx):
        off = self._group_offset(a)
        m_prev = self.m_ref[...]
        m_new = jnp.maximum(m_prev, mx + off)
        alpha = jnp.exp2(m_prev - m_new)
        p = jnp.exp2(s - (m_new - off))
        self.m_ref[...] = m_new
        self.l_ref[...] = alpha * self.l_ref[...] + jnp.sum(p, axis=0, keepdims=True)
        self.acc_ref[...] = alpha * self.acc_ref[...] + lax.dot_general(
            self.v_ref[self._group_rows(a), :], p.astype(BF16), (((0,), (0,)), ((), ())),
            preferred_element_type=F32)

    def full_update(self, buf, a):
        self._update(a, self.s_refs[buf][...], self.mx_refs[buf][...])

    def last_update(self, buf, a):
        kk = lax.broadcasted_iota(jnp.int32, (TK, TQ), 0)
        qq = lax.broadcasted_iota(jnp.int32, (TK, TQ), 1)
        diag = jnp.where(kk // CHUNK <= qq // CHUNK, self.slope * (qq - jnp.abs(qq - kk) - kk).astype(F32), NEG)
        diag = jnp.concatenate([diag, diag], axis=1)
        off = self._group_offset(a)
        scores, offsets = [], []
        for g in range(KV_GROUP):
            jb = a * KV_GROUP + g
            scores.append(self.s_refs[buf][g * TK:(g + 1) * TK, :] + jnp.where(jb == self.i, diag, 0.0))
            offsets.append(jnp.where(jb > self.i, NEG, off))
        m_prev = self.m_ref[...]
        m_new = m_prev
        for s, o in zip(scores, offsets):
            m_new = jnp.maximum(m_new, jnp.max(s, axis=0, keepdims=True) + o)
        alpha = jnp.exp2(m_prev - m_new)
        p = jnp.concatenate([jnp.exp2(s - (m_new - o)) for s, o in zip(scores, offsets)], axis=0)
        self.m_ref[...] = m_new
        self.l_ref[...] = alpha * self.l_ref[...] + jnp.sum(p, axis=0, keepdims=True)
        self.acc_ref[...] = alpha * self.acc_ref[...] + lax.dot_general(
            self.v_ref[self._group_rows(a), :], p.astype(BF16), (((0,), (0,)), ((), ())),
            preferred_element_type=F32)

    def output(self, lam, subln):
        o = self.acc_ref[...] / self.l_ref[...]
        o = o[:, :TQ] - lam * o[:, TQ:]
        o = o * lax.rsqrt(jnp.mean(o * o, axis=0, keepdims=True) + LN_EPS) * subln
        return (o * (1.0 - LAM_INIT)).T.astype(BF16)


def _diff_attn_kernel(slopes_ref, lq1_ref, lk1_ref, lq2_ref, lk2_ref, subln_ref, *refs):
    hp = pl.program_id(1)
    i = pl.program_id(2)
    nh = HEADS_PER_STEP
    q_refs, k_refs, v_refs = refs[:nh], refs[nh:2 * nh], refs[2 * nh:3 * nh]
    o_ref = refs[3 * nh]
    scratch = refs[3 * nh + 1:]
    heads = []
    for t in range(nh):
        s0, s1, mx0, mx1, m, l, acc = scratch[7 * t:7 * t + 7]
        heads.append(_DiffHead(slopes_ref[hp * nh + t] * LOG2E, i, q_refs[t], k_refs[t], v_refs[t],
                               (s0, s1), (mx0, mx1), m, l, acc))

    n_full = i // KV_GROUP
    odd = n_full % 2

    @pl.when(odd == 1)
    def _():
        for h in heads:
            h.scores_into(1, 0)
        for h in heads:
            h.scores_into(0, 1)
        for h in heads:
            h.full_update(1, 0)

    @pl.when(odd == 0)
    def _():
        for h in heads:
            h.scores_into(0, 0)

    def pair(b, _):
        a = odd + 2 * b
        for h in heads:
            h.scores_into(1, a + 1)
        for h in heads:
            h.full_update(0, a)
        for h in heads:
            h.scores_into(0, a + 2)
        for h in heads:
            h.full_update(1, a + 1)
        return 0
    lax.fori_loop(0, (n_full - odd) // 2, pair, 0)

    for h in heads:
        h.last_update(0, n_full)

    lam = (jnp.exp(jnp.sum(lq1_ref[...] * lk1_ref[...], axis=-1, keepdims=True))
           - jnp.exp(jnp.sum(lq2_ref[...] * lk2_ref[...], axis=-1, keepdims=True)) + LAM_INIT)
    for t, h in enumerate(heads):
        o_ref[:, t * LANES:(t + 1) * LANES] = h.output(lam, subln_ref[...])


def _diff_attention(qkv, slopes, lq1, lk1, lq2, lk2, subln, batch, seq):
    nq = seq // TQ
    assert seq % (TQ * KV_GROUP) == 0, "the last key group of a query block must stay inside the sequence"
    n = batch * seq
    nh = HEADS_PER_STEP
    vec = lambda w: pl.BlockSpec((1, w), lambda b, h, i: (0, 0))
    q_spec = lambda t: pl.BlockSpec((None, TQ, LANES), lambda b, h, i: (h * nh + t, b * nq + i, 0))
    kv_spec = lambda t, base: pl.BlockSpec((None, seq, LANES), lambda b, h, i: (base + h * nh + t, b, 0))
    stat = pltpu.VMEM((1, 2 * TQ), F32)
    head_scratch = [pltpu.VMEM((KV_GROUP * TK, 2 * TQ), F32), pltpu.VMEM((KV_GROUP * TK, 2 * TQ), F32),
                    stat, stat, stat, stat, pltpu.VMEM((LANES, 2 * TQ), F32)]
    return pl.pallas_call(
        _diff_attn_kernel,
        grid=(batch, N_HEADS_A // nh, nq),
        in_specs=[
            pl.BlockSpec(memory_space=pltpu.SMEM),
            vec(HEAD_DIM_A), vec(HEAD_DIM_A), vec(HEAD_DIM_A), vec(HEAD_DIM_A),
            pl.BlockSpec((2 * HEAD_DIM_A, 1), lambda b, h, i: (0, 0)),
            *[q_spec(t) for t in range(nh)],
            *[kv_spec(t, N_HEADS_A) for t in range(nh)],
            *[kv_spec(t, 2 * N_HEADS_A) for t in range(nh)],
        ],
        out_specs=pl.BlockSpec((TQ, nh * LANES), lambda b, h, i: (b * nq + i, h)),
        out_shape=jax.ShapeDtypeStruct((n, WIDTH_A), BF16),
        scratch_shapes=head_scratch * nh,
        compiler_params=_params(("parallel", "parallel", "arbitrary")),
        name="diff_attn",
    )(slopes, lq1, lk1, lq2, lk2, subln, *([qkv] * (3 * nh)))


def _band_attn_kernel(*refs):
    npair = PAIRS_PER_STEP
    bias_refs, q_refs = refs[:npair], refs[npair:2 * npair]
    k_refs, v_refs = refs[2 * npair:3 * npair], refs[3 * npair:4 * npair]
    o_ref = refs[4 * npair]
    s_refs = refs[4 * npair + 1:]
    n_sub = BAND // TK
    nq = o_ref.shape[0] // TQ
    feat = lax.broadcasted_iota(jnp.int32, (LANES, TQ), 0)

    def key_rows(i, jj):
        start = i * TQ - N_PREV_CHUNKS * CHUNK + jj * TK
        return start, pl.ds(pl.multiple_of(jnp.maximum(start, 0), TK), TK)

    def scores_into(buf, i):
        for t in range(npair):
            q2 = _stack_halves(q_refs[t][pl.ds(pl.multiple_of(i * TQ, TQ), TQ), :])
            for jj in range(n_sub):
                _, rows = key_rows(i, jj)
                s_refs[2 * t + buf][jj * TK:(jj + 1) * TK, :] = (
                    _dot_nt(k_refs[t][rows, :], q2) + bias_refs[t][jj * TK:(jj + 1) * TK, :])

    def softmax_out(buf, i):
        starts_rows = [key_rows(i, jj) for jj in range(n_sub)]
        offsets = [jnp.where(start < 0, NEG, 0.0).astype(F32) for start, _ in starts_rows]
        for t in range(npair):
            scores = [s_refs[2 * t + buf][jj * TK:(jj + 1) * TK, :] for jj in range(n_sub)]
            m = functools.reduce(jnp.maximum, [jnp.max(s, axis=0, keepdims=True) + off
                                               for s, off in zip(scores, offsets)])
            l = jnp.zeros((1, 2 * TQ), F32)
            acc = jnp.zeros((LANES, 2 * TQ), F32)
            for jj in range(n_sub):
                p = jnp.exp2(scores[jj] - (m - offsets[jj]))
                l = l + jnp.sum(p, axis=0, keepdims=True)
                acc = acc + lax.dot_general(v_refs[t][starts_rows[jj][1], :], p.astype(BF16),
                                            (((0,), (0,)), ((), ())), preferred_element_type=F32)
            o = acc / l
            o_ref[pl.ds(pl.multiple_of(i * TQ, TQ), TQ), t * LANES:(t + 1) * LANES] = (
                jnp.where(feat < 64, o[:, :TQ], o[:, TQ:]).T.astype(BF16))

    scores_into(0, 0)

    def two_blocks(b, _):
        i = 2 * b
        scores_into(1, i + 1)
        softmax_out(0, i)
        scores_into(0, jnp.minimum(i + 2, nq - 1))
        softmax_out(1, i + 1)
        return 0
    lax.fori_loop(0, nq // 2, two_blocks, 0)


BIAS_SPAN = 1024


def _band_bias_kernel(line_ref, o_ref):
    kj = lax.broadcasted_iota(jnp.int32, (BAND, TQ), 0)
    qi = lax.broadcasted_iota(jnp.int32, (BAND, TQ), 1)
    kc = kj // CHUNK - N_PREV_CHUNKS
    qc = qi // CHUNK
    allowed = jnp.logical_and(kc <= qc, kc >= qc - N_PREV_CHUNKS)
    for hh in range(2):
        line = jnp.broadcast_to(line_ref[hh:hh + 1, :], (BAND, BIAS_SPAN))
        rolled = pltpu.roll(line, BIAS_SPAN - (BAND - 1), 1, stride=1, stride_axis=0)
        o_ref[:, hh * TQ:(hh + 1) * TQ] = jnp.where(allowed, LOG2E * rolled[:, :TQ], NEG)


def _band_bias(rel_bias):
    rb = rel_bias.astype(F32)
    n_low = (BAND - 1) - REL_CLIP
    n_high = (BAND + TQ - 1) - (BAND - 1) - REL_CLIP - 1
    line = jnp.concatenate([jnp.broadcast_to(rb[:, :1], (N_HEADS_B, n_low)), rb,
                            jnp.broadcast_to(rb[:, -1:], (N_HEADS_B, n_high))], axis=1)
    line = jnp.pad(line[:, ::-1], ((0, 0), (0, BIAS_SPAN - line.shape[1])))
    pairs = N_HEADS_B // 2
    return pl.pallas_call(
        _band_bias_kernel,
        grid=(pairs,),
        in_specs=[pl.BlockSpec((None, 2, BIAS_SPAN), lambda p: (p, 0, 0))],
        out_specs=pl.BlockSpec((None, BAND, 2 * TQ), lambda p: (p, 0, 0)),
        out_shape=jax.ShapeDtypeStruct((pairs, BAND, 2 * TQ), F32),
        compiler_params=_params(("parallel",)),
        name="band_bias",
    )(line.reshape(pairs, 2, BIAS_SPAN))


def _band_attention(qkv, bias, batch, seq):
    assert (seq // TQ) % 2 == 0, "query blocks are walked two at a time"
    n = batch * seq
    base = 3 * WIDTH_A // LANES
    pairs = N_HEADS_B // 2
    npair = PAIRS_PER_STEP
    bias_spec = lambda t: pl.BlockSpec((None, BAND, 2 * TQ), lambda b, p: (p * npair + t, 0, 0))
    slab_spec = lambda t, off: pl.BlockSpec((None, seq, LANES), lambda b, p: (base + off + p * npair + t, b, 0))
    return pl.pallas_call(
        _band_attn_kernel,
        grid=(batch, pairs // npair),
        in_specs=[
            *[bias_spec(t) for t in range(npair)],
            *[slab_spec(t, 0) for t in range(npair)],
            *[slab_spec(t, pairs) for t in range(npair)],
            *[slab_spec(t, 2 * pairs) for t in range(npair)],
        ],
        out_specs=pl.BlockSpec((seq, npair * LANES), lambda b, p: (b, p)),
        out_shape=jax.ShapeDtypeStruct((n, WIDTH_B), BF16),
        scratch_shapes=[pltpu.VMEM((BAND, 2 * TQ), F32)] * (2 * npair),
        compiler_params=_params(("parallel", "parallel")),
        name="band_attn",
    )(*([bias] * npair), *([qkv] * (3 * npair)))


def _pack_bf16_pairs(x):
    w = x.shape[1] // 2
    bits = pltpu.bitcast(x.astype(BF16).astype(F32), jnp.uint32)
    return (bits[:, :w] >> 16) | (bits[:, w:] & jnp.uint32(0xFFFF0000))


def _unpack_bf16_pairs(p):
    lo = pltpu.bitcast(p << 16, F32)
    hi = pltpu.bitcast(p & jnp.uint32(0xFFFF0000), F32)
    return jnp.concatenate([lo, hi], axis=1).astype(BF16)


def _merge_kernel(oa_ref, ob_ref, gates_ref, h_ref, pa_ref, pb_ref, wo_ref, g_ref, b_ref, wra_ref, wrb_ref, br_ref,
                  h1_ref, h1p_ref, gw_ref, idx_ref, rank_ref, counts_ref, seen_ref):
    ma = jnp.dot(oa_ref[...], pa_ref[...], preferred_element_type=F32)
    mb = jnp.dot(ob_ref[...], pb_ref[...], preferred_element_type=F32)
    merged = gates_ref[:, :D_MODEL].astype(F32) * ma + gates_ref[:, D_MODEL:].astype(F32) * mb
    m = jnp.dot(merged.astype(BF16), wo_ref[...], preferred_element_type=F32)
    h1 = _layer_norm(DEEPNORM_ALPHA * h_ref[...] + m, g_ref[...], b_ref[...])
    h1_ref[...] = h1
    h1p_ref[...] = _pack_bf16_pairs(h1)

    h_hi = h1.astype(BF16)
    h_lo = (h1 - h_hi.astype(F32)).astype(BF16)
    part_a = _dot_nt(wra_ref[...], h_hi)
    part_b = _dot_nt(wrb_ref[...], h_lo)
    logits = part_a[:N_EXPERTS] + part_a[N_EXPERTS:2 * N_EXPERTS] + part_b[:N_EXPERTS] + br_ref[...]

    tm = logits.shape[1]
    expert = lax.broadcasted_iota(jnp.int32, (N_EXPERTS, tm), 0)
    vals = logits
    top_val, top_idx = [], []
    for _ in range(TOP_K):
        mx = jnp.max(vals, axis=0, keepdims=True)
        sel = jnp.min(jnp.where(vals == mx, expert, N_EXPERTS), axis=0, keepdims=True)
        top_val.append(mx)
        top_idx.append(sel)
        vals = jnp.where(expert == sel, -jnp.inf, vals)
    ex = [jnp.exp(v - top_val[0]) for v in top_val]
    denom = functools.reduce(jnp.add, ex)
    gw_ref[...] = jnp.concatenate([e / denom for e in ex], axis=0)
    idx_ref[...] = jnp.concatenate(top_idx, axis=0)

    @pl.when(pl.program_id(0) == 0)
    def _():
        seen_ref[...] = jnp.zeros_like(seen_ref)

    chosen = functools.reduce(jnp.logical_or, [expert == s for s in top_idx])
    onehot = jnp.where(chosen, 1.0, 0.0)
    r_i = lax.broadcasted_iota(jnp.int32, (tm, tm), 0)
    c_i = lax.broadcasted_iota(jnp.int32, (tm, tm), 1)
    earlier = jnp.where(r_i < c_i, 1.0, 0.0)
    before = jnp.dot(onehot, earlier, preferred_element_type=F32) + seen_ref[...]
    rank_ref[...] = jnp.concatenate([jnp.sum(jnp.where(expert == s, before, 0.0), axis=0, keepdims=True)
                                     for s in top_idx], axis=0).astype(jnp.int32)
    seen_ref[...] = seen_ref[...] + jnp.sum(onehot, axis=1, keepdims=True)
    counts_ref[...] = seen_ref[...].astype(jnp.int32)


def _merge(oa, ob, gates, h, pa, pb, wo, g, b, wr, br):
    n = h.shape[0]
    tm = min(TM_MERGE, n)
    wr_hi = wr.astype(BF16)
    wr_lo = (wr - wr_hi.astype(F32)).astype(BF16)
    pad = lambda a: jnp.pad(a, ((0, 0), (0, LANES - a.shape[1]))).T
    row = lambda w: pl.BlockSpec((tm, w), lambda i: (i, 0))
    full = lambda r, c: pl.BlockSpec((r, c), lambda i: (0, 0))
    slot_row = pl.BlockSpec((TOP_K, tm), lambda i: (0, i))
    return pl.pallas_call(
        _merge_kernel,
        grid=(n // tm,),
        in_specs=[row(WIDTH_A), row(WIDTH_B), row(GATE_COLS), row(D_MODEL),
                  full(WIDTH_A, D_MODEL), full(WIDTH_B, D_MODEL), full(D_MODEL, D_MODEL),
                  full(1, D_MODEL), full(1, D_MODEL), full(LANES, D_MODEL), full(LANES, D_MODEL),
                  full(N_EXPERTS, 1)],
        out_specs=[row(D_MODEL), row(D_MODEL // 2), slot_row, slot_row, slot_row, full(N_EXPERTS, 1)],
        out_shape=[jax.ShapeDtypeStruct((n, D_MODEL), F32), jax.ShapeDtypeStruct((n, D_MODEL // 2), jnp.uint32),
                   jax.ShapeDtypeStruct((TOP_K, n), F32), jax.ShapeDtypeStruct((TOP_K, n), jnp.int32),
                   jax.ShapeDtypeStruct((TOP_K, n), jnp.int32), jax.ShapeDtypeStruct((N_EXPERTS, 1), jnp.int32)],
        scratch_shapes=[pltpu.VMEM((N_EXPERTS, 1), F32)],
        compiler_params=_params(("arbitrary",)),
        name="merge_ln1_router",
    )(oa, ob, gates, h, pa, pb, wo, g, b, pad(jnp.concatenate([wr_hi, wr_lo], axis=1)), pad(wr_hi),
      br.reshape(N_EXPERTS, 1))


def _dispatch_kernel(pend_ref, dest_ref, h1p_ref, xs_hbm, zero_ref, sem, zero_sem):
    @pl.when(pl.program_id(0) == 0)
    def _():
        zero_ref[...] = jnp.zeros_like(zero_ref)

        def fill(row0):
            block = pl.ds(pl.multiple_of(row0, ROWS_MOE), ROWS_MOE)
            return pltpu.make_async_copy(zero_ref, xs_hbm.at[block, :], zero_sem)
        for e in range(N_EXPERTS):
            fill(jnp.maximum(pend_ref[e] - ROWS_MOE, 0)).start()
        for e in range(N_EXPERTS):
            fill(0).wait()

        first_unused = pend_ref[N_EXPERTS - 1] // ROWS_MOE
        n_blocks = xs_hbm.shape[0] // ROWS_MOE

        @pl.loop(first_unused, n_blocks)
        def _(blk):
            fill(blk * ROWS_MOE).start()

        @pl.loop(first_unused, n_blocks)
        def _(blk):
            fill(0).wait()

    def start(group, _):
        t0 = pl.multiple_of(group * SUBLANES, SUBLANES)
        for j in range(SUBLANES):
            for k in range(TOP_K):
                pltpu.make_async_copy(h1p_ref.at[pl.ds(t0 + j, 1), :],
                                      xs_hbm.at[pl.ds(dest_ref[0, 0, (t0 + j) * TOP_K + k], 1), :],
                                      sem).start(priority=k % 2)
        return 0
    lax.fori_loop(0, TM_DISPATCH // SUBLANES, start, 0)

    for k in range(TOP_K):
        pltpu.make_async_copy(h1p_ref, xs_hbm.at[pl.ds(0, TM_DISPATCH), :], sem).wait()


def _dispatch(dest, pend, h1p, n_rows):
    n = h1p.shape[0]
    tm = min(TM_DISPATCH, n)
    assert tm == TM_DISPATCH
    n_steps = n // tm
    grid_spec = pltpu.PrefetchScalarGridSpec(
        num_scalar_prefetch=1,
        grid=(n_steps,),
        in_specs=[
            pl.BlockSpec((1, 1, tm * TOP_K), lambda i, pe: (i, 0, 0), memory_space=pltpu.SMEM),
            pl.BlockSpec((tm, D_MODEL // 2), lambda i, pe: (i, 0)),
        ],
        out_specs=pl.BlockSpec(memory_space=pl.ANY),
        scratch_shapes=[pltpu.VMEM((ROWS_MOE, D_MODEL // 2), jnp.uint32), pltpu.SemaphoreType.DMA(()),
                        pltpu.SemaphoreType.DMA(())],
    )
    return pl.pallas_call(
        _dispatch_kernel,
        grid_spec=grid_spec,
        out_shape=jax.ShapeDtypeStruct((n_rows, D_MODEL // 2), jnp.uint32),
        compiler_params=_params(("arbitrary",)),
        name="dispatch_rows",
    )(pend, dest.reshape(n_steps, 1, tm * TOP_K), h1p)


def _row_gather(src_hbm, idx_ref, dst_ref, sem, n_rows):
    def start(group, _):
        for j in range(SUBLANES):
            pltpu.make_async_copy(src_hbm.at[pl.ds(idx_ref[0, 0, group * SUBLANES + j], 1), :],
                                  dst_ref.at[group, pl.ds(j, 1), :], sem).start(priority=j % 2)
        return 0
    lax.fori_loop(0, n_rows // SUBLANES, start, 0)


def _moe_kernel(bexp_ref, nused_ref, x_ref, wi_ref, bi_ref, wo_ref, bo_ref, y_ref, wi_bf, wo_bf):
    i = pl.program_id(0)
    n_used = nused_ref[0]
    new_expert = jnp.logical_or(i == 0, bexp_ref[i] != bexp_ref[jnp.maximum(i - 1, 0)])

    @pl.when(jnp.logical_and(new_expert, i < n_used))
    def _():
        wi_bf[...] = wi_ref[...].astype(BF16)
        wo_bf[...] = wo_ref[...].astype(BF16)

    @pl.when(i < n_used)
    def _():
        x = _unpack_bf16_pairs(x_ref[...])
        hb = jnp.dot(x, wi_bf[...], preferred_element_type=F32) + bi_ref[...]
        g = jnp.minimum(hb[:, :D_FF], SWIGLU_LIMIT)
        u = jnp.clip(hb[:, D_FF:], -SWIGLU_LIMIT, SWIGLU_LIMIT)
        a = g * jax.nn.sigmoid(SWIGLU_ALPHA * g) * (u + 1.0)
        y_ref[...] = jnp.dot(a.astype(BF16), wo_bf[...], preferred_element_type=F32) + bo_ref[...]

    @pl.when(i >= n_used)
    def _():
        y_ref[...] = jnp.zeros_like(y_ref)


def _moe(block_exp, n_used, x_sorted, wi, bi, wo, bo):
    n_blocks = block_exp.shape[0]
    x_block = lambda i, be, nu: (jnp.maximum(jnp.minimum(i, nu[0] - 1), 0), 0)
    grid_spec = pltpu.PrefetchScalarGridSpec(
        num_scalar_prefetch=2,
        grid=(n_blocks,),
        in_specs=[
            pl.BlockSpec((ROWS_MOE, D_MODEL // 2), x_block),
            pl.BlockSpec((None, D_MODEL, 2 * D_FF), lambda i, be, nu: (be[i], 0, 0)),
            pl.BlockSpec((None, 1, 2 * D_FF), lambda i, be, nu: (be[i], 0, 0)),
            pl.BlockSpec((None, D_FF, D_MODEL), lambda i, be, nu: (be[i], 0, 0)),
            pl.BlockSpec((None, 1, D_MODEL), lambda i, be, nu: (be[i], 0, 0)),
        ],
        out_specs=pl.BlockSpec((ROWS_MOE, D_MODEL), lambda i, be, nu: (i, 0)),
        scratch_shapes=[pltpu.VMEM((D_MODEL, 2 * D_FF), BF16), pltpu.VMEM((D_FF, D_MODEL), BF16)],
    )
    return pl.pallas_call(
        _moe_kernel,
        grid_spec=grid_spec,
        out_shape=jax.ShapeDtypeStruct((n_blocks * ROWS_MOE, D_MODEL), F32),
        compiler_params=_params(("arbitrary",)),
        name="moe_experts",
    )(block_exp, n_used, x_sorted, wi, bi, wo, bo)


def _combine_kernel(dest_cur_ref, dest_nxt_ref, gw_ref, h1_ref, g_ref, b_ref, y_hbm, o_ref, ybuf, sems):
    i = pl.program_id(0)
    n_steps = pl.num_programs(0)
    slot = i % 2
    rows = TOP_K * TM_COMB

    @pl.when(i == 0)
    def _():
        _row_gather(y_hbm, dest_cur_ref, ybuf.at[0], sems.at[0], rows)

    @pl.when(i + 1 < n_steps)
    def _():
        _row_gather(y_hbm, dest_nxt_ref, ybuf.at[1 - slot], sems.at[1 - slot], rows)

    pltpu.make_async_copy(ybuf.at[slot], ybuf.at[slot], sems.at[slot]).wait()
    gw = gw_ref[...]
    f = jnp.zeros((TM_COMB, D_MODEL), F32)
    tiles = TM_COMB // SUBLANES
    for k in range(TOP_K):
        y_k = ybuf[slot, k * tiles:(k + 1) * tiles].reshape(TM_COMB, D_MODEL)
        f = f + gw[:, k:k + 1] * y_k
    o_ref[...] = _layer_norm(DEEPNORM_ALPHA * h1_ref[...] + f, g_ref[...], b_ref[...])


def _combine(dest_km, gate_w, h1, g, b, y_buf):
    n = h1.shape[0]
    n_steps = n // TM_COMB
    rows = TOP_K * TM_COMB
    return pl.pallas_call(
        _combine_kernel,
        grid=(n_steps,),
        in_specs=[
            pl.BlockSpec((1, 1, rows), lambda i: (i, 0, 0), memory_space=pltpu.SMEM),
            pl.BlockSpec((1, 1, rows), lambda i: (jnp.minimum(i + 1, n_steps - 1), 0, 0), memory_space=pltpu.SMEM),
            pl.BlockSpec((TM_COMB, TOP_K), lambda i: (i, 0)),
            pl.BlockSpec((TM_COMB, D_MODEL), lambda i: (i, 0)),
            pl.BlockSpec((1, D_MODEL), lambda i: (0, 0)),
            pl.BlockSpec((1, D_MODEL), lambda i: (0, 0)),
            pl.BlockSpec(memory_space=pl.ANY),
        ],
        out_specs=pl.BlockSpec((TM_COMB, D_MODEL), lambda i: (i, 0)),
        out_shape=jax.ShapeDtypeStruct((n, D_MODEL), F32),
        scratch_shapes=[pltpu.VMEM((2, rows // SUBLANES, SUBLANES, D_MODEL), F32), pltpu.SemaphoreType.DMA((2,))],
        compiler_params=_params(("arbitrary",)),
        name="combine_ln2",
    )(dest_km, dest_km, gate_w, h1, g, b, y_buf)


def _block_layout(top_idx, rank, counts):
    n_slots = top_idx.shape[1] * TOP_K
    padded = (counts + ROWS_MOE - 1) // ROWS_MOE * ROWS_MOE
    pend = jnp.cumsum(padded)
    pstart = pend - padded
    experts = jnp.arange(N_EXPERTS, dtype=jnp.int32)[:, None, None]
    dest = rank + jnp.sum(jnp.where(top_idx[None] == experts, pstart[:, None, None], 0), axis=0)
    n_blocks = n_slots // ROWS_MOE + N_EXPERTS
    block_start = jnp.arange(n_blocks, dtype=jnp.int32) * ROWS_MOE
    block_exp = jnp.minimum(jnp.sum(block_start[:, None] >= pend[None, :], axis=-1), N_EXPERTS - 1).astype(jnp.int32)
    n_used = (pend[-1] // ROWS_MOE).astype(jnp.int32).reshape(1)
    return dest.astype(jnp.int32), pend.astype(jnp.int32), block_exp, n_used


def kernel(x, ln_in_g, ln_in_b, w_in, b_gate, lambda_q1, lambda_k1, lambda_q2, lambda_k2, subln_w, rel_bias,
           w_branch_a, w_branch_b, w_out, ln1_g, ln1_b, w_router, b_router, w_exp_in, b_exp_in, w_exp_out,
           b_exp_out, ln2_g, ln2_b):
    batch, seq, d = x.shape
    n = batch * seq
    row = lambda a: a.reshape(1, -1).astype(F32)
    l = 0
    h, qkv, gates = _ln_proj(x.reshape(n, d), row(ln_in_g), row(ln_in_b), w_in[l].astype(BF16), row(b_gate[l]))
    slopes = jnp.asarray([2.0 ** (-8.0 * (i + 1) / N_HEADS_A) for i in range(N_HEADS_A)], F32)
    out_a = _diff_attention(qkv, slopes, row(lambda_q1[l]), row(lambda_k1[l]), row(lambda_q2[l]),
                            row(lambda_k2[l]), subln_w[l].reshape(-1, 1).astype(F32), batch, seq)
    out_b = _band_attention(qkv, _band_bias(rel_bias[l]), batch, seq)
    h1, h1p, gate_w, top_idx, rank, counts = _merge(
        out_a, out_b, gates, h, w_branch_a[l].astype(BF16), w_branch_b[l].astype(BF16), w_out[l].astype(BF16),
        row(ln1_g[l]), row(ln1_b[l]), w_router[l].astype(F32), row(b_router[l]))
    dest, pend, block_exp, n_used = _block_layout(top_idx, rank, counts[:, 0])
    x_sorted = _dispatch(dest.T, pend, h1p, block_exp.shape[0] * ROWS_MOE)
    y_buf = _moe(block_exp, n_used, x_sorted, w_exp_in[l].astype(F32), b_exp_in[l].reshape(N_EXPERTS, 1, -1),
                 w_exp_out[l].astype(F32), b_exp_out[l].reshape(N_EXPERTS, 1, -1))
    n_steps = n // TM_COMB
    dest_km = dest.reshape(TOP_K, n_steps, TM_COMB).transpose(1, 0, 2).reshape(n_steps, 1, TOP_K * TM_COMB)
    out = _combine(dest_km, gate_w.T, h1, row(ln2_g[l]), row(ln2_b[l]), y_buf)
    return out.reshape(batch, seq, d)
```

```python
import functools
import math

import jax
import jax.numpy as jnp
import numpy as np
from jax import lax
from jax.experimental import pallas as pl
from jax.experimental.pallas import tpu as pltpu

F32 = jnp.float32
BF16 = jnp.bfloat16

D_MODEL = 1024
CHUNK = 64
N_HEADS_A = 4
HEAD_DIM_A = 64
WIDTH_A = 512
N_HEADS_B = 8
HEAD_DIM_B = 64
WIDTH_B = 512
N_PREV_CHUNKS = 8
REL_CLIP = 128
N_EXPERTS = 32
TOP_K = 4
D_FF = 1024
SWIGLU_ALPHA = 1.702
SWIGLU_LIMIT = 7.0
DEEPNORM_ALPHA = 2.0 ** 0.25
LN_EPS = 1e-5
LAM_INIT = 0.8 - 0.6 * math.exp(-0.3 * 0)

LANES = 128
SUBLANES = 8
N_SLABS = (3 * WIDTH_A + 3 * WIDTH_B) // LANES
GATE_COLS = 2 * D_MODEL
IN_COLS = 3 * WIDTH_A + 3 * WIDTH_B + GATE_COLS
LOG2E = math.log2(math.e)
NEG = -1e30

TM_PROJ = 512
PROJ_CHUNK = 512
TQ = 256
TK = 256
KV_GROUP = 2
HEADS_PER_STEP = 4
PAIRS_PER_STEP = 2
BAND = 3 * TK
TM_MERGE = 1024
TM_DISPATCH = 512
ROWS_MOE = 512
TM_COMB = 256
VMEM_LIMIT = 56 * 1024 * 1024


def _layer_norm(x, g, b):
    mu = jnp.mean(x, axis=-1, keepdims=True)
    xc = x - mu
    var = jnp.mean(xc * xc, axis=-1, keepdims=True)
    return xc * lax.rsqrt(var + LN_EPS) * g + b


def _params(sem):
    return pltpu.CompilerParams(dimension_semantics=sem, vmem_limit_bytes=VMEM_LIMIT)


def _ln_proj_kernel(x_ref, g_ref, b_ref, w_ref, bg_ref, h_ref, qkv_ref, gates_ref):
    h = _layer_norm(x_ref[...], g_ref[...], b_ref[...])
    h_ref[...] = h
    hb = h.astype(BF16)
    n_qkv_chunks = (N_SLABS * LANES) // PROJ_CHUNK
    slabs_per_chunk = PROJ_CHUNK // LANES
    q_scale = HEAD_DIM_A ** -0.5 * LOG2E
    for c in range(n_qkv_chunks):
        r = jnp.dot(hb, w_ref[:, c * PROJ_CHUNK:(c + 1) * PROJ_CHUNK], preferred_element_type=F32)
        first = c * slabs_per_chunk
        is_q = (first < WIDTH_A // LANES) or (3 * WIDTH_A // LANES <= first < (3 * WIDTH_A + WIDTH_B) // LANES)
        if is_q:
            r = r * q_scale
        for s in range(slabs_per_chunk):
            qkv_ref[first + s] = r[:, s * LANES:(s + 1) * LANES].astype(BF16)
    g0 = N_SLABS * LANES
    for c in range(GATE_COLS // PROJ_CHUNK):
        r = jnp.dot(hb, w_ref[:, g0 + c * PROJ_CHUNK:g0 + (c + 1) * PROJ_CHUNK], preferred_element_type=F32)
        r = r + bg_ref[:, c * PROJ_CHUNK:(c + 1) * PROJ_CHUNK]
        gates_ref[:, c * PROJ_CHUNK:(c + 1) * PROJ_CHUNK] = jax.nn.sigmoid(r).astype(BF16)


def _ln_proj(x2d, g, b, w_bf16, b_gate):
    n = x2d.shape[0]
    tm = min(TM_PROJ, n)
    const = lambda i: (0, 0)
    return pl.pallas_call(
        _ln_proj_kernel,
        grid=(n // tm,),
        in_specs=[
            pl.BlockSpec((tm, D_MODEL), lambda i: (i, 0)),
            pl.BlockSpec((1, D_MODEL), const),
            pl.BlockSpec((1, D_MODEL), const),
            pl.BlockSpec((D_MODEL, IN_COLS), const, pipeline_mode=pl.Buffered(1)),
            pl.BlockSpec((1, GATE_COLS), const),
        ],
        out_specs=[
            pl.BlockSpec((tm, D_MODEL), lambda i: (i, 0)),
            pl.BlockSpec((N_SLABS, tm, LANES), lambda i: (0, i, 0)),
            pl.BlockSpec((tm, GATE_COLS), lambda i: (i, 0)),
        ],
        out_shape=[
            jax.ShapeDtypeStruct((n, D_MODEL), F32),
            jax.ShapeDtypeStruct((N_SLABS, n, LANES), BF16),
            jax.ShapeDtypeStruct((n, GATE_COLS), BF16),
        ],
        compiler_params=_params(("parallel",)),
        name="ln_proj",
    )(x2d, g, b, w_bf16, b_gate)


def _stack_halves(q):
    lane = lax.broadcasted_iota(jnp.int32, q.shape, 1)
    zero = jnp.zeros_like(q)
    return jnp.concatenate([jnp.where(lane < 64, q, zero), jnp.where(lane >= 64, q, zero)], axis=0)


def _dot_nt(a, b):
    return lax.dot_general(a, b, (((1,), (1,)), ((), ())), preferred_element_type=F32)


def _split3_bf16(x):
    hi = x.astype(BF16)
    r1 = x - hi.astype(F32)
    mid = r1.astype(BF16)
    lo = (r1 - mid.astype(F32)).astype(BF16)
    return hi, mid, lo


class _DiffHead:
    def __init__(self, slope, i, q_ref, k_ref, v_ref, s_refs, mx_refs, m_ref, l_ref, acc_ref):
        self.slope, self.i = slope, i
        self.k_ref, self.v_ref, self.s_refs, self.mx_refs = k_ref, v_ref, s_refs, mx_refs
        self.m_ref, self.l_ref, self.acc_ref = m_ref, l_ref, acc_ref
        lane_q = lax.broadcasted_iota(jnp.int32, (2 * TQ, LANES), 1)
        self.q_aug = jnp.concatenate([_stack_halves(q_ref[...]), jnp.where(lane_q < 3, 1.0, 0.0).astype(BF16)],
                                     axis=1)
        lane_k = lax.broadcasted_iota(jnp.int32, (TK, LANES), 1)
        key_pos = lax.broadcasted_iota(jnp.int32, (TK, LANES), 0)
        self.k_bias = []
        for g in range(KV_GROUP):
            hi, mid, lo = [t.astype(F32) for t in _split3_bf16(slope * (key_pos + g * TK).astype(F32))]
            self.k_bias.append(jnp.where(lane_k == 0, hi, jnp.where(lane_k == 1, mid, jnp.where(lane_k == 2, lo, 0.0))
                                         ).astype(BF16))
        m_ref[...] = jnp.full(m_ref.shape, NEG, F32)
        l_ref[...] = jnp.zeros(l_ref.shape, F32)
        acc_ref[...] = jnp.zeros(acc_ref.shape, F32)

    @staticmethod
    def _group_rows(a):
        return pl.ds(pl.multiple_of(a * (KV_GROUP * TK), KV_GROUP * TK), KV_GROUP * TK)

    def scores_into(self, buf, a):
        mx = None
        for g in range(KV_GROUP):
            rows = pl.ds(pl.multiple_of((a * KV_GROUP + g) * TK, TK), TK)
            s = _dot_nt(jnp.concatenate([self.k_ref[rows, :], self.k_bias[g]], axis=1), self.q_aug)
            self.s_refs[buf][g * TK:(g + 1) * TK, :] = s
            mg = jnp.max(s, axis=0, keepdims=True)
            mx = mg if mx is None else jnp.maximum(mx, mg)
        self.mx_refs[buf][...] = mx

    def _group_offset(self, a):
        return self.slope * ((a * KV_GROUP - self.i) * TK).astype(F32)

    def _update(self, a, s, mx):
        off = self._group_offset(a)
        m_prev = self.m_ref[...]
        m_new = jnp.maximum(m_prev, mx + off)
        alpha = jnp.exp2(m_prev - m_new)
        p = jnp.exp2(s - (m_new - off))
        self.m_ref[...] = m_new
        self.l_ref[...] = alpha * self.l_ref[...] + jnp.sum(p, axis=0, keepdims=True)
        self.acc_ref[...] = alpha * self.acc_ref[...] + lax.dot_general(
            self.v_ref[self._group_rows(a), :], p.astype(BF16), (((0,), (0,)), ((), ())),
            preferred_element_type=F32)

    def full_update(self, buf, a):
        self._update(a, self.s_refs[buf][...], self.mx_refs[buf][...])

    def last_update(self, buf, a):
        kk = lax.broadcasted_iota(jnp.int32, (TK, TQ), 0)
        qq = lax.broadcasted_iota(jnp.int32, (TK, TQ), 1)
        diag = jnp.where(kk // CHUNK <= qq // CHUNK, self.slope * (qq - jnp.abs(qq - kk) - kk).astype(F32), NEG)
        diag = jnp.concatenate([diag, diag], axis=1)
        off = self._group_offset(a)
        scores, offsets = [], []
        for g in range(KV_GROUP):
            jb = a * KV_GROUP + g
            scores.append(self.s_refs[buf][g * TK:(g + 1) * TK, :] + jnp.where(jb == self.i, diag, 0.0))
            offsets.append(jnp.where(jb > self.i, NEG, off))
        m_prev = self.m_ref[...]
        m_new = m_prev
        for s, o in zip(scores, offsets):
            m_new = jnp.maximum(m_new, jnp.max(s, axis=0, keepdims=True) + o)
        alpha = jnp.exp2(m_prev - m_new)
        p = jnp.concatenate([jnp.exp2(s - (m_new - o)) for s, o in zip(scores, offsets)], axis=0)
        self.m_ref[...] = m_new
        self.l_ref[...] = alpha * self.l_ref[...] + jnp.sum(p, axis=0, keepdims=True)
        self.acc_ref[...] = alpha * self.acc_ref[...] + lax.dot_general(
            self.v_ref[self._group_rows(a), :], p.astype(BF16), (((0,), (0,)), ((), ())),
            preferred_element_type=F32)

    def output(self, lam, subln):
        o = self.acc_ref[...] / self.l_ref[...]
        o = o[:, :TQ] - lam * o[:, TQ:]
        o = o * lax.rsqrt(jnp.mean(o * o, axis=0, keepdims=True) + LN_EPS) * subln
        return (o * (1.0 - LAM_INIT)).T.astype(BF16)


def _diff_attn_kernel(slopes_ref, lq1_ref, lk1_ref, lq2_ref, lk2_ref, subln_ref, *refs):
    hp = pl.program_id(1)
    i = pl.program_id(2)
    nh = HEADS_PER_STEP
    q_refs, k_refs, v_refs = refs[:nh], refs[nh:2 * nh], refs[2 * nh:3 * nh]
    o_ref = refs[3 * nh]
    scratch = refs[3 * nh + 1:]
    heads = []
    for t in range(nh):
        s0, s1, mx0, mx1, m, l, acc = scratch[7 * t:7 * t + 7]
        heads.append(_DiffHead(slopes_ref[hp * nh + t] * LOG2E, i, q_refs[t], k_refs[t], v_refs[t],
                               (s0, s1), (mx0, mx1), m, l, acc))

    n_full = i // KV_GROUP
    odd = n_full % 2

    @pl.when(odd == 1)
    def _():
        for h in heads:
            h.scores_into(1, 0)
        for h in heads:
            h.scores_into(0, 1)
        for h in heads:
            h.full_update(1, 0)

    @pl.when(odd == 0)
    def _():
        for h in heads:
            h.scores_into(0, 0)

    def pair(b, _):
        a = odd + 2 * b
        for h in heads:
            h.scores_into(1, a + 1)
        for h in heads:
            h.full_update(0, a)
        for h in heads:
            h.scores_into(0, a + 2)
        for h in heads:
            h.full_update(1, a + 1)
        return 0
    lax.fori_loop(0, (n_full - odd) // 2, pair, 0)

    for h in heads:
        h.last_update(0, n_full)

    lam = (jnp.exp(jnp.sum(lq1_ref[...] * lk1_ref[...], axis=-1, keepdims=True))
           - jnp.exp(jnp.sum(lq2_ref[...] * lk2_ref[...], axis=-1, keepdims=True)) + LAM_INIT)
    for t, h in enumerate(heads):
        o_ref[:, t * LANES:(t + 1) * LANES] = h.output(lam, subln_ref[...])


def _diff_attention(qkv, slopes, lq1, lk1, lq2, lk2, subln, batch, seq):
    nq = seq // TQ
    assert seq % (TQ * KV_GROUP) == 0, "the last key group of a query block must stay inside the sequence"
    n = batch * seq
    nh = HEADS_PER_STEP
    vec = lambda w: pl.BlockSpec((1, w), lambda b, h, i: (0, 0))
    q_spec = lambda t: pl.BlockSpec((None, TQ, LANES), lambda b, h, i: (h * nh + t, b * nq + i, 0))
    kv_spec = lambda t, base: pl.BlockSpec((None, seq, LANES), lambda b, h, i: (base + h * nh + t, b, 0))
    stat = pltpu.VMEM((1, 2 * TQ), F32)
    head_scratch = [pltpu.VMEM((KV_GROUP * TK, 2 * TQ), F32), pltpu.VMEM((KV_GROUP * TK, 2 * TQ), F32),
                    stat, stat, stat, stat, pltpu.VMEM((LANES, 2 * TQ), F32)]
    return pl.pallas_call(
        _diff_attn_kernel,
        grid=(batch, N_HEADS_A // nh, nq),
        in_specs=[
            pl.BlockSpec(memory_space=pltpu.SMEM),
            vec(HEAD_DIM_A), vec(HEAD_DIM_A), vec(HEAD_DIM_A), vec(HEAD_DIM_A),
            pl.BlockSpec((2 * HEAD_DIM_A, 1), lambda b, h, i: (0, 0)),
            *[q_spec(t) for t in range(nh)],
            *[kv_spec(t, N_HEADS_A) for t in range(nh)],
            *[kv_spec(t, 2 * N_HEADS_A) for t in range(nh)],
        ],
        out_specs=pl.BlockSpec((TQ, nh * LANES), lambda b, h, i: (b * nq + i, h)),
        out_shape=jax.ShapeDtypeStruct((n, WIDTH_A), BF16),
        scratch_shapes=head_scratch * nh,
        compiler_params=_params(("parallel", "parallel", "arbitrary")),
        name="diff_attn",
    )(slopes, lq1, lk1, lq2, lk2, subln, *([qkv] * (3 * nh)))


def _band_attn_kernel(*refs):
    npair = PAIRS_PER_STEP
    bias_refs, q_refs = refs[:npair], refs[npair:2 * npair]
    k_refs, v_refs = refs[2 * npair:3 * npair], refs[3 * npair:4 * npair]
    o_ref = refs[4 * npair]
    s_refs = refs[4 * npair + 1:]
    n_sub = BAND // TK
    nq = o_ref.shape[0] // TQ
    feat = lax.broadcasted_iota(jnp.int32, (LANES, TQ), 0)

    def key_rows(i, jj):
        start = i * TQ - N_PREV_CHUNKS * CHUNK + jj * TK
        return start, pl.ds(pl.multiple_of(jnp.maximum(start, 0), TK), TK)

    def scores_into(buf, i):
        for t in range(npair):
            q2 = _stack_halves(q_refs[t][pl.ds(pl.multiple_of(i * TQ, TQ), TQ), :])
            for jj in range(n_sub):
                _, rows = key_rows(i, jj)
                s_refs[2 * t + buf][jj * TK:(jj + 1) * TK, :] = (
                    _dot_nt(k_refs[t][rows, :], q2) + bias_refs[t][jj * TK:(jj + 1) * TK, :])

    def softmax_out(buf, i):
        starts_rows = [key_rows(i, jj) for jj in range(n_sub)]
        offsets = [jnp.where(start < 0, NEG, 0.0).astype(F32) for start, _ in starts_rows]
        for t in range(npair):
            scores = [s_refs[2 * t + buf][jj * TK:(jj + 1) * TK, :] for jj in range(n_sub)]
            m = functools.reduce(jnp.maximum, [jnp.max(s, axis=0, keepdims=True) + off
                                               for s, off in zip(scores, offsets)])
            l = jnp.zeros((1, 2 * TQ), F32)
            acc = jnp.zeros((LANES, 2 * TQ), F32)
            for jj in range(n_sub):
                p = jnp.exp2(scores[jj] - (m - offsets[jj]))
                l = l + jnp.sum(p, axis=0, keepdims=True)
                acc = acc + lax.dot_general(v_refs[t][starts_rows[jj][1], :], p.astype(BF16),
                                            (((0,), (0,)), ((), ())), preferred_element_type=F32)
            o = acc / l
            o_ref[pl.ds(pl.multiple_of(i * TQ, TQ), TQ), t * LANES:(t + 1) * LANES] = (
                jnp.where(feat < 64, o[:, :TQ], o[:, TQ:]).T.astype(BF16))

    scores_into(0, 0)

    def two_blocks(b, _):
        i = 2 * b
        scores_into(1, i + 1)
        softmax_out(0, i)
        scores_into(0, jnp.minimum(i + 2, nq - 1))
        softmax_out(1, i + 1)
        return 0
    lax.fori_loop(0, nq // 2, two_blocks, 0)


BIAS_SPAN = 1024


def _band_bias_kernel(line_ref, o_ref):
    kj = lax.broadcasted_iota(jnp.int32, (BAND, TQ), 0)
    qi = lax.broadcasted_iota(jnp.int32, (BAND, TQ), 1)
    kc = kj // CHUNK - N_PREV_CHUNKS
    qc = qi // CHUNK
    allowed = jnp.logical_and(kc <= qc, kc >= qc - N_PREV_CHUNKS)
    for hh in range(2):
        line = jnp.broadcast_to(line_ref[hh:hh + 1, :], (BAND, BIAS_SPAN))
        rolled = pltpu.roll(line, BIAS_SPAN - (BAND - 1), 1, stride=1, stride_axis=0)
        o_ref[:, hh * TQ:(hh + 1) * TQ] = jnp.where(allowed, LOG2E * rolled[:, :TQ], NEG)


def _band_bias(rel_bias):
    rb = rel_bias.astype(F32)
    n_low = (BAND - 1) - REL_CLIP
    n_high = (BAND + TQ - 1) - (BAND - 1) - REL_CLIP - 1
    line = jnp.concatenate([jnp.broadcast_to(rb[:, :1], (N_HEADS_B, n_low)), rb,
                            jnp.broadcast_to(rb[:, -1:], (N_HEADS_B, n_high))], axis=1)
    line = jnp.pad(line[:, ::-1], ((0, 0), (0, BIAS_SPAN - line.shape[1])))
    pairs = N_HEADS_B // 2
    return pl.pallas_call(
        _band_bias_kernel,
        grid=(pairs,),
        in_specs=[pl.BlockSpec((None, 2, BIAS_SPAN), lambda p: (p, 0, 0))],
        out_specs=pl.BlockSpec((None, BAND, 2 * TQ), lambda p: (p, 0, 0)),
        out_shape=jax.ShapeDtypeStruct((pairs, BAND, 2 * TQ), F32),
        compiler_params=_params(("parallel",)),
        name="band_bias",
    )(line.reshape(pairs, 2, BIAS_SPAN))


def _band_attention(qkv, bias, batch, seq):
    assert (seq // TQ) % 2 == 0, "query blocks are walked two at a time"
    n = batch * seq
    base = 3 * WIDTH_A // LANES
    pairs = N_HEADS_B // 2
    npair = PAIRS_PER_STEP
    bias_spec = lambda t: pl.BlockSpec((None, BAND, 2 * TQ), lambda b, p: (p * npair + t, 0, 0))
    slab_spec = lambda t, off: pl.BlockSpec((None, seq, LANES), lambda b, p: (base + off + p * npair + t, b, 0))
    return pl.pallas_call(
        _band_attn_kernel,
        grid=(batch, pairs // npair),
        in_specs=[
            *[bias_spec(t) for t in range(npair)],
            *[slab_spec(t, 0) for t in range(npair)],
            *[slab_spec(t, pairs) for t in range(npair)],
            *[slab_spec(t, 2 * pairs) for t in range(npair)],
        ],
        out_specs=pl.BlockSpec((seq, npair * LANES), lambda b, p: (b, p)),
        out_shape=jax.ShapeDtypeStruct((n, WIDTH_B), BF16),
        scratch_shapes=[pltpu.VMEM((BAND, 2 * TQ), F32)] * (2 * npair),
        compiler_params=_params(("parallel", "parallel")),
        name="band_attn",
    )(*([bias] * npair), *([qkv] * (3 * npair)))


def _pack_bf16_pairs(x):
    w = x.shape[1] // 2
    bits = pltpu.bitcast(x.astype(BF16).astype(F32), jnp.uint32)
    return (bits[:, :w] >> 16) | (bits[:, w:] & jnp.uint32(0xFFFF0000))


def _unpack_bf16_pairs(p):
    lo = pltpu.bitcast(p << 16, F32)
    hi = pltpu.bitcast(p & jnp.uint32(0xFFFF0000), F32)
    return jnp.concatenate([lo, hi], axis=1).astype(BF16)


def _merge_kernel(oa_ref, ob_ref, gates_ref, h_ref, pa_ref, pb_ref, wo_ref, g_ref, b_ref, wra_ref, wrb_ref, br_ref,
                  h1_ref, h1p_ref, gw_ref, idx_ref, rank_ref, counts_ref, seen_ref):
    ma = jnp.dot(oa_ref[...], pa_ref[...], preferred_element_type=F32)
    mb = jnp.dot(ob_ref[...], pb_ref[...], preferred_element_type=F32)
    merged = gates_ref[:, :D_MODEL].astype(F32) * ma + gates_ref[:, D_MODEL:].astype(F32) * mb
    m = jnp.dot(merged.astype(BF16), wo_ref[...], preferred_element_type=F32)
    h1 = _layer_norm(DEEPNORM_ALPHA * h_ref[...] + m, g_ref[...], b_ref[...])
    h1_ref[...] = h1
    h1p_ref[...] = _pack_bf16_pairs(h1)

    h_hi = h1.astype(BF16)
    h_lo = (h1 - h_hi.astype(F32)).astype(BF16)
    part_a = _dot_nt(wra_ref[...], h_hi)
    part_b = _dot_nt(wrb_ref[...], h_lo)
    logits = part_a[:N_EXPERTS] + part_a[N_EXPERTS:2 * N_EXPERTS] + part_b[:N_EXPERTS] + br_ref[...]

    tm = logits.shape[1]
    expert = lax.broadcasted_iota(jnp.int32, (N_EXPERTS, tm), 0)
    vals = logits
    top_val, top_idx = [], []
    for _ in range(TOP_K):
        mx = jnp.max(vals, axis=0, keepdims=True)
        sel = jnp.min(jnp.where(vals == mx, expert, N_EXPERTS), axis=0, keepdims=True)
        top_val.append(mx)
        top_idx.append(sel)
        vals = jnp.where(expert == sel, -jnp.inf, vals)
    ex = [jnp.exp(v - top_val[0]) for v in top_val]
    denom = functools.reduce(jnp.add, ex)
    gw_ref[...] = jnp.concatenate([e / denom for e in ex], axis=0)
    idx_ref[...] = jnp.concatenate(top_idx, axis=0)

    @pl.when(pl.program_id(0) == 0)
    def _():
        seen_ref[...] = jnp.zeros_like(seen_ref)

    chosen = functools.reduce(jnp.logical_or, [expert == s for s in top_idx])
    onehot = jnp.where(chosen, 1.0, 0.0)
    r_i = lax.broadcasted_iota(jnp.int32, (tm, tm), 0)
    c_i = lax.broadcasted_iota(jnp.int32, (tm, tm), 1)
    earlier = jnp.where(r_i < c_i, 1.0, 0.0)
    before = jnp.dot(onehot, earlier, preferred_element_type=F32) + seen_ref[...]
    rank_ref[...] = jnp.concatenate([jnp.sum(jnp.where(expert == s, before, 0.0), axis=0, keepdims=True)
                                     for s in top_idx], axis=0).astype(jnp.int32)
    seen_ref[...] = seen_ref[...] + jnp.sum(onehot, axis=1, keepdims=True)
    counts_ref[...] = seen_ref[...].astype(jnp.int32)


def _merge(oa, ob, gates, h, pa, pb, wo, g, b, wr, br):
    n = h.shape[0]
    tm = min(TM_MERGE, n)
    wr_hi = wr.astype(BF16)
    wr_lo = (wr - wr_hi.astype(F32)).astype(BF16)
    pad = lambda a: jnp.pad(a, ((0, 0), (0, LANES - a.shape[1]))).T
    row = lambda w: pl.BlockSpec((tm, w), lambda i: (i, 0))
    full = lambda r, c: pl.BlockSpec((r, c), lambda i: (0, 0))
    slot_row = pl.BlockSpec((TOP_K, tm), lambda i: (0, i))
    return pl.pallas_call(
        _merge_kernel,
        grid=(n // tm,),
        in_specs=[row(WIDTH_A), row(WIDTH_B), row(GATE_COLS), row(D_MODEL),
                  full(WIDTH_A, D_MODEL), full(WIDTH_B, D_MODEL), full(D_MODEL, D_MODEL),
                  full(1, D_MODEL), full(1, D_MODEL), full(LANES, D_MODEL), full(LANES, D_MODEL),
                  full(N_EXPERTS, 1)],
        out_specs=[row(D_MODEL), row(D_MODEL // 2), slot_row, slot_row, slot_row, full(N_EXPERTS, 1)],
        out_shape=[jax.ShapeDtypeStruct((n, D_MODEL), F32), jax.ShapeDtypeStruct((n, D_MODEL // 2), jnp.uint32),
                   jax.ShapeDtypeStruct((TOP_K, n), F32), jax.ShapeDtypeStruct((TOP_K, n), jnp.int32),
                   jax.ShapeDtypeStruct((TOP_K, n), jnp.int32), jax.ShapeDtypeStruct((N_EXPERTS, 1), jnp.int32)],
        scratch_shapes=[pltpu.VMEM((N_EXPERTS, 1), F32)],
        compiler_params=_params(("arbitrary",)),
        name="merge_ln1_router",
    )(oa, ob, gates, h, pa, pb, wo, g, b, pad(jnp.concatenate([wr_hi, wr_lo], axis=1)), pad(wr_hi),
      br.reshape(N_EXPERTS, 1))


def _dispatch_kernel(pend_ref, dest_ref, h1p_ref, xs_hbm, zero_ref, sem, zero_sem):
    @pl.when(pl.program_id(0) == 0)
    def _():
        zero_ref[...] = jnp.zeros_like(zero_ref)

        def fill(row0):
            block = pl.ds(pl.multiple_of(row0, ROWS_MOE), ROWS_MOE)
            return pltpu.make_async_copy(zero_ref, xs_hbm.at[block, :], zero_sem)
        for e in range(N_EXPERTS):
            fill(jnp.maximum(pend_ref[e] - ROWS_MOE, 0)).start()
        for e in range(N_EXPERTS):
            fill(0).wait()

        first_unused = pend_ref[N_EXPERTS - 1] // ROWS_MOE
        n_blocks = xs_hbm.shape[0] // ROWS_MOE

        @pl.loop(first_unused, n_blocks)
        def _(blk):
            fill(blk * ROWS_MOE).start()

        @pl.loop(first_unused, n_blocks)
        def _(blk):
            fill(0).wait()

    def start(group, _):
        for j in range(SUBLANES):
            for k in range(TOP_K):
                pltpu.make_async_copy(h1p_ref.at[group, pl.ds(j, 1), :],
                                      xs_hbm.at[pl.ds(dest_ref[0, 0, (group * SUBLANES + j) * TOP_K + k], 1), :],
                                      sem).start(priority=k % 2)
        return 0
    lax.fori_loop(0, TM_DISPATCH // SUBLANES, start, 0)

    for k in range(TOP_K):
        pltpu.make_async_copy(h1p_ref, h1p_ref, sem).wait()


def _dispatch(dest, pend, h1p, n_rows):
    n = h1p.shape[0]
    tm = min(TM_DISPATCH, n)
    assert tm == TM_DISPATCH
    n_steps = n // tm
    grid_spec = pltpu.PrefetchScalarGridSpec(
        num_scalar_prefetch=1,
        grid=(n_steps,),
        in_specs=[
            pl.BlockSpec((1, 1, tm * TOP_K), lambda i, pe: (i, 0, 0), memory_space=pltpu.SMEM),
            pl.BlockSpec((tm // SUBLANES, SUBLANES, D_MODEL // 2), lambda i, pe: (i, 0, 0)),
        ],
        out_specs=pl.BlockSpec(memory_space=pl.ANY),
        scratch_shapes=[pltpu.VMEM((ROWS_MOE, D_MODEL // 2), jnp.uint32), pltpu.SemaphoreType.DMA(()),
                        pltpu.SemaphoreType.DMA(())],
    )
    return pl.pallas_call(
        _dispatch_kernel,
        grid_spec=grid_spec,
        out_shape=jax.ShapeDtypeStruct((n_rows, D_MODEL // 2), jnp.uint32),
        compiler_params=_params(("arbitrary",)),
        name="dispatch_rows",
    )(pend, dest.reshape(n_steps, 1, tm * TOP_K), h1p.reshape(n // SUBLANES, SUBLANES, D_MODEL // 2))


def _row_gather(src_hbm, idx_ref, dst_ref, sem, n_rows):
    def start(group, _):
        for j in range(SUBLANES):
            pltpu.make_async_copy(src_hbm.at[pl.ds(idx_ref[0, 0, group * SUBLANES + j], 1), :],
                                  dst_ref.at[group, pl.ds(j, 1), :], sem).start(priority=j % 2)
        return 0
    lax.fori_loop(0, n_rows // SUBLANES, start, 0)


def _moe_kernel(bexp_ref, nused_ref, nvalid_ref, x_ref, wi_ref, bi_ref, wo_ref, bo_ref, y_ref, wi_bf, wo_bf):
    i = pl.program_id(0)
    n_used = nused_ref[0]
    n_valid = nvalid_ref[i]
    new_expert = jnp.logical_or(i == 0, bexp_ref[i] != bexp_ref[jnp.maximum(i - 1, 0)])
    half = ROWS_MOE // 2

    @pl.when(jnp.logical_and(new_expert, i < n_used))
    def _():
        wi_bf[...] = wi_ref[...].astype(BF16)
        wo_bf[...] = wo_ref[...].astype(BF16)

    def ffn(rows):
        x = _unpack_bf16_pairs(x_ref[rows, :])
        hb = jnp.dot(x, wi_bf[...], preferred_element_type=F32) + bi_ref[...]
        g = jnp.minimum(hb[:, :D_FF], SWIGLU_LIMIT)
        u = jnp.clip(hb[:, D_FF:], -SWIGLU_LIMIT, SWIGLU_LIMIT)
        a = g * jax.nn.sigmoid(SWIGLU_ALPHA * g) * (u + 1.0)
        y_ref[rows, :] = jnp.dot(a.astype(BF16), wo_bf[...], preferred_element_type=F32) + bo_ref[...]

    @pl.when(jnp.logical_and(i < n_used, n_valid > half))
    def _():
        ffn(slice(0, ROWS_MOE))

    @pl.when(jnp.logical_and(i < n_used, n_valid <= half))
    def _():
        ffn(slice(0, half))
        y_ref[half:, :] = jnp.zeros((ROWS_MOE - half, D_MODEL), F32)

    @pl.when(i >= n_used)
    def _():
        y_ref[...] = jnp.zeros_like(y_ref)


def _moe(block_exp, n_used, n_valid, x_sorted, wi, bi, wo, bo):
    n_blocks = block_exp.shape[0]
    x_block = lambda i, be, nu, nv: (jnp.maximum(jnp.minimum(i, nu[0] - 1), 0), 0)
    expert = lambda i, be, nu, nv: (be[i], 0, 0)
    grid_spec = pltpu.PrefetchScalarGridSpec(
        num_scalar_prefetch=3,
        grid=(n_blocks,),
        in_specs=[
            pl.BlockSpec((ROWS_MOE, D_MODEL // 2), x_block),
            pl.BlockSpec((None, D_MODEL, 2 * D_FF), expert),
            pl.BlockSpec((None, 1, 2 * D_FF), expert),
            pl.BlockSpec((None, D_FF, D_MODEL), expert),
            pl.BlockSpec((None, 1, D_MODEL), expert),
        ],
        out_specs=pl.BlockSpec((ROWS_MOE, D_MODEL), lambda i, be, nu, nv: (i, 0)),
        scratch_shapes=[pltpu.VMEM((D_MODEL, 2 * D_FF), BF16), pltpu.VMEM((D_FF, D_MODEL), BF16)],
    )
    return pl.pallas_call(
        _moe_kernel,
        grid_spec=grid_spec,
        out_shape=jax.ShapeDtypeStruct((n_blocks * ROWS_MOE, D_MODEL), F32),
        compiler_params=_params(("arbitrary",)),
        name="moe_experts",
    )(block_exp, n_used, n_valid, x_sorted, wi, bi, wo, bo)


def _combine_kernel(dest_cur_ref, dest_nxt_ref, gw_ref, h1_ref, g_ref, b_ref, y_hbm, o_ref, ybuf, sems):
    i = pl.program_id(0)
    n_steps = pl.num_programs(0)
    slot = i % 2
    rows = TOP_K * TM_COMB

    @pl.when(i == 0)
    def _():
        _row_gather(y_hbm, dest_cur_ref, ybuf.at[0], sems.at[0], rows)

    @pl.when(i + 1 < n_steps)
    def _():
        _row_gather(y_hbm, dest_nxt_ref, ybuf.at[1 - slot], sems.at[1 - slot], rows)

    pltpu.make_async_copy(ybuf.at[slot], ybuf.at[slot], sems.at[slot]).wait()
    gw = gw_ref[...]
    f = jnp.zeros((TM_COMB, D_MODEL), F32)
    tiles = TM_COMB // SUBLANES
    for k in range(TOP_K):
        y_k = ybuf[slot, k * tiles:(k + 1) * tiles].reshape(TM_COMB, D_MODEL)
        f = f + gw[:, k:k + 1] * y_k
    o_ref[...] = _layer_norm(DEEPNORM_ALPHA * h1_ref[...] + f, g_ref[...], b_ref[...])


def _combine(dest_km, gate_w, h1, g, b, y_buf):
    n = h1.shape[0]
    n_steps = n // TM_COMB
    rows = TOP_K * TM_COMB
    return pl.pallas_call(
        _combine_kernel,
        grid=(n_steps,),
        in_specs=[
            pl.BlockSpec((1, 1, rows), lambda i: (i, 0, 0), memory_space=pltpu.SMEM),
            pl.BlockSpec((1, 1, rows), lambda i: (jnp.minimum(i + 1, n_steps - 1), 0, 0), memory_space=pltpu.SMEM),
            pl.BlockSpec((TM_COMB, TOP_K), lambda i: (i, 0)),
            pl.BlockSpec((TM_COMB, D_MODEL), lambda i: (i, 0)),
            pl.BlockSpec((1, D_MODEL), lambda i: (0, 0)),
            pl.BlockSpec((1, D_MODEL), lambda i: (0, 0)),
            pl.BlockSpec(memory_space=pl.ANY),
        ],
        out_specs=pl.BlockSpec((TM_COMB, D_MODEL), lambda i: (i, 0)),
        out_shape=jax.ShapeDtypeStruct((n, D_MODEL), F32),
        scratch_shapes=[pltpu.VMEM((2, rows // SUBLANES, SUBLANES, D_MODEL), F32), pltpu.SemaphoreType.DMA((2,))],
        compiler_params=_params(("arbitrary",)),
        name="combine_ln2",
    )(dest_km, dest_km, gate_w, h1, g, b, y_buf)


def _block_layout(top_idx, rank, counts):
    n_slots = top_idx.shape[1] * TOP_K
    padded = (counts + ROWS_MOE - 1) // ROWS_MOE * ROWS_MOE
    pend = jnp.cumsum(padded)
    pstart = pend - padded
    experts = jnp.arange(N_EXPERTS, dtype=jnp.int32)[:, None, None]
    dest = rank + jnp.sum(jnp.where(top_idx[None] == experts, pstart[:, None, None], 0), axis=0)
    n_blocks = n_slots // ROWS_MOE + N_EXPERTS
    block_start = jnp.arange(n_blocks, dtype=jnp.int32) * ROWS_MOE
    block_exp = jnp.minimum(jnp.sum(block_start[:, None] >= pend[None, :], axis=-1), N_EXPERTS - 1).astype(jnp.int32)
    n_used = (pend[-1] // ROWS_MOE).astype(jnp.int32).reshape(1)
    onehot_exp = block_exp[:, None] == jnp.arange(N_EXPERTS, dtype=jnp.int32)[None, :]
    seg_end = jnp.sum(jnp.where(onehot_exp, (pstart + counts)[None, :], 0), axis=-1)
    n_valid = jnp.clip(seg_end - block_start, 0, ROWS_MOE).astype(jnp.int32)
    return dest.astype(jnp.int32), pend.astype(jnp.int32), block_exp, n_used, n_valid


def kernel(x, ln_in_g, ln_in_b, w_in, b_gate, lambda_q1, lambda_k1, lambda_q2, lambda_k2, subln_w, rel_bias,
           w_branch_a, w_branch_b, w_out, ln1_g, ln1_b, w_router, b_router, w_exp_in, b_exp_in, w_exp_out,
           b_exp_out, ln2_g, ln2_b):
    batch, seq, d = x.shape
    n = batch * seq
    row = lambda a: a.reshape(1, -1).astype(F32)
    l = 0
    h, qkv, gates = _ln_proj(x.reshape(n, d), row(ln_in_g), row(ln_in_b), w_in[l].astype(BF16), row(b_gate[l]))
    slopes = jnp.asarray([2.0 ** (-8.0 * (i + 1) / N_HEADS_A) for i in range(N_HEADS_A)], F32)
    out_a = _diff_attention(qkv, slopes, row(lambda_q1[l]), row(lambda_k1[l]), row(lambda_q2[l]),
                            row(lambda_k2[l]), subln_w[l].reshape(-1, 1).astype(F32), batch, seq)
    out_b = _band_attention(qkv, _band_bias(rel_bias[l]), batch, seq)
    h1, h1p, gate_w, top_idx, rank, counts = _merge(
        out_a, out_b, gates, h, w_branch_a[l].astype(BF16), w_branch_b[l].astype(BF16), w_out[l].astype(BF16),
        row(ln1_g[l]), row(ln1_b[l]), w_router[l].astype(F32), row(b_router[l]))
    dest, pend, block_exp, n_used, n_valid = _block_layout(top_idx, rank, counts[:, 0])
    x_sorted = _dispatch(dest.T, pend, h1p, block_exp.shape[0] * ROWS_MOE)
    y_buf = _moe(block_exp, n_used, n_valid, x_sorted, w_exp_in[l].astype(F32), b_exp_in[l].reshape(N_EXPERTS, 1, -1),
                 w_exp_out[l].astype(F32), b_exp_out[l].reshape(N_EXPERTS, 1, -1))
    n_steps = n // TM_COMB
    dest_km = dest.reshape(TOP_K, n_steps, TM_COMB).transpose(1, 0, 2).reshape(n_steps, 1, TOP_K * TM_COMB)
    out = _combine(dest_km, gate_w.T, h1, row(ln2_g[l]), row(ln2_b[l]), y_buf)
    return out.reshape(batch, seq, d)
```

```python
import functools
import math

import jax
import jax.numpy as jnp
import numpy as np
from jax import lax
from jax.experimental import pallas as pl
from jax.experimental.pallas import tpu as pltpu

F32 = jnp.float32
BF16 = jnp.bfloat16

D_MODEL = 1024
CHUNK = 64
N_HEADS_A = 4
HEAD_DIM_A = 64
WIDTH_A = 512
N_HEADS_B = 8
HEAD_DIM_B = 64
WIDTH_B = 512
N_PREV_CHUNKS = 8
REL_CLIP = 128
N_EXPERTS = 32
TOP_K = 4
D_FF = 1024
SWIGLU_ALPHA = 1.702
SWIGLU_LIMIT = 7.0
DEEPNORM_ALPHA = 2.0 ** 0.25
LN_EPS = 1e-5
LAM_INIT = 0.8 - 0.6 * math.exp(-0.3 * 0)

LANES = 128
SUBLANES = 8
N_SLABS = (3 * WIDTH_A + 3 * WIDTH_B) // LANES
GATE_COLS = 2 * D_MODEL
IN_COLS = 3 * WIDTH_A + 3 * WIDTH_B + GATE_COLS
LOG2E = math.log2(math.e)
NEG = -1e30

TM_PROJ = 512
PROJ_CHUNK = 512
TQ = 256
TK = 256
KV_GROUP = 2
HEADS_PER_STEP = 4
PAIRS_PER_STEP = 2
BAND = 3 * TK
TM_MERGE = 1024
TM_DISPATCH = 512
ROWS_MOE = 512
TM_COMB = 256
VMEM_LIMIT = 56 * 1024 * 1024


def _layer_norm(x, g, b):
    mu = jnp.mean(x, axis=-1, keepdims=True)
    xc = x - mu
    var = jnp.mean(xc * xc, axis=-1, keepdims=True)
    return xc * lax.rsqrt(var + LN_EPS) * g + b


def _params(sem):
    return pltpu.CompilerParams(dimension_semantics=sem, vmem_limit_bytes=VMEM_LIMIT)


def _ln_proj_kernel(x_ref, g_ref, b_ref, w_ref, bg_ref, h_ref, qkv_ref, gates_ref):
    h = _layer_norm(x_ref[...], g_ref[...], b_ref[...])
    h_ref[...] = h
    hb = h.astype(BF16)
    n_qkv_chunks = (N_SLABS * LANES) // PROJ_CHUNK
    slabs_per_chunk = PROJ_CHUNK // LANES
    q_scale = HEAD_DIM_A ** -0.5 * LOG2E
    for c in range(n_qkv_chunks):
        r = jnp.dot(hb, w_ref[:, c * PROJ_CHUNK:(c + 1) * PROJ_CHUNK], preferred_element_type=F32)
        first = c * slabs_per_chunk
        is_q = (first < WIDTH_A // LANES) or (3 * WIDTH_A // LANES <= first < (3 * WIDTH_A + WIDTH_B) // LANES)
        if is_q:
            r = r * q_scale
        for s in range(slabs_per_chunk):
            qkv_ref[first + s] = r[:, s * LANES:(s + 1) * LANES].astype(BF16)
    g0 = N_SLABS * LANES
    for c in range(GATE_COLS // PROJ_CHUNK):
        r = jnp.dot(hb, w_ref[:, g0 + c * PROJ_CHUNK:g0 + (c + 1) * PROJ_CHUNK], preferred_element_type=F32)
        r = r + bg_ref[:, c * PROJ_CHUNK:(c + 1) * PROJ_CHUNK]
        gates_ref[:, c * PROJ_CHUNK:(c + 1) * PROJ_CHUNK] = jax.nn.sigmoid(r).astype(BF16)


def _ln_proj(x2d, g, b, w_bf16, b_gate):
    n = x2d.shape[0]
    tm = min(TM_PROJ, n)
    const = lambda i: (0, 0)
    return pl.pallas_call(
        _ln_proj_kernel,
        grid=(n // tm,),
        in_specs=[
            pl.BlockSpec((tm, D_MODEL), lambda i: (i, 0)),
            pl.BlockSpec((1, D_MODEL), const),
            pl.BlockSpec((1, D_MODEL), const),
            pl.BlockSpec((D_MODEL, IN_COLS), const, pipeline_mode=pl.Buffered(1)),
            pl.BlockSpec((1, GATE_COLS), const),
        ],
        out_specs=[
            pl.BlockSpec((tm, D_MODEL), lambda i: (i, 0)),
            pl.BlockSpec((N_SLABS, tm, LANES), lambda i: (0, i, 0)),
            pl.BlockSpec((tm, GATE_COLS), lambda i: (i, 0)),
        ],
        out_shape=[
            jax.ShapeDtypeStruct((n, D_MODEL), F32),
            jax.ShapeDtypeStruct((N_SLABS, n, LANES), BF16),
            jax.ShapeDtypeStruct((n, GATE_COLS), BF16),
        ],
        compiler_params=_params(("parallel",)),
        name="ln_proj",
    )(x2d, g, b, w_bf16, b_gate)


def _stack_halves(q):
    lane = lax.broadcasted_iota(jnp.int32, q.shape, 1)
    zero = jnp.zeros_like(q)
    return jnp.concatenate([jnp.where(lane < 64, q, zero), jnp.where(lane >= 64, q, zero)], axis=0)


def _dot_nt(a, b):
    return lax.dot_general(a, b, (((1,), (1,)), ((), ())), preferred_element_type=F32)


def _split3_bf16(x):
    hi = x.astype(BF16)
    r1 = x - hi.astype(F32)
    mid = r1.astype(BF16)
    lo = (r1 - mid.astype(F32)).astype(BF16)
    return hi, mid, lo


class _DiffHead:
    def __init__(self, slope, i, q_ref, k_ref, v_ref, s_refs, mx_refs, m_ref, l_ref, acc_ref):
        self.slope, self.i = slope, i
        self.k_ref, self.v_ref, self.s_refs, self.mx_refs = k_ref, v_ref, s_refs, mx_refs
        self.m_ref, self.l_ref, self.acc_ref = m_ref, l_ref, acc_ref
        lane_q = lax.broadcasted_iota(jnp.int32, (2 * TQ, LANES), 1)
        self.q_aug = jnp.concatenate([_stack_halves(q_ref[...]), jnp.where(lane_q < 3, 1.0, 0.0).astype(BF16)],
                                     axis=1)
        lane_k = lax.broadcasted_iota(jnp.int32, (TK, LANES), 1)
        key_pos = lax.broadcasted_iota(jnp.int32, (TK, LANES), 0)
        self.k_bias = []
        for g in range(KV_GROUP):
            hi, mid, lo = [t.astype(F32) for t in _split3_bf16(slope * (key_pos + g * TK).astype(F32))]
            self.k_bias.append(jnp.where(lane_k == 0, hi, jnp.where(lane_k == 1, mid, jnp.where(lane_k == 2, lo, 0.0))
                                         ).astype(BF16))
        m_ref[...] = jnp.full(m_ref.shape, NEG, F32)
        l_ref[...] = jnp.zeros(l_ref.shape, F32)
        acc_ref[...] = jnp.zeros(acc_ref.shape, F32)

    @staticmethod
    def _group_rows(a):
        return pl.ds(pl.multiple_of(a * (KV_GROUP * TK), KV_GROUP * TK), KV_GROUP * TK)

    def scores_into(self, buf, a):
        mx = None
        for g in range(KV_GROUP):
            rows = pl.ds(pl.multiple_of((a * KV_GROUP + g) * TK, TK), TK)
            s = _dot_nt(jnp.concatenate([self.k_ref[rows, :], self.k_bias[g]], axis=1), self.q_aug)
            self.s_refs[buf][g * TK:(g + 1) * TK, :] = s
            mg = jnp.max(s, axis=0, keepdims=True)
            mx = mg if mx is None else jnp.maximum(mx, mg)
        self.mx_refs[buf][...] = mx

    def _group_offset(self, a):
        return self.slope * ((a * KV_GROUP - self.i) * TK).astype(F32)

    def _update(self, a, s, mx):
        off = self._group_offset(a)
        m_prev = self.m_ref[...]
        m_new = jnp.maximum(m_prev, mx + off)
        alpha = jnp.exp2(m_prev - m_new)
        p = jnp.exp2(s - (m_new - off))
        self.m_ref[...] = m_new
        self.l_ref[...] = alpha * self.l_ref[...] + jnp.sum(p, axis=0, keepdims=True)
        self.acc_ref[...] = alpha * self.acc_ref[...] + lax.dot_general(
            self.v_ref[self._group_rows(a), :], p.astype(BF16), (((0,), (0,)), ((), ())),
            preferred_element_type=F32)

    def full_update(self, buf, a):
        self._update(a, self.s_refs[buf][...], self.mx_refs[buf][...])

    def last_update(self, buf, a):
        kk = lax.broadcasted_iota(jnp.int32, (TK, TQ), 0)
        qq = lax.broadcasted_iota(jnp.int32, (TK, TQ), 1)
        diag = jnp.where(kk // CHUNK <= qq // CHUNK, self.slope * (qq - jnp.abs(qq - kk) - kk).astype(F32), NEG)
        diag = jnp.concatenate([diag, diag], axis=1)
        off = self._group_offset(a)
        scores, offsets = [], []
        for g in range(KV_GROUP):
            jb = a * KV_GROUP + g
            scores.append(self.s_refs[buf][g * TK:(g + 1) * TK, :] + jnp.where(jb == self.i, diag, 0.0))
            offsets.append(jnp.where(jb > self.i, NEG, off))
        m_prev = self.m_ref[...]
        m_new = m_prev
        for s, o in zip(scores, offsets):
            m_new = jnp.maximum(m_new, jnp.max(s, axis=0, keepdims=True) + o)
        alpha = jnp.exp2(m_prev - m_new)
        p = jnp.concatenate([jnp.exp2(s - (m_new - o)) for s, o in zip(scores, offsets)], axis=0)
        self.m_ref[...] = m_new
        self.l_ref[...] = alpha * self.l_ref[...] + jnp.sum(p, axis=0, keepdims=True)
        self.acc_ref[...] = alpha * self.acc_ref[...] + lax.dot_general(
            self.v_ref[self._group_rows(a), :], p.astype(BF16), (((0,), (0,)), ((), ())),
            preferred_element_type=F32)

    def output(self, lam, subln):
        o = self.acc_ref[...] / self.l_ref[...]
        o = o[:, :TQ] - lam * o[:, TQ:]
        o = o * lax.rsqrt(jnp.mean(o * o, axis=0, keepdims=True) + LN_EPS) * subln
        return (o * (1.0 - LAM_INIT)).T.astype(BF16)


def _diff_attn_kernel(slopes_ref, lq1_ref, lk1_ref, lq2_ref, lk2_ref, subln_ref, *refs):
    hp = pl.program_id(1)
    i = pl.program_id(2)
    nh = HEADS_PER_STEP
    q_refs, k_refs, v_refs = refs[:nh], refs[nh:2 * nh], refs[2 * nh:3 * nh]
    o_ref = refs[3 * nh]
    scratch = refs[3 * nh + 1:]
    heads = []
    for t in range(nh):
        s0, s1, mx0, mx1, m, l, acc = scratch[7 * t:7 * t + 7]
        heads.append(_DiffHead(slopes_ref[hp * nh + t] * LOG2E, i, q_refs[t], k_refs[t], v_refs[t],
                               (s0, s1), (mx0, mx1), m, l, acc))

    n_full = i // KV_GROUP
    odd = n_full % 2

    @pl.when(odd == 1)
    def _():
        for h in heads:
            h.scores_into(1, 0)
        for h in heads:
            h.scores_into(0, 1)
        for h in heads:
            h.full_update(1, 0)

    @pl.when(odd == 0)
    def _():
        for h in heads:
            h.scores_into(0, 0)

    def pair(b, _):
        a = odd + 2 * b
        for h in heads:
            h.scores_into(1, a + 1)
        for h in heads:
            h.full_update(0, a)
        for h in heads:
            h.scores_into(0, a + 2)
        for h in heads:
            h.full_update(1, a + 1)
        return 0
    lax.fori_loop(0, (n_full - odd) // 2, pair, 0)

    for h in heads:
        h.last_update(0, n_full)

    lam = (jnp.exp(jnp.sum(lq1_ref[...] * lk1_ref[...], axis=-1, keepdims=True))
           - jnp.exp(jnp.sum(lq2_ref[...] * lk2_ref[...], axis=-1, keepdims=True)) + LAM_INIT)
    for t, h in enumerate(heads):
        o_ref[:, t * LANES:(t + 1) * LANES] = h.output(lam, subln_ref[...])


def _diff_attention(qkv, slopes, lq1, lk1, lq2, lk2, subln, batch, seq):
    nq = seq // TQ
    assert seq % (TQ * KV_GROUP) == 0, "the last key group of a query block must stay inside the sequence"
    n = batch * seq
    nh = HEADS_PER_STEP
    vec = lambda w: pl.BlockSpec((1, w), lambda b, h, i: (0, 0))
    q_spec = lambda t: pl.BlockSpec((None, TQ, LANES), lambda b, h, i: (h * nh + t, b * nq + i, 0))
    kv_spec = lambda t, base: pl.BlockSpec((None, seq, LANES), lambda b, h, i: (base + h * nh + t, b, 0))
    stat = pltpu.VMEM((1, 2 * TQ), F32)
    head_scratch = [pltpu.VMEM((KV_GROUP * TK, 2 * TQ), F32), pltpu.VMEM((KV_GROUP * TK, 2 * TQ), F32),
                    stat, stat, stat, stat, pltpu.VMEM((LANES, 2 * TQ), F32)]
    return pl.pallas_call(
        _diff_attn_kernel,
        grid=(batch, N_HEADS_A // nh, nq),
        in_specs=[
            pl.BlockSpec(memory_space=pltpu.SMEM),
            vec(HEAD_DIM_A), vec(HEAD_DIM_A), vec(HEAD_DIM_A), vec(HEAD_DIM_A),
            pl.BlockSpec((2 * HEAD_DIM_A, 1), lambda b, h, i: (0, 0)),
            *[q_spec(t) for t in range(nh)],
            *[kv_spec(t, N_HEADS_A) for t in range(nh)],
            *[kv_spec(t, 2 * N_HEADS_A) for t in range(nh)],
        ],
        out_specs=pl.BlockSpec((TQ, nh * LANES), lambda b, h, i: (b * nq + i, h)),
        out_shape=jax.ShapeDtypeStruct((n, WIDTH_A), BF16),
        scratch_shapes=head_scratch * nh,
        compiler_params=_params(("parallel", "parallel", "arbitrary")),
        name="diff_attn",
    )(slopes, lq1, lk1, lq2, lk2, subln, *([qkv] * (3 * nh)))


def _band_attn_kernel(*refs):
    npair = PAIRS_PER_STEP
    bias_refs, q_refs = refs[:npair], refs[npair:2 * npair]
    k_refs, v_refs = refs[2 * npair:3 * npair], refs[3 * npair:4 * npair]
    o_ref = refs[4 * npair]
    s_refs = refs[4 * npair + 1:]
    n_sub = BAND // TK
    nq = o_ref.shape[0] // TQ
    feat = lax.broadcasted_iota(jnp.int32, (LANES, TQ), 0)

    def key_rows(i, jj):
        start = i * TQ - N_PREV_CHUNKS * CHUNK + jj * TK
        return start, pl.ds(pl.multiple_of(jnp.maximum(start, 0), TK), TK)

    def scores_into(buf, i):
        for t in range(npair):
            q2 = _stack_halves(q_refs[t][pl.ds(pl.multiple_of(i * TQ, TQ), TQ), :])
            for jj in range(n_sub):
                _, rows = key_rows(i, jj)
                s_refs[2 * t + buf][jj * TK:(jj + 1) * TK, :] = (
                    _dot_nt(k_refs[t][rows, :], q2) + bias_refs[t][jj * TK:(jj + 1) * TK, :])

    def softmax_out(buf, i):
        starts_rows = [key_rows(i, jj) for jj in range(n_sub)]
        offsets = [jnp.where(start < 0, NEG, 0.0).astype(F32) for start, _ in starts_rows]
        for t in range(npair):
            scores = [s_refs[2 * t + buf][jj * TK:(jj + 1) * TK, :] for jj in range(n_sub)]
            m = functools.reduce(jnp.maximum, [jnp.max(s, axis=0, keepdims=True) + off
                                               for s, off in zip(scores, offsets)])
            l = jnp.zeros((1, 2 * TQ), F32)
            acc = jnp.zeros((LANES, 2 * TQ), F32)
            for jj in range(n_sub):
                p = jnp.exp2(scores[jj] - (m - offsets[jj]))
                l = l + jnp.sum(p, axis=0, keepdims=True)
                acc = acc + lax.dot_general(v_refs[t][starts_rows[jj][1], :], p.astype(BF16),
                                            (((0,), (0,)), ((), ())), preferred_element_type=F32)
            o = acc / l
            o_ref[pl.ds(pl.multiple_of(i * TQ, TQ), TQ), t * LANES:(t + 1) * LANES] = (
                jnp.where(feat < 64, o[:, :TQ], o[:, TQ:]).T.astype(BF16))

    scores_into(0, 0)

    def two_blocks(b, _):
        i = 2 * b
        scores_into(1, i + 1)
        softmax_out(0, i)
        scores_into(0, jnp.minimum(i + 2, nq - 1))
        softmax_out(1, i + 1)
        return 0
    lax.fori_loop(0, nq // 2, two_blocks, 0)


BIAS_SPAN = 1024


def _band_bias_kernel(line_ref, o_ref):
    kj = lax.broadcasted_iota(jnp.int32, (BAND, TQ), 0)
    qi = lax.broadcasted_iota(jnp.int32, (BAND, TQ), 1)
    kc = kj // CHUNK - N_PREV_CHUNKS
    qc = qi // CHUNK
    allowed = jnp.logical_and(kc <= qc, kc >= qc - N_PREV_CHUNKS)
    for hh in range(2):
        line = jnp.broadcast_to(line_ref[hh:hh + 1, :], (BAND, BIAS_SPAN))
        rolled = pltpu.roll(line, BIAS_SPAN - (BAND - 1), 1, stride=1, stride_axis=0)
        o_ref[:, hh * TQ:(hh + 1) * TQ] = jnp.where(allowed, LOG2E * rolled[:, :TQ], NEG)


def _band_bias(rel_bias):
    rb = rel_bias.astype(F32)
    n_low = (BAND - 1) - REL_CLIP
    n_high = (BAND + TQ - 1) - (BAND - 1) - REL_CLIP - 1
    line = jnp.concatenate([jnp.broadcast_to(rb[:, :1], (N_HEADS_B, n_low)), rb,
                            jnp.broadcast_to(rb[:, -1:], (N_HEADS_B, n_high))], axis=1)
    line = jnp.pad(line[:, ::-1], ((0, 0), (0, BIAS_SPAN - line.shape[1])))
    pairs = N_HEADS_B // 2
    return pl.pallas_call(
        _band_bias_kernel,
        grid=(pairs,),
        in_specs=[pl.BlockSpec((None, 2, BIAS_SPAN), lambda p: (p, 0, 0))],
        out_specs=pl.BlockSpec((None, BAND, 2 * TQ), lambda p: (p, 0, 0)),
        out_shape=jax.ShapeDtypeStruct((pairs, BAND, 2 * TQ), F32),
        compiler_params=_params(("parallel",)),
        name="band_bias",
    )(line.reshape(pairs, 2, BIAS_SPAN))


def _band_attention(qkv, bias, batch, seq):
    assert (seq // TQ) % 2 == 0, "query blocks are walked two at a time"
    n = batch * seq
    base = 3 * WIDTH_A // LANES
    pairs = N_HEADS_B // 2
    npair = PAIRS_PER_STEP
    bias_spec = lambda t: pl.BlockSpec((None, BAND, 2 * TQ), lambda b, p: (p * npair + t, 0, 0))
    slab_spec = lambda t, off: pl.BlockSpec((None, seq, LANES), lambda b, p: (base + off + p * npair + t, b, 0))
    return pl.pallas_call(
        _band_attn_kernel,
        grid=(batch, pairs // npair),
        in_specs=[
            *[bias_spec(t) for t in range(npair)],
            *[slab_spec(t, 0) for t in range(npair)],
            *[slab_spec(t, pairs) for t in range(npair)],
            *[slab_spec(t, 2 * pairs) for t in range(npair)],
        ],
        out_specs=pl.BlockSpec((seq, npair * LANES), lambda b, p: (b, p)),
        out_shape=jax.ShapeDtypeStruct((n, WIDTH_B), BF16),
        scratch_shapes=[pltpu.VMEM((BAND, 2 * TQ), F32)] * (2 * npair),
        compiler_params=_params(("parallel", "parallel")),
        name="band_attn",
    )(*([bias] * npair), *([qkv] * (3 * npair)))


def _pack_bf16_pairs(x):
    w = x.shape[1] // 2
    bits = pltpu.bitcast(x.astype(BF16).astype(F32), jnp.uint32)
    return (bits[:, :w] >> 16) | (bits[:, w:] & jnp.uint32(0xFFFF0000))


def _unpack_bf16_pairs(p):
    lo = pltpu.bitcast(p << 16, F32)
    hi = pltpu.bitcast(p & jnp.uint32(0xFFFF0000), F32)
    return jnp.concatenate([lo, hi], axis=1).astype(BF16)


def _merge_kernel(oa_ref, ob_ref, gates_ref, h_ref, pa_ref, pb_ref, wo_ref, g_ref, b_ref, wra_ref, wrb_ref, br_ref,
                  h1_ref, h1p_ref, gw_ref, idx_ref, rank_ref, counts_ref, seen_ref):
    ma = jnp.dot(oa_ref[...], pa_ref[...], preferred_element_type=F32)
    mb = jnp.dot(ob_ref[...], pb_ref[...], preferred_element_type=F32)
    merged = gates_ref[:, :D_MODEL].astype(F32) * ma + gates_ref[:, D_MODEL:].astype(F32) * mb
    m = jnp.dot(merged.astype(BF16), wo_ref[...], preferred_element_type=F32)
    h1 = _layer_norm(DEEPNORM_ALPHA * h_ref[...] + m, g_ref[...], b_ref[...])
    h1_ref[...] = h1
    h1p_ref[...] = _pack_bf16_pairs(h1)

    h_hi = h1.astype(BF16)
    h_lo = (h1 - h_hi.astype(F32)).astype(BF16)
    part_a = _dot_nt(wra_ref[...], h_hi)
    part_b = _dot_nt(wrb_ref[...], h_lo)
    logits = part_a[:N_EXPERTS] + part_a[N_EXPERTS:2 * N_EXPERTS] + part_b[:N_EXPERTS] + br_ref[...]

    tm = logits.shape[1]
    expert = lax.broadcasted_iota(jnp.int32, (N_EXPERTS, tm), 0)
    vals = logits
    top_val, top_idx = [], []
    for _ in range(TOP_K):
        mx = jnp.max(vals, axis=0, keepdims=True)
        sel = jnp.min(jnp.where(vals == mx, expert, N_EXPERTS), axis=0, keepdims=True)
        top_val.append(mx)
        top_idx.append(sel)
        vals = jnp.where(expert == sel, -jnp.inf, vals)
    ex = [jnp.exp(v - top_val[0]) for v in top_val]
    denom = functools.reduce(jnp.add, ex)
    gw_ref[...] = jnp.concatenate([e / denom for e in ex], axis=0)
    idx_ref[...] = jnp.concatenate(top_idx, axis=0)

    @pl.when(pl.program_id(0) == 0)
    def _():
        seen_ref[...] = jnp.zeros_like(seen_ref)

    chosen = functools.reduce(jnp.logical_or, [expert == s for s in top_idx])
    onehot = jnp.where(chosen, 1.0, 0.0)
    r_i = lax.broadcasted_iota(jnp.int32, (tm, tm), 0)
    c_i = lax.broadcasted_iota(jnp.int32, (tm, tm), 1)
    earlier = jnp.where(r_i < c_i, 1.0, 0.0)
    before = jnp.dot(onehot, earlier, preferred_element_type=F32) + seen_ref[...]
    rank_ref[...] = jnp.concatenate([jnp.sum(jnp.where(expert == s, before, 0.0), axis=0, keepdims=True)
                                     for s in top_idx], axis=0).astype(jnp.int32)
    seen_ref[...] = seen_ref[...] + jnp.sum(onehot, axis=1, keepdims=True)
    counts_ref[...] = seen_ref[...].astype(jnp.int32)


def _merge(oa, ob, gates, h, pa, pb, wo, g, b, wr, br):
    n = h.shape[0]
    tm = min(TM_MERGE, n)
    wr_hi = wr.astype(BF16)
    wr_lo = (wr - wr_hi.astype(F32)).astype(BF16)
    pad = lambda a: jnp.pad(a, ((0, 0), (0, LANES - a.shape[1]))).T
    row = lambda w: pl.BlockSpec((tm, w), lambda i: (i, 0))
    full = lambda r, c: pl.BlockSpec((r, c), lambda i: (0, 0))
    slot_row = pl.BlockSpec((TOP_K, tm), lambda i: (0, i))
    return pl.pallas_call(
        _merge_kernel,
        grid=(n // tm,),
        in_specs=[row(WIDTH_A), row(WIDTH_B), row(GATE_COLS), row(D_MODEL),
                  full(WIDTH_A, D_MODEL), full(WIDTH_B, D_MODEL), full(D_MODEL, D_MODEL),
                  full(1, D_MODEL), full(1, D_MODEL), full(LANES, D_MODEL), full(LANES, D_MODEL),
                  full(N_EXPERTS, 1)],
        out_specs=[row(D_MODEL), row(D_MODEL // 2), slot_row, slot_row, slot_row, full(N_EXPERTS, 1)],
        out_shape=[jax.ShapeDtypeStruct((n, D_MODEL), F32), jax.ShapeDtypeStruct((n, D_MODEL // 2), jnp.uint32),
                   jax.ShapeDtypeStruct((TOP_K, n), F32), jax.ShapeDtypeStruct((TOP_K, n), jnp.int32),
                   jax.ShapeDtypeStruct((TOP_K, n), jnp.int32), jax.ShapeDtypeStruct((N_EXPERTS, 1), jnp.int32)],
        scratch_shapes=[pltpu.VMEM((N_EXPERTS, 1), F32)],
        compiler_params=_params(("arbitrary",)),
        name="merge_ln1_router",
    )(oa, ob, gates, h, pa, pb, wo, g, b, pad(jnp.concatenate([wr_hi, wr_lo], axis=1)), pad(wr_hi),
      br.reshape(N_EXPERTS, 1))


def _dispatch_kernel(pend_ref, dest_ref, h1p_ref, xs_hbm, zero_ref, sem, zero_sem):
    @pl.when(pl.program_id(0) == 0)
    def _():
        zero_ref[...] = jnp.zeros_like(zero_ref)

        def fill(row0):
            block = pl.ds(pl.multiple_of(row0, ROWS_MOE), ROWS_MOE)
            return pltpu.make_async_copy(zero_ref, xs_hbm.at[block, :], zero_sem)
        for e in range(N_EXPERTS):
            fill(jnp.maximum(pend_ref[e] - ROWS_MOE, 0)).start()
        for e in range(N_EXPERTS):
            fill(0).wait()

        first_unused = pend_ref[N_EXPERTS - 1] // ROWS_MOE
        n_blocks = xs_hbm.shape[0] // ROWS_MOE

        @pl.loop(first_unused, n_blocks)
        def _(blk):
            fill(blk * ROWS_MOE).start()

        @pl.loop(first_unused, n_blocks)
        def _(blk):
            fill(0).wait()

    def start(group, _):
        for j in range(SUBLANES):
            for k in range(TOP_K):
                pltpu.make_async_copy(h1p_ref.at[group, pl.ds(j, 1), :],
                                      xs_hbm.at[pl.ds(dest_ref[0, 0, (group * SUBLANES + j) * TOP_K + k], 1), :],
                                      sem).start(priority=k % 2)
        return 0
    lax.fori_loop(0, TM_DISPATCH // SUBLANES, start, 0)

    for k in range(TOP_K):
        pltpu.make_async_copy(h1p_ref, h1p_ref, sem).wait()


def _dispatch(dest, pend, h1p, n_rows):
    n = h1p.shape[0]
    tm = min(TM_DISPATCH, n)
    assert tm == TM_DISPATCH
    n_steps = n // tm
    grid_spec = pltpu.PrefetchScalarGridSpec(
        num_scalar_prefetch=1,
        grid=(n_steps,),
        in_specs=[
            pl.BlockSpec((1, 1, tm * TOP_K), lambda i, pe: (i, 0, 0), memory_space=pltpu.SMEM),
            pl.BlockSpec((tm // SUBLANES, SUBLANES, D_MODEL // 2), lambda i, pe: (i, 0, 0)),
        ],
        out_specs=pl.BlockSpec(memory_space=pl.ANY),
        scratch_shapes=[pltpu.VMEM((ROWS_MOE, D_MODEL // 2), jnp.uint32), pltpu.SemaphoreType.DMA(()),
                        pltpu.SemaphoreType.DMA(())],
    )
    return pl.pallas_call(
        _dispatch_kernel,
        grid_spec=grid_spec,
        out_shape=jax.ShapeDtypeStruct((n_rows, D_MODEL // 2), jnp.uint32),
        compiler_params=_params(("arbitrary",)),
        name="dispatch_rows",
    )(pend, dest.reshape(n_steps, 1, tm * TOP_K), h1p.reshape(n // SUBLANES, SUBLANES, D_MODEL // 2))


def _row_gather(src_hbm, idx_ref, dst_ref, sem, n_rows):
    def start(group, _):
        for j in range(SUBLANES):
            pltpu.make_async_copy(src_hbm.at[pl.ds(idx_ref[0, 0, group * SUBLANES + j], 1), :],
                                  dst_ref.at[group, pl.ds(j, 1), :], sem).start(priority=j % 2)
        return 0
    lax.fori_loop(0, n_rows // SUBLANES, start, 0)


def _moe_kernel(bexp_ref, nused_ref, nvalid_ref, x_ref, wi_ref, bi_ref, wo_ref, bo_ref, y_ref, wi_bf, wo_bf):
    i = pl.program_id(0)
    n_used = nused_ref[0]
    n_valid = nvalid_ref[i]
    new_expert = jnp.logical_or(i == 0, bexp_ref[i] != bexp_ref[jnp.maximum(i - 1, 0)])
    half = ROWS_MOE // 2

    @pl.when(jnp.logical_and(new_expert, i < n_used))
    def _():
        wi_bf[...] = wi_ref[...].astype(BF16)
        wo_bf[...] = wo_ref[...].astype(BF16)

    def ffn(rows):
        x = _unpack_bf16_pairs(x_ref[rows, :])
        hb = jnp.dot(x, wi_bf[...], preferred_element_type=F32) + bi_ref[...]
        g = jnp.minimum(hb[:, :D_FF], SWIGLU_LIMIT)
        u = jnp.clip(hb[:, D_FF:], -SWIGLU_LIMIT, SWIGLU_LIMIT)
        a = g * jax.nn.sigmoid(SWIGLU_ALPHA * g) * (u + 1.0)
        y_ref[rows, :] = jnp.dot(a.astype(BF16), wo_bf[...], preferred_element_type=F32) + bo_ref[...]

    @pl.when(jnp.logical_and(i < n_used, n_valid > half))
    def _():
        ffn(slice(0, ROWS_MOE))

    @pl.when(jnp.logical_and(i < n_used, n_valid <= half))
    def _():
        ffn(slice(0, half))
        y_ref[half:, :] = jnp.zeros((ROWS_MOE - half, D_MODEL), F32)

    @pl.when(i >= n_used)
    def _():
        y_ref[...] = jnp.zeros_like(y_ref)


def _moe(block_exp, n_used, n_valid, x_sorted, wi, bi, wo, bo):
    n_blocks = block_exp.shape[0]
    x_block = lambda i, be, nu, nv: (jnp.maximum(jnp.minimum(i, nu[0] - 1), 0), 0)
    expert = lambda i, be, nu, nv: (be[i], 0, 0)
    grid_spec = pltpu.PrefetchScalarGridSpec(
        num_scalar_prefetch=3,
        grid=(n_blocks,),
        in_specs=[
            pl.BlockSpec((ROWS_MOE, D_MODEL // 2), x_block),
            pl.BlockSpec((None, D_MODEL, 2 * D_FF), expert),
            pl.BlockSpec((None, 1, 2 * D_FF), expert),
            pl.BlockSpec((None, D_FF, D_MODEL), expert),
            pl.BlockSpec((None, 1, D_MODEL), expert),
        ],
        out_specs=pl.BlockSpec((ROWS_MOE, D_MODEL), lambda i, be, nu, nv: (i, 0)),
        scratch_shapes=[pltpu.VMEM((D_MODEL, 2 * D_FF), BF16), pltpu.VMEM((D_FF, D_MODEL), BF16)],
    )
    return pl.pallas_call(
        _moe_kernel,
        grid_spec=grid_spec,
        out_shape=jax.ShapeDtypeStruct((n_blocks * ROWS_MOE, D_MODEL), F32),
        compiler_params=_params(("arbitrary",)),
        name="moe_experts",
    )(block_exp, n_used, n_valid, x_sorted, wi, bi, wo, bo)


def _combine_kernel(dest_cur_ref, dest_nxt_ref, gw_ref, h1_ref, g_ref, b_ref, y_hbm, o_ref, ybuf0, ybuf1, sems):
    i = pl.program_id(0)
    n_steps = pl.num_programs(0)
    rows = TOP_K * TM_COMB
    tiles = TM_COMB // SUBLANES

    def wait(buf, sem):
        pltpu.make_async_copy(buf, buf, sem).wait()

    @pl.when(i == 0)
    def _():
        _row_gather(y_hbm, dest_cur_ref, ybuf0, sems.at[0], rows)

    def step(cur, cur_sem, nxt, nxt_sem):
        wait(cur, cur_sem)
        for r in range(rows):
            pltpu.make_async_copy(y_hbm.at[pl.ds(dest_nxt_ref[0, 0, r], 1), :],
                                  nxt.at[r // SUBLANES, pl.ds(r % SUBLANES, 1), :], nxt_sem).start(priority=r % 2)
        gw = gw_ref[...]
        f = jnp.zeros((TM_COMB, D_MODEL), F32)
        for k in range(TOP_K):
            f = f + gw[:, k:k + 1] * cur[k * tiles:(k + 1) * tiles].reshape(TM_COMB, D_MODEL)
        o_ref[...] = _layer_norm(DEEPNORM_ALPHA * h1_ref[...] + f, g_ref[...], b_ref[...])

        @pl.when(i == n_steps - 1)
        def _():
            wait(nxt, nxt_sem)

    @pl.when(i % 2 == 0)
    def _():
        step(ybuf0, sems.at[0], ybuf1, sems.at[1])

    @pl.when(i % 2 == 1)
    def _():
        step(ybuf1, sems.at[1], ybuf0, sems.at[0])


def _combine(dest_km, gate_w, h1, g, b, y_buf):
    n = h1.shape[0]
    n_steps = n // TM_COMB
    rows = TOP_K * TM_COMB
    return pl.pallas_call(
        _combine_kernel,
        grid=(n_steps,),
        in_specs=[
            pl.BlockSpec((1, 1, rows), lambda i: (i, 0, 0), memory_space=pltpu.SMEM),
            pl.BlockSpec((1, 1, rows), lambda i: (jnp.minimum(i + 1, n_steps - 1), 0, 0), memory_space=pltpu.SMEM),
            pl.BlockSpec((TM_COMB, TOP_K), lambda i: (i, 0)),
            pl.BlockSpec((TM_COMB, D_MODEL), lambda i: (i, 0)),
            pl.BlockSpec((1, D_MODEL), lambda i: (0, 0)),
            pl.BlockSpec((1, D_MODEL), lambda i: (0, 0)),
            pl.BlockSpec(memory_space=pl.ANY),
        ],
        out_specs=pl.BlockSpec((TM_COMB, D_MODEL), lambda i: (i, 0)),
        out_shape=jax.ShapeDtypeStruct((n, D_MODEL), F32),
        scratch_shapes=[pltpu.VMEM((rows // SUBLANES, SUBLANES, D_MODEL), F32),
                        pltpu.VMEM((rows // SUBLANES, SUBLANES, D_MODEL), F32), pltpu.SemaphoreType.DMA((2,))],
        compiler_params=_params(("arbitrary",)),
        name="combine_ln2",
    )(dest_km, dest_km, gate_w, h1, g, b, y_buf)


def _block_layout(top_idx, rank, counts):
    n_slots = top_idx.shape[1] * TOP_K
    padded = (counts + ROWS_MOE - 1) // ROWS_MOE * ROWS_MOE
    pend = jnp.cumsum(padded)
    pstart = pend - padded
    experts = jnp.arange(N_EXPERTS, dtype=jnp.int32)[:, None, None]
    dest = rank + jnp.sum(jnp.where(top_idx[None] == experts, pstart[:, None, None], 0), axis=0)
    n_blocks = n_slots // ROWS_MOE + N_EXPERTS
    block_start = jnp.arange(n_blocks, dtype=jnp.int32) * ROWS_MOE
    block_exp = jnp.minimum(jnp.sum(block_start[:, None] >= pend[None, :], axis=-1), N_EXPERTS - 1).astype(jnp.int32)
    n_used = (pend[-1] // ROWS_MOE).astype(jnp.int32).reshape(1)
    onehot_exp = block_exp[:, None] == jnp.arange(N_EXPERTS, dtype=jnp.int32)[None, :]
    seg_end = jnp.sum(jnp.where(onehot_exp, (pstart + counts)[None, :], 0), axis=-1)
    n_valid = jnp.clip(seg_end - block_start, 0, ROWS_MOE).astype(jnp.int32)
    return dest.astype(jnp.int32), pend.astype(jnp.int32), block_exp, n_used, n_valid


def kernel(x, ln_in_g, ln_in_b, w_in, b_gate, lambda_q1, lambda_k1, lambda_q2, lambda_k2, subln_w, rel_bias,
           w_branch_a, w_branch_b, w_out, ln1_g, ln1_b, w_router, b_router, w_exp_in, b_exp_in, w_exp_out,
           b_exp_out, ln2_g, ln2_b):
    batch, seq, d = x.shape
    n = batch * seq
    row = lambda a: a.reshape(1, -1).astype(F32)
    l = 0
    h, qkv, gates = _ln_proj(x.reshape(n, d), row(ln_in_g), row(ln_in_b), w_in[l].astype(BF16), row(b_gate[l]))
    slopes = jnp.asarray([2.0 ** (-8.0 * (i + 1) / N_HEADS_A) for i in range(N_HEADS_A)], F32)
    out_a = _diff_attention(qkv, slopes, row(lambda_q1[l]), row(lambda_k1[l]), row(lambda_q2[l]),
                            row(lambda_k2[l]), subln_w[l].reshape(-1, 1).astype(F32), batch, seq)
    out_b = _band_attention(qkv, _band_bias(rel_bias[l]), batch, seq)
    h1, h1p, gate_w, top_idx, rank, counts = _merge(
        out_a, out_b, gates, h, w_branch_a[l].astype(BF16), w_branch_b[l].astype(BF16), w_out[l].astype(BF16),
        row(ln1_g[l]), row(ln1_b[l]), w_router[l].astype(F32), row(b_router[l]))
    dest, pend, block_exp, n_used, n_valid = _block_layout(top_idx, rank, counts[:, 0])
    x_sorted = _dispatch(dest.T, pend, h1p, block_exp.shape[0] * ROWS_MOE)
    y_buf = _moe(block_exp, n_used, n_valid, x_sorted, w_exp_in[l].astype(F32), b_exp_in[l].reshape(N_EXPERTS, 1, -1),
                 w_exp_out[l].astype(F32), b_exp_out[l].reshape(N_EXPERTS, 1, -1))
    n_steps = n // TM_COMB
    dest_km = dest.reshape(TOP_K, n_steps, TM_COMB).transpose(1, 0, 2).reshape(n_steps, 1, TOP_K * TM_COMB)
    out = _combine(dest_km, gate_w.T, h1, row(ln2_g[l]), row(ln2_b[l]), y_buf)
    return out.reshape(batch, seq, d)
```

```python
import functools
import math

import jax
import jax.numpy as jnp
import numpy as np
from jax import lax
from jax.experimental import pallas as pl
from jax.experimental.pallas import tpu as pltpu

F32 = jnp.float32
BF16 = jnp.bfloat16

D_MODEL = 1024
CHUNK = 64
N_HEADS_A = 4
HEAD_DIM_A = 64
WIDTH_A = 512
N_HEADS_B = 8
HEAD_DIM_B = 64
WIDTH_B = 512
N_PREV_CHUNKS = 8
REL_CLIP = 128
N_EXPERTS = 32
TOP_K = 4
D_FF = 1024
SWIGLU_ALPHA = 1.702
SWIGLU_LIMIT = 7.0
DEEPNORM_ALPHA = 2.0 ** 0.25
LN_EPS = 1e-5
LAM_INIT = 0.8 - 0.6 * math.exp(-0.3 * 0)

LANES = 128
SUBLANES = 8
N_SLABS = (3 * WIDTH_A + 3 * WIDTH_B) // LANES
GATE_COLS = 2 * D_MODEL
IN_COLS = 3 * WIDTH_A + 3 * WIDTH_B + GATE_COLS
LOG2E = math.log2(math.e)
NEG = -1e30

TM_PROJ = 1024
PROJ_CHUNK = 512
TQ = 256
TK = 256
KV_GROUP = 2
HEADS_PER_STEP = 4
PAIRS_PER_STEP = 2
BAND = 3 * TK
TM_MERGE = 1024
TM_DISPATCH = 512
ROWS_MOE = 512
TM_COMB = 256
VMEM_LIMIT = 56 * 1024 * 1024


def _layer_norm(x, g, b):
    mu = jnp.mean(x, axis=-1, keepdims=True)
    xc = x - mu
    var = jnp.mean(xc * xc, axis=-1, keepdims=True)
    return xc * lax.rsqrt(var + LN_EPS) * g + b


def _params(sem):
    return pltpu.CompilerParams(dimension_semantics=sem, vmem_limit_bytes=VMEM_LIMIT)


def _ln_proj_kernel(x_ref, g_ref, b_ref, w_ref, bg_ref, h_ref, qkv_ref, gates_ref):
    h = _layer_norm(x_ref[...], g_ref[...], b_ref[...])
    h_ref[...] = h
    hb = h.astype(BF16)
    n_qkv_chunks = (N_SLABS * LANES) // PROJ_CHUNK
    slabs_per_chunk = PROJ_CHUNK // LANES
    q_scale = HEAD_DIM_A ** -0.5 * LOG2E
    for c in range(n_qkv_chunks):
        r = jnp.dot(hb, w_ref[:, c * PROJ_CHUNK:(c + 1) * PROJ_CHUNK], preferred_element_type=F32)
        first = c * slabs_per_chunk
        is_q = (first < WIDTH_A // LANES) or (3 * WIDTH_A // LANES <= first < (3 * WIDTH_A + WIDTH_B) // LANES)
        if is_q:
            r = r * q_scale
        for s in range(slabs_per_chunk):
            qkv_ref[first + s] = r[:, s * LANES:(s + 1) * LANES].astype(BF16)
    g0 = N_SLABS * LANES
    for c in range(GATE_COLS // PROJ_CHUNK):
        r = jnp.dot(hb, w_ref[:, g0 + c * PROJ_CHUNK:g0 + (c + 1) * PROJ_CHUNK], preferred_element_type=F32)
        r = r + bg_ref[:, c * PROJ_CHUNK:(c + 1) * PROJ_CHUNK]
        gates_ref[:, c * PROJ_CHUNK:(c + 1) * PROJ_CHUNK] = jax.nn.sigmoid(r).astype(BF16)


def _ln_proj(x2d, g, b, w_bf16, b_gate):
    n = x2d.shape[0]
    tm = min(TM_PROJ, n)
    const = lambda i: (0, 0)
    return pl.pallas_call(
        _ln_proj_kernel,
        grid=(n // tm,),
        in_specs=[
            pl.BlockSpec((tm, D_MODEL), lambda i: (i, 0)),
            pl.BlockSpec((1, D_MODEL), const),
            pl.BlockSpec((1, D_MODEL), const),
            pl.BlockSpec((D_MODEL, IN_COLS), const, pipeline_mode=pl.Buffered(1)),
            pl.BlockSpec((1, GATE_COLS), const),
        ],
        out_specs=[
            pl.BlockSpec((tm, D_MODEL), lambda i: (i, 0)),
            pl.BlockSpec((N_SLABS, tm, LANES), lambda i: (0, i, 0)),
            pl.BlockSpec((tm, GATE_COLS), lambda i: (i, 0)),
        ],
        out_shape=[
            jax.ShapeDtypeStruct((n, D_MODEL), F32),
            jax.ShapeDtypeStruct((N_SLABS, n, LANES), BF16),
            jax.ShapeDtypeStruct((n, GATE_COLS), BF16),
        ],
        compiler_params=_params(("parallel",)),
        name="ln_proj",
    )(x2d, g, b, w_bf16, b_gate)


def _stack_halves(q):
    lane = lax.broadcasted_iota(jnp.int32, q.shape, 1)
    zero = jnp.zeros_like(q)
    return jnp.concatenate([jnp.where(lane < 64, q, zero), jnp.where(lane >= 64, q, zero)], axis=0)


def _dot_nt(a, b):
    return lax.dot_general(a, b, (((1,), (1,)), ((), ())), preferred_element_type=F32)


def _split3_bf16(x):
    hi = x.astype(BF16)
    r1 = x - hi.astype(F32)
    mid = r1.astype(BF16)
    lo = (r1 - mid.astype(F32)).astype(BF16)
    return hi, mid, lo


class _DiffHead:
    def __init__(self, slope, i, q_ref, k_ref, v_ref, s_refs, mx_refs, m_ref, l_ref, acc_ref):
        self.slope, self.i = slope, i
        self.k_ref, self.v_ref, self.s_refs, self.mx_refs = k_ref, v_ref, s_refs, mx_refs
        self.m_ref, self.l_ref, self.acc_ref = m_ref, l_ref, acc_ref
        lane_q = lax.broadcasted_iota(jnp.int32, (2 * TQ, LANES), 1)
        self.q_aug = jnp.concatenate([_stack_halves(q_ref[...]), jnp.where(lane_q < 3, 1.0, 0.0).astype(BF16)],
                                     axis=1)
        lane_k = lax.broadcasted_iota(jnp.int32, (TK, LANES), 1)
        key_pos = lax.broadcasted_iota(jnp.int32, (TK, LANES), 0)
        self.k_bias = []
        for g in range(KV_GROUP):
            hi, mid, lo = [t.astype(F32) for t in _split3_bf16(slope * (key_pos + g * TK).astype(F32))]
            self.k_bias.append(jnp.where(lane_k == 0, hi, jnp.where(lane_k == 1, mid, jnp.where(lane_k == 2, lo, 0.0))
                                         ).astype(BF16))
        m_ref[...] = jnp.full(m_ref.shape, NEG, F32)
        l_ref[...] = jnp.zeros(l_ref.shape, F32)
        acc_ref[...] = jnp.zeros(acc_ref.shape, F32)

    @staticmethod
    def _group_rows(a):
        return pl.ds(pl.multiple_of(a * (KV_GROUP * TK), KV_GROUP * TK), KV_GROUP * TK)

    def scores_into(self, buf, a):
        mx = None
        for g in range(KV_GROUP):
            rows = pl.ds(pl.multiple_of((a * KV_GROUP + g) * TK, TK), TK)
            s = _dot_nt(jnp.concatenate([self.k_ref[rows, :], self.k_bias[g]], axis=1), self.q_aug)
            self.s_refs[buf][g * TK:(g + 1) * TK, :] = s
            mg = jnp.max(s, axis=0, keepdims=True)
            mx = mg if mx is None else jnp.maximum(mx, mg)
        self.mx_refs[buf][...] = mx

    def _group_offset(self, a):
        return self.slope * ((a * KV_GROUP - self.i) * TK).astype(F32)

    def _update(self, a, s, mx):
        off = self._group_offset(a)
        m_prev = self.m_ref[...]
        m_new = jnp.maximum(m_prev, mx + off)
        alpha = jnp.exp2(m_prev - m_new)
        p = jnp.exp2(s - (m_new - off))
        self.m_ref[...] = m_new
        self.l_ref[...] = alpha * self.l_ref[...] + jnp.sum(p, axis=0, keepdims=True)
        self.acc_ref[...] = alpha * self.acc_ref[...] + lax.dot_general(
            self.v_ref[self._group_rows(a), :], p.astype(BF16), (((0,), (0,)), ((), ())),
            preferred_element_type=F32)

    def full_update(self, buf, a):
        self._update(a, self.s_refs[buf][...], self.mx_refs[buf][...])

    def last_update(self, buf, a):
        kk = lax.broadcasted_iota(jnp.int32, (TK, TQ), 0)
        qq = lax.broadcasted_iota(jnp.int32, (TK, TQ), 1)
        diag = jnp.where(kk // CHUNK <= qq // CHUNK, self.slope * (qq - jnp.abs(qq - kk) - kk).astype(F32), NEG)
        diag = jnp.concatenate([diag, diag], axis=1)
        off = self._group_offset(a)
        scores, offsets = [], []
        for g in range(KV_GROUP):
            jb = a * KV_GROUP + g
            scores.append(self.s_refs[buf][g * TK:(g + 1) * TK, :] + jnp.where(jb == self.i, diag, 0.0))
            offsets.append(jnp.where(jb > self.i, NEG, off))
        m_prev = self.m_ref[...]
        m_new = m_prev
        for s, o in zip(scores, offsets):
            m_new = jnp.maximum(m_new, jnp.max(s, axis=0, keepdims=True) + o)
        alpha = jnp.exp2(m_prev - m_new)
        p = jnp.concatenate([jnp.exp2(s - (m_new - o)) for s, o in zip(scores, offsets)], axis=0)
        self.m_ref[...] = m_new
        self.l_ref[...] = alpha * self.l_ref[...] + jnp.sum(p, axis=0, keepdims=True)
        self.acc_ref[...] = alpha * self.acc_ref[...] + lax.dot_general(
            self.v_ref[self._group_rows(a), :], p.astype(BF16), (((0,), (0,)), ((), ())),
            preferred_element_type=F32)

    def output(self, lam, subln):
        o = self.acc_ref[...] / self.l_ref[...]
        o = o[:, :TQ] - lam * o[:, TQ:]
        o = o * lax.rsqrt(jnp.mean(o * o, axis=0, keepdims=True) + LN_EPS) * subln
        return (o * (1.0 - LAM_INIT)).T.astype(BF16)


def _diff_attn_kernel(slopes_ref, lq1_ref, lk1_ref, lq2_ref, lk2_ref, subln_ref, *refs):
    hp = pl.program_id(1)
    i = pl.program_id(2)
    nh = HEADS_PER_STEP
    q_refs, k_refs, v_refs = refs[:nh], refs[nh:2 * nh], refs[2 * nh:3 * nh]
    o_ref = refs[3 * nh]
    scratch = refs[3 * nh + 1:]
    heads = []
    for t in range(nh):
        s0, s1, mx0, mx1, m, l, acc = scratch[7 * t:7 * t + 7]
        heads.append(_DiffHead(slopes_ref[hp * nh + t] * LOG2E, i, q_refs[t], k_refs[t], v_refs[t],
                               (s0, s1), (mx0, mx1), m, l, acc))

    n_full = i // KV_GROUP
    odd = n_full % 2

    @pl.when(odd == 1)
    def _():
        for h in heads:
            h.scores_into(1, 0)
        for h in heads:
            h.scores_into(0, 1)
        for h in heads:
            h.full_update(1, 0)

    @pl.when(odd == 0)
    def _():
        for h in heads:
            h.scores_into(0, 0)

    def pair(b, _):
        a = odd + 2 * b
        for h in heads:
            h.scores_into(1, a + 1)
        for h in heads:
            h.full_update(0, a)
        for h in heads:
            h.scores_into(0, a + 2)
        for h in heads:
            h.full_update(1, a + 1)
        return 0
    lax.fori_loop(0, (n_full - odd) // 2, pair, 0)

    for h in heads:
        h.last_update(0, n_full)

    lam = (jnp.exp(jnp.sum(lq1_ref[...] * lk1_ref[...], axis=-1, keepdims=True))
           - jnp.exp(jnp.sum(lq2_ref[...] * lk2_ref[...], axis=-1, keepdims=True)) + LAM_INIT)
    for t, h in enumerate(heads):
        o_ref[:, t * LANES:(t + 1) * LANES] = h.output(lam, subln_ref[...])


def _diff_attention(qkv, slopes, lq1, lk1, lq2, lk2, subln, batch, seq):
    nq = seq // TQ
    assert seq % (TQ * KV_GROUP) == 0, "the last key group of a query block must stay inside the sequence"
    n = batch * seq
    nh = HEADS_PER_STEP
    vec = lambda w: pl.BlockSpec((1, w), lambda b, h, i: (0, 0))
    q_spec = lambda t: pl.BlockSpec((None, TQ, LANES), lambda b, h, i: (h * nh + t, b * nq + i, 0))
    kv_spec = lambda t, base: pl.BlockSpec((None, seq, LANES), lambda b, h, i: (base + h * nh + t, b, 0))
    stat = pltpu.VMEM((1, 2 * TQ), F32)
    head_scratch = [pltpu.VMEM((KV_GROUP * TK, 2 * TQ), F32), pltpu.VMEM((KV_GROUP * TK, 2 * TQ), F32),
                    stat, stat, stat, stat, pltpu.VMEM((LANES, 2 * TQ), F32)]
    return pl.pallas_call(
        _diff_attn_kernel,
        grid=(batch, N_HEADS_A // nh, nq),
        in_specs=[
            pl.BlockSpec(memory_space=pltpu.SMEM),
            vec(HEAD_DIM_A), vec(HEAD_DIM_A), vec(HEAD_DIM_A), vec(HEAD_DIM_A),
            pl.BlockSpec((2 * HEAD_DIM_A, 1), lambda b, h, i: (0, 0)),
            *[q_spec(t) for t in range(nh)],
            *[kv_spec(t, N_HEADS_A) for t in range(nh)],
            *[kv_spec(t, 2 * N_HEADS_A) for t in range(nh)],
        ],
        out_specs=pl.BlockSpec((TQ, nh * LANES), lambda b, h, i: (b * nq + i, h)),
        out_shape=jax.ShapeDtypeStruct((n, WIDTH_A), BF16),
        scratch_shapes=head_scratch * nh,
        compiler_params=_params(("parallel", "parallel", "arbitrary")),
        name="diff_attn",
    )(slopes, lq1, lk1, lq2, lk2, subln, *([qkv] * (3 * nh)))


def _band_attn_kernel(*refs):
    npair = PAIRS_PER_STEP
    bias_refs, q_refs = refs[:npair], refs[npair:2 * npair]
    k_refs, v_refs = refs[2 * npair:3 * npair], refs[3 * npair:4 * npair]
    o_ref = refs[4 * npair]
    s_refs = refs[4 * npair + 1:]
    n_sub = BAND // TK
    nq = o_ref.shape[0] // TQ
    feat = lax.broadcasted_iota(jnp.int32, (LANES, TQ), 0)

    def key_rows(i, jj):
        start = i * TQ - N_PREV_CHUNKS * CHUNK + jj * TK
        return start, pl.ds(pl.multiple_of(jnp.maximum(start, 0), TK), TK)

    def scores_into(buf, i):
        for t in range(npair):
            q2 = _stack_halves(q_refs[t][pl.ds(pl.multiple_of(i * TQ, TQ), TQ), :])
            for jj in range(n_sub):
                _, rows = key_rows(i, jj)
                s_refs[2 * t + buf][jj * TK:(jj + 1) * TK, :] = (
                    _dot_nt(k_refs[t][rows, :], q2) + bias_refs[t][jj * TK:(jj + 1) * TK, :])

    def softmax_out(buf, i):
        starts_rows = [key_rows(i, jj) for jj in range(n_sub)]
        offsets = [jnp.where(start < 0, NEG, 0.0).astype(F32) for start, _ in starts_rows]
        for t in range(npair):
            scores = [s_refs[2 * t + buf][jj * TK:(jj + 1) * TK, :] for jj in range(n_sub)]
            m = functools.reduce(jnp.maximum, [jnp.max(s, axis=0, keepdims=True) + off
                                               for s, off in zip(scores, offsets)])
            l = jnp.zeros((1, 2 * TQ), F32)
            acc = jnp.zeros((LANES, 2 * TQ), F32)
            for jj in range(n_sub):
                p = jnp.exp2(scores[jj] - (m - offsets[jj]))
                l = l + jnp.sum(p, axis=0, keepdims=True)
                acc = acc + lax.dot_general(v_refs[t][starts_rows[jj][1], :], p.astype(BF16),
                                            (((0,), (0,)), ((), ())), preferred_element_type=F32)
            o = acc / l
            o_ref[pl.ds(pl.multiple_of(i * TQ, TQ), TQ), t * LANES:(t + 1) * LANES] = (
                jnp.where(feat < 64, o[:, :TQ], o[:, TQ:]).T.astype(BF16))

    scores_into(0, 0)

    def two_blocks(b, _):
        i = 2 * b
        scores_into(1, i + 1)
        softmax_out(0, i)
        scores_into(0, jnp.minimum(i + 2, nq - 1))
        softmax_out(1, i + 1)
        return 0
    lax.fori_loop(0, nq // 2, two_blocks, 0)


BIAS_SPAN = 1024


def _band_bias_kernel(line_ref, o_ref):
    kj = lax.broadcasted_iota(jnp.int32, (BAND, TQ), 0)
    qi = lax.broadcasted_iota(jnp.int32, (BAND, TQ), 1)
    kc = kj // CHUNK - N_PREV_CHUNKS
    qc = qi // CHUNK
    allowed = jnp.logical_and(kc <= qc, kc >= qc - N_PREV_CHUNKS)
    for hh in range(2):
        line = jnp.broadcast_to(line_ref[hh:hh + 1, :], (BAND, BIAS_SPAN))
        rolled = pltpu.roll(line, BIAS_SPAN - (BAND - 1), 1, stride=1, stride_axis=0)
        o_ref[:, hh * TQ:(hh + 1) * TQ] = jnp.where(allowed, LOG2E * rolled[:, :TQ], NEG)


def _band_bias(rel_bias):
    rb = rel_bias.astype(F32)
    n_low = (BAND - 1) - REL_CLIP
    n_high = (BAND + TQ - 1) - (BAND - 1) - REL_CLIP - 1
    line = jnp.concatenate([jnp.broadcast_to(rb[:, :1], (N_HEADS_B, n_low)), rb,
                            jnp.broadcast_to(rb[:, -1:], (N_HEADS_B, n_high))], axis=1)
    line = jnp.pad(line[:, ::-1], ((0, 0), (0, BIAS_SPAN - line.shape[1])))
    pairs = N_HEADS_B // 2
    return pl.pallas_call(
        _band_bias_kernel,
        grid=(pairs,),
        in_specs=[pl.BlockSpec((None, 2, BIAS_SPAN), lambda p: (p, 0, 0))],
        out_specs=pl.BlockSpec((None, BAND, 2 * TQ), lambda p: (p, 0, 0)),
        out_shape=jax.ShapeDtypeStruct((pairs, BAND, 2 * TQ), F32),
        compiler_params=_params(("parallel",)),
        name="band_bias",
    )(line.reshape(pairs, 2, BIAS_SPAN))


def _band_attention(qkv, bias, batch, seq):
    assert (seq // TQ) % 2 == 0, "query blocks are walked two at a time"
    n = batch * seq
    base = 3 * WIDTH_A // LANES
    pairs = N_HEADS_B // 2
    npair = PAIRS_PER_STEP
    bias_spec = lambda t: pl.BlockSpec((None, BAND, 2 * TQ), lambda b, p: (p * npair + t, 0, 0))
    slab_spec = lambda t, off: pl.BlockSpec((None, seq, LANES), lambda b, p: (base + off + p * npair + t, b, 0))
    return pl.pallas_call(
        _band_attn_kernel,
        grid=(batch, pairs // npair),
        in_specs=[
            *[bias_spec(t) for t in range(npair)],
            *[slab_spec(t, 0) for t in range(npair)],
            *[slab_spec(t, pairs) for t in range(npair)],
            *[slab_spec(t, 2 * pairs) for t in range(npair)],
        ],
        out_specs=pl.BlockSpec((seq, npair * LANES), lambda b, p: (b, p)),
        out_shape=jax.ShapeDtypeStruct((n, WIDTH_B), BF16),
        scratch_shapes=[pltpu.VMEM((BAND, 2 * TQ), F32)] * (2 * npair),
        compiler_params=_params(("parallel", "parallel")),
        name="band_attn",
    )(*([bias] * npair), *([qkv] * (3 * npair)))


def _pack_bf16_pairs(x):
    w = x.shape[1] // 2
    bits = pltpu.bitcast(x.astype(BF16).astype(F32), jnp.uint32)
    return (bits[:, :w] >> 16) | (bits[:, w:] & jnp.uint32(0xFFFF0000))


def _unpack_bf16_pairs(p):
    lo = pltpu.bitcast(p << 16, F32)
    hi = pltpu.bitcast(p & jnp.uint32(0xFFFF0000), F32)
    return jnp.concatenate([lo, hi], axis=1).astype(BF16)


def _merge_kernel(oa_ref, ob_ref, gates_ref, h_ref, pa_ref, pb_ref, wo_ref, g_ref, b_ref, wra_ref, wrb_ref, br_ref,
                  h1_ref, h1p_ref, gw_ref, idx_ref, rank_ref, counts_ref, seen_ref):
    ma = jnp.dot(oa_ref[...], pa_ref[...], preferred_element_type=F32)
    mb = jnp.dot(ob_ref[...], pb_ref[...], preferred_element_type=F32)
    merged = gates_ref[:, :D_MODEL].astype(F32) * ma + gates_ref[:, D_MODEL:].astype(F32) * mb
    m = jnp.dot(merged.astype(BF16), wo_ref[...], preferred_element_type=F32)
    h1 = _layer_norm(DEEPNORM_ALPHA * h_ref[...] + m, g_ref[...], b_ref[...])
    h1_ref[...] = h1
    h1p_ref[...] = _pack_bf16_pairs(h1)

    h_hi = h1.astype(BF16)
    h_lo = (h1 - h_hi.astype(F32)).astype(BF16)
    part_a = _dot_nt(wra_ref[...], h_hi)
    part_b = _dot_nt(wrb_ref[...], h_lo)
    logits = part_a[:N_EXPERTS] + part_a[N_EXPERTS:2 * N_EXPERTS] + part_b[:N_EXPERTS] + br_ref[...]

    tm = logits.shape[1]
    expert = lax.broadcasted_iota(jnp.int32, (N_EXPERTS, tm), 0)
    vals = logits
    top_val, top_idx = [], []
    for _ in range(TOP_K):
        mx = jnp.max(vals, axis=0, keepdims=True)
        sel = jnp.min(jnp.where(vals == mx, expert, N_EXPERTS), axis=0, keepdims=True)
        top_val.append(mx)
        top_idx.append(sel)
        vals = jnp.where(expert == sel, -jnp.inf, vals)
    ex = [jnp.exp(v - top_val[0]) for v in top_val]
    denom = functools.reduce(jnp.add, ex)
    gw_ref[...] = jnp.concatenate([e / denom for e in ex], axis=0)
    idx_ref[...] = jnp.concatenate(top_idx, axis=0)

    @pl.when(pl.program_id(0) == 0)
    def _():
        seen_ref[...] = jnp.zeros_like(seen_ref)

    chosen = functools.reduce(jnp.logical_or, [expert == s for s in top_idx])
    onehot = jnp.where(chosen, 1.0, 0.0)
    r_i = lax.broadcasted_iota(jnp.int32, (tm, tm), 0)
    c_i = lax.broadcasted_iota(jnp.int32, (tm, tm), 1)
    earlier = jnp.where(r_i < c_i, 1.0, 0.0)
    before = jnp.dot(onehot, earlier, preferred_element_type=F32) + seen_ref[...]
    rank_ref[...] = jnp.concatenate([jnp.sum(jnp.where(expert == s, before, 0.0), axis=0, keepdims=True)
                                     for s in top_idx], axis=0).astype(jnp.int32)
    seen_ref[...] = seen_ref[...] + jnp.sum(onehot, axis=1, keepdims=True)
    counts_ref[...] = seen_ref[...].astype(jnp.int32)


def _merge(oa, ob, gates, h, pa, pb, wo, g, b, wr, br):
    n = h.shape[0]
    tm = min(TM_MERGE, n)
    wr_hi = wr.astype(BF16)
    wr_lo = (wr - wr_hi.astype(F32)).astype(BF16)
    pad = lambda a: jnp.pad(a, ((0, 0), (0, LANES - a.shape[1]))).T
    row = lambda w: pl.BlockSpec((tm, w), lambda i: (i, 0))
    full = lambda r, c: pl.BlockSpec((r, c), lambda i: (0, 0))
    slot_row = pl.BlockSpec((TOP_K, tm), lambda i: (0, i))
    return pl.pallas_call(
        _merge_kernel,
        grid=(n // tm,),
        in_specs=[row(WIDTH_A), row(WIDTH_B), row(GATE_COLS), row(D_MODEL),
                  full(WIDTH_A, D_MODEL), full(WIDTH_B, D_MODEL), full(D_MODEL, D_MODEL),
                  full(1, D_MODEL), full(1, D_MODEL), full(LANES, D_MODEL), full(LANES, D_MODEL),
                  full(N_EXPERTS, 1)],
        out_specs=[row(D_MODEL), row(D_MODEL // 2), slot_row, slot_row, slot_row, full(N_EXPERTS, 1)],
        out_shape=[jax.ShapeDtypeStruct((n, D_MODEL), F32), jax.ShapeDtypeStruct((n, D_MODEL // 2), jnp.uint32),
                   jax.ShapeDtypeStruct((TOP_K, n), F32), jax.ShapeDtypeStruct((TOP_K, n), jnp.int32),
                   jax.ShapeDtypeStruct((TOP_K, n), jnp.int32), jax.ShapeDtypeStruct((N_EXPERTS, 1), jnp.int32)],
        scratch_shapes=[pltpu.VMEM((N_EXPERTS, 1), F32)],
        compiler_params=_params(("arbitrary",)),
        name="merge_ln1_router",
    )(oa, ob, gates, h, pa, pb, wo, g, b, pad(jnp.concatenate([wr_hi, wr_lo], axis=1)), pad(wr_hi),
      br.reshape(N_EXPERTS, 1))


def _dispatch_kernel(pend_ref, dest_ref, h1p_ref, xs_hbm, zero_ref, sem, zero_sem):
    @pl.when(pl.program_id(0) == 0)
    def _():
        zero_ref[...] = jnp.zeros_like(zero_ref)

        def fill(row0):
            block = pl.ds(pl.multiple_of(row0, ROWS_MOE), ROWS_MOE)
            return pltpu.make_async_copy(zero_ref, xs_hbm.at[block, :], zero_sem)
        for e in range(N_EXPERTS):
            fill(jnp.maximum(pend_ref[e] - ROWS_MOE, 0)).start()
        for e in range(N_EXPERTS):
            fill(0).wait()

        first_unused = pend_ref[N_EXPERTS - 1] // ROWS_MOE
        n_blocks = xs_hbm.shape[0] // ROWS_MOE

        @pl.loop(first_unused, n_blocks)
        def _(blk):
            fill(blk * ROWS_MOE).start()

        @pl.loop(first_unused, n_blocks)
        def _(blk):
            fill(0).wait()

    for t in range(TM_DISPATCH):
        for k in range(TOP_K):
            pltpu.make_async_copy(h1p_ref.at[t // SUBLANES, pl.ds(t % SUBLANES, 1), :],
                                  xs_hbm.at[pl.ds(dest_ref[0, 0, t * TOP_K + k], 1), :],
                                  sem).start(priority=k % 2)

    for k in range(TOP_K):
        pltpu.make_async_copy(h1p_ref, h1p_ref, sem).wait()


def _dispatch(dest, pend, h1p, n_rows):
    n = h1p.shape[0]
    tm = min(TM_DISPATCH, n)
    assert tm == TM_DISPATCH
    n_steps = n // tm
    grid_spec = pltpu.PrefetchScalarGridSpec(
        num_scalar_prefetch=1,
        grid=(n_steps,),
        in_specs=[
            pl.BlockSpec((1, 1, tm * TOP_K), lambda i, pe: (i, 0, 0), memory_space=pltpu.SMEM),
            pl.BlockSpec((tm // SUBLANES, SUBLANES, D_MODEL // 2), lambda i, pe: (i, 0, 0)),
        ],
        out_specs=pl.BlockSpec(memory_space=pl.ANY),
        scratch_shapes=[pltpu.VMEM((ROWS_MOE, D_MODEL // 2), jnp.uint32), pltpu.SemaphoreType.DMA(()),
                        pltpu.SemaphoreType.DMA(())],
    )
    return pl.pallas_call(
        _dispatch_kernel,
        grid_spec=grid_spec,
        out_shape=jax.ShapeDtypeStruct((n_rows, D_MODEL // 2), jnp.uint32),
        compiler_params=_params(("arbitrary",)),
        name="dispatch_rows",
    )(pend, dest.reshape(n_steps, 1, tm * TOP_K), h1p.reshape(n // SUBLANES, SUBLANES, D_MODEL // 2))


def _row_gather(src_hbm, idx_ref, dst_ref, sem, n_rows):
    def start(group, _):
        for j in range(SUBLANES):
            pltpu.make_async_copy(src_hbm.at[pl.ds(idx_ref[0, 0, group * SUBLANES + j], 1), :],
                                  dst_ref.at[group, pl.ds(j, 1), :], sem).start(priority=j % 2)
        return 0
    lax.fori_loop(0, n_rows // SUBLANES, start, 0)


def _moe_kernel(bexp_ref, nused_ref, nvalid_ref, x_ref, wi_ref, bi_ref, wo_ref, bo_ref, y_ref, wi_bf, wo_bf):
    i = pl.program_id(0)
    n_used = nused_ref[0]
    n_valid = nvalid_ref[i]
    new_expert = jnp.logical_or(i == 0, bexp_ref[i] != bexp_ref[jnp.maximum(i - 1, 0)])
    half = ROWS_MOE // 2

    @pl.when(jnp.logical_and(new_expert, i < n_used))
    def _():
        wi_bf[...] = wi_ref[...].astype(BF16)
        wo_bf[...] = wo_ref[...].astype(BF16)

    def ffn(rows):
        x = _unpack_bf16_pairs(x_ref[rows, :])
        hb = jnp.dot(x, wi_bf[...], preferred_element_type=F32) + bi_ref[...]
        g = jnp.minimum(hb[:, :D_FF], SWIGLU_LIMIT)
        u = jnp.clip(hb[:, D_FF:], -SWIGLU_LIMIT, SWIGLU_LIMIT)
        a = g * jax.nn.sigmoid(SWIGLU_ALPHA * g) * (u + 1.0)
        y_ref[rows, :] = jnp.dot(a.astype(BF16), wo_bf[...], preferred_element_type=F32) + bo_ref[...]

    @pl.when(jnp.logical_and(i < n_used, n_valid > half))
    def _():
        ffn(slice(0, ROWS_MOE))

    @pl.when(jnp.logical_and(i < n_used, n_valid <= half))
    def _():
        ffn(slice(0, half))
        y_ref[half:, :] = jnp.zeros((ROWS_MOE - half, D_MODEL), F32)

    @pl.when(i >= n_used)
    def _():
        y_ref[...] = jnp.zeros_like(y_ref)


def _moe(block_exp, n_used, n_valid, x_sorted, wi, bi, wo, bo):
    n_blocks = block_exp.shape[0]
    x_block = lambda i, be, nu, nv: (jnp.maximum(jnp.minimum(i, nu[0] - 1), 0), 0)
    expert = lambda i, be, nu, nv: (be[i], 0, 0)
    grid_spec = pltpu.PrefetchScalarGridSpec(
        num_scalar_prefetch=3,
        grid=(n_blocks,),
        in_specs=[
            pl.BlockSpec((ROWS_MOE, D_MODEL // 2), x_block),
            pl.BlockSpec((None, D_MODEL, 2 * D_FF), expert),
            pl.BlockSpec((None, 1, 2 * D_FF), expert),
            pl.BlockSpec((None, D_FF, D_MODEL), expert),
            pl.BlockSpec((None, 1, D_MODEL), expert),
        ],
        out_specs=pl.BlockSpec((ROWS_MOE, D_MODEL), lambda i, be, nu, nv: (i, 0)),
        scratch_shapes=[pltpu.VMEM((D_MODEL, 2 * D_FF), BF16), pltpu.VMEM((D_FF, D_MODEL), BF16)],
    )
    return pl.pallas_call(
        _moe_kernel,
        grid_spec=grid_spec,
        out_shape=jax.ShapeDtypeStruct((n_blocks * ROWS_MOE, D_MODEL), F32),
        compiler_params=_params(("arbitrary",)),
        name="moe_experts",
    )(block_exp, n_used, n_valid, x_sorted, wi, bi, wo, bo)


def _combine_kernel(dest_cur_ref, dest_nxt_ref, gw_ref, h1_ref, g_ref, b_ref, y_hbm, o_ref, ybuf0, ybuf1, sems):
    i = pl.program_id(0)
    n_steps = pl.num_programs(0)
    rows = TOP_K * TM_COMB
    tiles = TM_COMB // SUBLANES

    def wait(buf, sem):
        pltpu.make_async_copy(buf, buf, sem).wait()

    @pl.when(i == 0)
    def _():
        _row_gather(y_hbm, dest_cur_ref, ybuf0, sems.at[0], rows)

    def step(cur, cur_sem, nxt, nxt_sem):
        wait(cur, cur_sem)
        for r in range(rows):
            pltpu.make_async_copy(y_hbm.at[pl.ds(dest_nxt_ref[0, 0, r], 1), :],
                                  nxt.at[r // SUBLANES, pl.ds(r % SUBLANES, 1), :], nxt_sem).start(priority=r % 2)
        gw = gw_ref[...]
        f = jnp.zeros((TM_COMB, D_MODEL), F32)
        for k in range(TOP_K):
            f = f + gw[:, k:k + 1] * cur[k * tiles:(k + 1) * tiles].reshape(TM_COMB, D_MODEL)
        o_ref[...] = _layer_norm(DEEPNORM_ALPHA * h1_ref[...] + f, g_ref[...], b_ref[...])

        @pl.when(i == n_steps - 1)
        def _():
            wait(nxt, nxt_sem)

    @pl.when(i % 2 == 0)
    def _():
        step(ybuf0, sems.at[0], ybuf1, sems.at[1])

    @pl.when(i % 2 == 1)
    def _():
        step(ybuf1, sems.at[1], ybuf0, sems.at[0])


def _combine(dest_km, gate_w, h1, g, b, y_buf):
    n = h1.shape[0]
    n_steps = n // TM_COMB
    rows = TOP_K * TM_COMB
    return pl.pallas_call(
        _combine_kernel,
        grid=(n_steps,),
        in_specs=[
            pl.BlockSpec((1, 1, rows), lambda i: (i, 0, 0), memory_space=pltpu.SMEM),
            pl.BlockSpec((1, 1, rows), lambda i: (jnp.minimum(i + 1, n_steps - 1), 0, 0), memory_space=pltpu.SMEM),
            pl.BlockSpec((TM_COMB, TOP_K), lambda i: (i, 0)),
            pl.BlockSpec((TM_COMB, D_MODEL), lambda i: (i, 0)),
            pl.BlockSpec((1, D_MODEL), lambda i: (0, 0)),
            pl.BlockSpec((1, D_MODEL), lambda i: (0, 0)),
            pl.BlockSpec(memory_space=pl.ANY),
        ],
        out_specs=pl.BlockSpec((TM_COMB, D_MODEL), lambda i: (i, 0)),
        out_shape=jax.ShapeDtypeStruct((n, D_MODEL), F32),
        scratch_shapes=[pltpu.VMEM((rows // SUBLANES, SUBLANES, D_MODEL), F32),
                        pltpu.VMEM((rows // SUBLANES, SUBLANES, D_MODEL), F32), pltpu.SemaphoreType.DMA((2,))],
        compiler_params=_params(("arbitrary",)),
        name="combine_ln2",
    )(dest_km, dest_km, gate_w, h1, g, b, y_buf)


def _block_layout(top_idx, rank, counts):
    n_slots = top_idx.shape[1] * TOP_K
    padded = (counts + ROWS_MOE - 1) // ROWS_MOE * ROWS_MOE
    pend = jnp.cumsum(padded)
    pstart = pend - padded
    experts = jnp.arange(N_EXPERTS, dtype=jnp.int32)[:, None, None]
    dest = rank + jnp.sum(jnp.where(top_idx[None] == experts, pstart[:, None, None], 0), axis=0)
    n_blocks = n_slots // ROWS_MOE + N_EXPERTS
    block_start = jnp.arange(n_blocks, dtype=jnp.int32) * ROWS_MOE
    block_exp = jnp.minimum(jnp.sum(block_start[:, None] >= pend[None, :], axis=-1), N_EXPERTS - 1).astype(jnp.int32)
    n_used = (pend[-1] // ROWS_MOE).astype(jnp.int32).reshape(1)
    onehot_exp = block_exp[:, None] == jnp.arange(N_EXPERTS, dtype=jnp.int32)[None, :]
    seg_end = jnp.sum(jnp.where(onehot_exp, (pstart + counts)[None, :], 0), axis=-1)
    n_valid = jnp.clip(seg_end - block_start, 0, ROWS_MOE).astype(jnp.int32)
    return dest.astype(jnp.int32), pend.astype(jnp.int32), block_exp, n_used, n_valid


def kernel(x, ln_in_g, ln_in_b, w_in, b_gate, lambda_q1, lambda_k1, lambda_q2, lambda_k2, subln_w, rel_bias,
           w_branch_a, w_branch_b, w_out, ln1_g, ln1_b, w_router, b_router, w_exp_in, b_exp_in, w_exp_out,
           b_exp_out, ln2_g, ln2_b):
    batch, seq, d = x.shape
    n = batch * seq
    row = lambda a: a.reshape(1, -1).astype(F32)
    l = 0
    h, qkv, gates = _ln_proj(x.reshape(n, d), row(ln_in_g), row(ln_in_b), w_in[l].astype(BF16), row(b_gate[l]))
    slopes = jnp.asarray([2.0 ** (-8.0 * (i + 1) / N_HEADS_A) for i in range(N_HEADS_A)], F32)
    out_a = _diff_attention(qkv, slopes, row(lambda_q1[l]), row(lambda_k1[l]), row(lambda_q2[l]),
                            row(lambda_k2[l]), subln_w[l].reshape(-1, 1).astype(F32), batch, seq)
    out_b = _band_attention(qkv, _band_bias(rel_bias[l]), batch, seq)
    h1, h1p, gate_w, top_idx, rank, counts = _merge(
        out_a, out_b, gates, h, w_branch_a[l].astype(BF16), w_branch_b[l].astype(BF16), w_out[l].astype(BF16),
        row(ln1_g[l]), row(ln1_b[l]), w_router[l].astype(F32), row(b_router[l]))
    dest, pend, block_exp, n_used, n_valid = _block_layout(top_idx, rank, counts[:, 0])
    x_sorted = _dispatch(dest.T, pend, h1p, block_exp.shape[0] * ROWS_MOE)
    y_buf = _moe(block_exp, n_used, n_valid, x_sorted, w_exp_in[l].astype(F32), b_exp_in[l].reshape(N_EXPERTS, 1, -1),
                 w_exp_out[l].astype(F32), b_exp_out[l].reshape(N_EXPERTS, 1, -1))
    n_steps = n // TM_COMB
    dest_km = dest.reshape(TOP_K, n_steps, TM_COMB).transpose(1, 0, 2).reshape(n_steps, 1, TOP_K * TM_COMB)
    out = _combine(dest_km, gate_w.T, h1, row(ln2_g[l]), row(ln2_b[l]), y_buf)
    return out.reshape(batch, seq, d)
```

```python
import functools
import math

import jax
import jax.numpy as jnp
import numpy as np
from jax import lax
from jax.experimental import pallas as pl
from jax.experimental.pallas import tpu as pltpu

F32 = jnp.float32
BF16 = jnp.bfloat16

D_MODEL = 1024
CHUNK = 64
N_HEADS_A = 4
HEAD_DIM_A = 64
WIDTH_A = 512
N_HEADS_B = 8
HEAD_DIM_B = 64
WIDTH_B = 512
N_PREV_CHUNKS = 8
REL_CLIP = 128
N_EXPERTS = 32
TOP_K = 4
D_FF = 1024
SWIGLU_ALPHA = 1.702
SWIGLU_LIMIT = 7.0
DEEPNORM_ALPHA = 2.0 ** 0.25
LN_EPS = 1e-5
LAM_INIT = 0.8 - 0.6 * math.exp(-0.3 * 0)

LANES = 128
SUBLANES = 8
N_SLABS = (3 * WIDTH_A + 3 * WIDTH_B) // LANES
GATE_COLS = 2 * D_MODEL
IN_COLS = 3 * WIDTH_A + 3 * WIDTH_B + GATE_COLS
LOG2E = math.log2(math.e)
NEG = -1e30

TM_PROJ = 1024
PROJ_CHUNK = 512
TQ = 256
TK = 256
KV_GROUP = 2
HEADS_PER_STEP = 4
PAIRS_PER_STEP = 2
BAND = 3 * TK
TM_MERGE = 1024
TM_DISPATCH = 512
ROWS_MOE = 512
TM_COMB = 256
VMEM_LIMIT = 56 * 1024 * 1024


def _layer_norm(x, g, b):
    mu = jnp.mean(x, axis=-1, keepdims=True)
    xc = x - mu
    var = jnp.mean(xc * xc, axis=-1, keepdims=True)
    return xc * lax.rsqrt(var + LN_EPS) * g + b


def _params(sem):
    return pltpu.CompilerParams(dimension_semantics=sem, vmem_limit_bytes=VMEM_LIMIT)


def _ln_proj_kernel(x_ref, g_ref, b_ref, w_ref, bg_ref, h_ref, qkv_ref, gates_ref):
    h = _layer_norm(x_ref[...], g_ref[...], b_ref[...])
    h_ref[...] = h
    hb = h.astype(BF16)
    n_qkv_chunks = (N_SLABS * LANES) // PROJ_CHUNK
    slabs_per_chunk = PROJ_CHUNK // LANES
    q_scale = HEAD_DIM_A ** -0.5 * LOG2E
    for c in range(n_qkv_chunks):
        r = jnp.dot(hb, w_ref[:, c * PROJ_CHUNK:(c + 1) * PROJ_CHUNK], preferred_element_type=F32)
        first = c * slabs_per_chunk
        is_q = (first < WIDTH_A // LANES) or (3 * WIDTH_A // LANES <= first < (3 * WIDTH_A + WIDTH_B) // LANES)
        if is_q:
            r = r * q_scale
        for s in range(slabs_per_chunk):
            qkv_ref[first + s] = r[:, s * LANES:(s + 1) * LANES].astype(BF16)
    g0 = N_SLABS * LANES
    for c in range(GATE_COLS // PROJ_CHUNK):
        r = jnp.dot(hb, w_ref[:, g0 + c * PROJ_CHUNK:g0 + (c + 1) * PROJ_CHUNK], preferred_element_type=F32)
        r = r + bg_ref[:, c * PROJ_CHUNK:(c + 1) * PROJ_CHUNK]
        gates_ref[:, c * PROJ_CHUNK:(c + 1) * PROJ_CHUNK] = jax.nn.sigmoid(r).astype(BF16)


def _ln_proj(x2d, g, b, w_bf16, b_gate):
    n = x2d.shape[0]
    tm = min(TM_PROJ, n)
    const = lambda i: (0, 0)
    return pl.pallas_call(
        _ln_proj_kernel,
        grid=(n // tm,),
        in_specs=[
            pl.BlockSpec((tm, D_MODEL), lambda i: (i, 0)),
            pl.BlockSpec((1, D_MODEL), const),
            pl.BlockSpec((1, D_MODEL), const),
            pl.BlockSpec((D_MODEL, IN_COLS), const, pipeline_mode=pl.Buffered(1)),
            pl.BlockSpec((1, GATE_COLS), const),
        ],
        out_specs=[
            pl.BlockSpec((tm, D_MODEL), lambda i: (i, 0)),
            pl.BlockSpec((N_SLABS, tm, LANES), lambda i: (0, i, 0)),
            pl.BlockSpec((tm, GATE_COLS), lambda i: (i, 0)),
        ],
        out_shape=[
            jax.ShapeDtypeStruct((n, D_MODEL), F32),
            jax.ShapeDtypeStruct((N_SLABS, n, LANES), BF16),
            jax.ShapeDtypeStruct((n, GATE_COLS), BF16),
        ],
        compiler_params=_params(("parallel",)),
        name="ln_proj",
    )(x2d, g, b, w_bf16, b_gate)


def _stack_halves(q):
    lane = lax.broadcasted_iota(jnp.int32, q.shape, 1)
    zero = jnp.zeros_like(q)
    return jnp.concatenate([jnp.where(lane < 64, q, zero), jnp.where(lane >= 64, q, zero)], axis=0)


def _dot_nt(a, b):
    return lax.dot_general(a, b, (((1,), (1,)), ((), ())), preferred_element_type=F32)


def _split3_bf16(x):
    hi = x.astype(BF16)
    r1 = x - hi.astype(F32)
    mid = r1.astype(BF16)
    lo = (r1 - mid.astype(F32)).astype(BF16)
    return hi, mid, lo


class _DiffHead:
    def __init__(self, slope, i, q_ref, k_ref, v_ref, s_refs, mx_refs, m_ref, l_ref, acc_ref):
        self.slope, self.i = slope, i
        self.k_ref, self.v_ref, self.s_refs, self.mx_refs = k_ref, v_ref, s_refs, mx_refs
        self.m_ref, self.l_ref, self.acc_ref = m_ref, l_ref, acc_ref
        lane_q = lax.broadcasted_iota(jnp.int32, (2 * TQ, LANES), 1)
        self.q_aug = jnp.concatenate([_stack_halves(q_ref[...]), jnp.where(lane_q < 3, 1.0, 0.0).astype(BF16)],
                                     axis=1)
        lane_k = lax.broadcasted_iota(jnp.int32, (TK, LANES), 1)
        key_pos = lax.broadcasted_iota(jnp.int32, (TK, LANES), 0)
        self.k_bias = []
        for g in range(KV_GROUP):
            hi, mid, lo = [t.astype(F32) for t in _split3_bf16(slope * (key_pos + g * TK).astype(F32))]
            self.k_bias.append(jnp.where(lane_k == 0, hi, jnp.where(lane_k == 1, mid, jnp.where(lane_k == 2, lo, 0.0))
                                         ).astype(BF16))
        m_ref[...] = jnp.full(m_ref.shape, NEG, F32)
        l_ref[...] = jnp.zeros(l_ref.shape, F32)
        acc_ref[...] = jnp.zeros(acc_ref.shape, F32)

    @staticmethod
    def _group_rows(a):
        return pl.ds(pl.multiple_of(a * (KV_GROUP * TK), KV_GROUP * TK), KV_GROUP * TK)

    def scores_into(self, buf, a):
        mx = None
        for g in range(KV_GROUP):
            rows = pl.ds(pl.multiple_of((a * KV_GROUP + g) * TK, TK), TK)
            s = _dot_nt(jnp.concatenate([self.k_ref[rows, :], self.k_bias[g]], axis=1), self.q_aug)
            self.s_refs[buf][g * TK:(g + 1) * TK, :] = s
            mg = jnp.max(s, axis=0, keepdims=True)
            mx = mg if mx is None else jnp.maximum(mx, mg)
        self.mx_refs[buf][...] = mx

    def _group_offset(self, a):
        return self.slope * ((a * KV_GROUP - self.i) * TK).astype(F32)

    def _update(self, a, s, mx):
        off = self._group_offset(a)
        m_prev = self.m_ref[...]
        m_new = jnp.maximum(m_prev, mx + off)
        alpha = jnp.exp2(m_prev - m_new)
        p = jnp.exp2(s - (m_new - off))
        self.m_ref[...] = m_new
        self.l_ref[...] = alpha * self.l_ref[...] + jnp.sum(p, axis=0, keepdims=True)
        self.acc_ref[...] = alpha * self.acc_ref[...] + lax.dot_general(
            self.v_ref[self._group_rows(a), :], p.astype(BF16), (((0,), (0,)), ((), ())),
            preferred_element_type=F32)

    def full_update(self, buf, a):
        self._update(a, self.s_refs[buf][...], self.mx_refs[buf][...])

    def last_update(self, buf, a):
        kk = lax.broadcasted_iota(jnp.int32, (TK, TQ), 0)
        qq = lax.broadcasted_iota(jnp.int32, (TK, TQ), 1)
        diag = jnp.where(kk // CHUNK <= qq // CHUNK, self.slope * (qq - jnp.abs(qq - kk) - kk).astype(F32), NEG)
        diag = jnp.concatenate([diag, diag], axis=1)
        off = self._group_offset(a)
        scores, offsets = [], []
        for g in range(KV_GROUP):
            jb = a * KV_GROUP + g
            scores.append(self.s_refs[buf][g * TK:(g + 1) * TK, :] + jnp.where(jb == self.i, diag, 0.0))
            offsets.append(jnp.where(jb > self.i, NEG, off))
        m_prev = self.m_ref[...]
        m_new = m_prev
        for s, o in zip(scores, offsets):
            m_new = jnp.maximum(m_new, jnp.max(s, axis=0, keepdims=True) + o)
        alpha = jnp.exp2(m_prev - m_new)
        p = jnp.concatenate([jnp.exp2(s - (m_new - o)) for s, o in zip(scores, offsets)], axis=0)
        self.m_ref[...] = m_new
        self.l_ref[...] = alpha * self.l_ref[...] + jnp.sum(p, axis=0, keepdims=True)
        self.acc_ref[...] = alpha * self.acc_ref[...] + lax.dot_general(
            self.v_ref[self._group_rows(a), :], p.astype(BF16), (((0,), (0,)), ((), ())),
            preferred_element_type=F32)

    def output(self, lam, subln):
        o = self.acc_ref[...] / self.l_ref[...]
        o = o[:, :TQ] - lam * o[:, TQ:]
        o = o * lax.rsqrt(jnp.mean(o * o, axis=0, keepdims=True) + LN_EPS) * subln
        return (o * (1.0 - LAM_INIT)).T.astype(BF16)


def _diff_attn_kernel(slopes_ref, lq1_ref, lk1_ref, lq2_ref, lk2_ref, subln_ref, *refs):
    hp = pl.program_id(1)
    i = pl.program_id(2)
    nh = HEADS_PER_STEP
    q_refs, k_refs, v_refs = refs[:nh], refs[nh:2 * nh], refs[2 * nh:3 * nh]
    o_ref = refs[3 * nh]
    scratch = refs[3 * nh + 1:]
    heads = []
    for t in range(nh):
        s0, s1, mx0, mx1, m, l, acc = scratch[7 * t:7 * t + 7]
        heads.append(_DiffHead(slopes_ref[hp * nh + t] * LOG2E, i, q_refs[t], k_refs[t], v_refs[t],
                               (s0, s1), (mx0, mx1), m, l, acc))

    n_full = i // KV_GROUP
    odd = n_full % 2

    @pl.when(odd == 1)
    def _():
        for h in heads:
            h.scores_into(1, 0)
        for h in heads:
            h.scores_into(0, 1)
        for h in heads:
            h.full_update(1, 0)

    @pl.when(odd == 0)
    def _():
        for h in heads:
            h.scores_into(0, 0)

    def pair(b, _):
        a = odd + 2 * b
        for h in heads:
            h.scores_into(1, a + 1)
        for h in heads:
            h.full_update(0, a)
        for h in heads:
            h.scores_into(0, a + 2)
        for h in heads:
            h.full_update(1, a + 1)
        return 0
    lax.fori_loop(0, (n_full - odd) // 2, pair, 0)

    for h in heads:
        h.last_update(0, n_full)

    lam = (jnp.exp(jnp.sum(lq1_ref[...] * lk1_ref[...], axis=-1, keepdims=True))
           - jnp.exp(jnp.sum(lq2_ref[...] * lk2_ref[...], axis=-1, keepdims=True)) + LAM_INIT)
    for t, h in enumerate(heads):
        o_ref[:, t * LANES:(t + 1) * LANES] = h.output(lam, subln_ref[...])


def _diff_attention(qkv, slopes, lq1, lk1, lq2, lk2, subln, batch, seq):
    nq = seq // TQ
    assert seq % (TQ * KV_GROUP) == 0, "the last key group of a query block must stay inside the sequence"
    n = batch * seq
    nh = HEADS_PER_STEP
    vec = lambda w: pl.BlockSpec((1, w), lambda b, h, i: (0, 0))
    q_spec = lambda t: pl.BlockSpec((None, TQ, LANES), lambda b, h, i: (h * nh + t, b * nq + i, 0))
    kv_spec = lambda t, base: pl.BlockSpec((None, seq, LANES), lambda b, h, i: (base + h * nh + t, b, 0))
    stat = pltpu.VMEM((1, 2 * TQ), F32)
    head_scratch = [pltpu.VMEM((KV_GROUP * TK, 2 * TQ), F32), pltpu.VMEM((KV_GROUP * TK, 2 * TQ), F32),
                    stat, stat, stat, stat, pltpu.VMEM((LANES, 2 * TQ), F32)]
    return pl.pallas_call(
        _diff_attn_kernel,
        grid=(batch, N_HEADS_A // nh, nq),
        in_specs=[
            pl.BlockSpec(memory_space=pltpu.SMEM),
            vec(HEAD_DIM_A), vec(HEAD_DIM_A), vec(HEAD_DIM_A), vec(HEAD_DIM_A),
            pl.BlockSpec((2 * HEAD_DIM_A, 1), lambda b, h, i: (0, 0)),
            *[q_spec(t) for t in range(nh)],
            *[kv_spec(t, N_HEADS_A) for t in range(nh)],
            *[kv_spec(t, 2 * N_HEADS_A) for t in range(nh)],
        ],
        out_specs=pl.BlockSpec((TQ, nh * LANES), lambda b, h, i: (b * nq + i, h)),
        out_shape=jax.ShapeDtypeStruct((n, WIDTH_A), BF16),
        scratch_shapes=head_scratch * nh,
        compiler_params=_params(("parallel", "parallel", "arbitrary")),
        name="diff_attn",
    )(slopes, lq1, lk1, lq2, lk2, subln, *([qkv] * (3 * nh)))


def _band_attn_kernel(*refs):
    npair = PAIRS_PER_STEP
    bias_refs, q_refs = refs[:npair], refs[npair:2 * npair]
    k_refs, v_refs = refs[2 * npair:3 * npair], refs[3 * npair:4 * npair]
    o_ref = refs[4 * npair]
    s_refs = refs[4 * npair + 1:]
    n_sub = BAND // TK
    nq = o_ref.shape[0] // TQ
    feat = lax.broadcasted_iota(jnp.int32, (LANES, TQ), 0)

    def key_rows(i, jj):
        start = i * TQ - N_PREV_CHUNKS * CHUNK + jj * TK
        return start, pl.ds(pl.multiple_of(jnp.maximum(start, 0), TK), TK)

    def scores_into(buf, i):
        for t in range(npair):
            q2 = _stack_halves(q_refs[t][pl.ds(pl.multiple_of(i * TQ, TQ), TQ), :])
            for jj in range(n_sub):
                _, rows = key_rows(i, jj)
                s_refs[2 * t + buf][jj * TK:(jj + 1) * TK, :] = (
                    _dot_nt(k_refs[t][rows, :], q2) + bias_refs[t][jj * TK:(jj + 1) * TK, :])

    def softmax_out(buf, i):
        starts_rows = [key_rows(i, jj) for jj in range(n_sub)]
        offsets = [jnp.where(start < 0, NEG, 0.0).astype(F32) for start, _ in starts_rows]
        for t in range(npair):
            scores = [s_refs[2 * t + buf][jj * TK:(jj + 1) * TK, :] for jj in range(n_sub)]
            m = functools.reduce(jnp.maximum, [jnp.max(s, axis=0, keepdims=True) + off
                                               for s, off in zip(scores, offsets)])
            l = jnp.zeros((1, 2 * TQ), F32)
            acc = jnp.zeros((LANES, 2 * TQ), F32)
            for jj in range(n_sub):
                p = jnp.exp2(scores[jj] - (m - offsets[jj]))
                l = l + jnp.sum(p, axis=0, keepdims=True)
                acc = acc + lax.dot_general(v_refs[t][starts_rows[jj][1], :], p.astype(BF16),
                                            (((0,), (0,)), ((), ())), preferred_element_type=F32)
            o = acc / l
            o_ref[pl.ds(pl.multiple_of(i * TQ, TQ), TQ), t * LANES:(t + 1) * LANES] = (
                jnp.where(feat < 64, o[:, :TQ], o[:, TQ:]).T.astype(BF16))

    scores_into(0, 0)

    def two_blocks(b, _):
        i = 2 * b
        scores_into(1, i + 1)
        softmax_out(0, i)
        scores_into(0, jnp.minimum(i + 2, nq - 1))
        softmax_out(1, i + 1)
        return 0
    lax.fori_loop(0, nq // 2, two_blocks, 0)


BIAS_SPAN = 1024


def _band_bias_kernel(line_ref, o_ref):
    kj = lax.broadcasted_iota(jnp.int32, (BAND, TQ), 0)
    qi = lax.broadcasted_iota(jnp.int32, (BAND, TQ), 1)
    kc = kj // CHUNK - N_PREV_CHUNKS
    qc = qi // CHUNK
    allowed = jnp.logical_and(kc <= qc, kc >= qc - N_PREV_CHUNKS)
    for hh in range(2):
        line = jnp.broadcast_to(line_ref[hh:hh + 1, :], (BAND, BIAS_SPAN))
        rolled = pltpu.roll(line, BIAS_SPAN - (BAND - 1), 1, stride=1, stride_axis=0)
        o_ref[:, hh * TQ:(hh + 1) * TQ] = jnp.where(allowed, LOG2E * rolled[:, :TQ], NEG)


def _band_bias(rel_bias):
    rb = rel_bias.astype(F32)
    n_low = (BAND - 1) - REL_CLIP
    n_high = (BAND + TQ - 1) - (BAND - 1) - REL_CLIP - 1
    line = jnp.concatenate([jnp.broadcast_to(rb[:, :1], (N_HEADS_B, n_low)), rb,
                            jnp.broadcast_to(rb[:, -1:], (N_HEADS_B, n_high))], axis=1)
    line = jnp.pad(line[:, ::-1], ((0, 0), (0, BIAS_SPAN - line.shape[1])))
    pairs = N_HEADS_B // 2
    return pl.pallas_call(
        _band_bias_kernel,
        grid=(pairs,),
        in_specs=[pl.BlockSpec((None, 2, BIAS_SPAN), lambda p: (p, 0, 0))],
        out_specs=pl.BlockSpec((None, BAND, 2 * TQ), lambda p: (p, 0, 0)),
        out_shape=jax.ShapeDtypeStruct((pairs, BAND, 2 * TQ), F32),
        compiler_params=_params(("parallel",)),
        name="band_bias",
    )(line.reshape(pairs, 2, BIAS_SPAN))


def _band_attention(qkv, bias, batch, seq):
    assert (seq // TQ) % 2 == 0, "query blocks are walked two at a time"
    n = batch * seq
    base = 3 * WIDTH_A // LANES
    pairs = N_HEADS_B // 2
    npair = PAIRS_PER_STEP
    bias_spec = lambda t: pl.BlockSpec((None, BAND, 2 * TQ), lambda b, p: (p * npair + t, 0, 0))
    slab_spec = lambda t, off: pl.BlockSpec((None, seq, LANES), lambda b, p: (base + off + p * npair + t, b, 0))
    return pl.pallas_call(
        _band_attn_kernel,
        grid=(batch, pairs // npair),
        in_specs=[
            *[bias_spec(t) for t in range(npair)],
            *[slab_spec(t, 0) for t in range(npair)],
            *[slab_spec(t, pairs) for t in range(npair)],
            *[slab_spec(t, 2 * pairs) for t in range(npair)],
        ],
        out_specs=pl.BlockSpec((seq, npair * LANES), lambda b, p: (b, p)),
        out_shape=jax.ShapeDtypeStruct((n, WIDTH_B), BF16),
        scratch_shapes=[pltpu.VMEM((BAND, 2 * TQ), F32)] * (2 * npair),
        compiler_params=_params(("parallel", "parallel")),
        name="band_attn",
    )(*([bias] * npair), *([qkv] * (3 * npair)))


def _pack_bf16_pairs(x):
    w = x.shape[1] // 2
    bits = pltpu.bitcast(x.astype(BF16).astype(F32), jnp.uint32)
    return (bits[:, :w] >> 16) | (bits[:, w:] & jnp.uint32(0xFFFF0000))


def _unpack_bf16_pairs(p):
    lo = pltpu.bitcast(p << 16, F32)
    hi = pltpu.bitcast(p & jnp.uint32(0xFFFF0000), F32)
    return jnp.concatenate([lo, hi], axis=1).astype(BF16)


def _merge_kernel(oa_ref, ob_ref, gates_ref, h_ref, pa_ref, pb_ref, wo_ref, g_ref, b_ref, wra_ref, wrb_ref, br_ref,
                  h1_ref, h1p_ref, gw_ref, idx_ref, rank_ref, counts_ref, seen_ref):
    ma = jnp.dot(oa_ref[...], pa_ref[...], preferred_element_type=F32)
    mb = jnp.dot(ob_ref[...], pb_ref[...], preferred_element_type=F32)
    merged = gates_ref[:, :D_MODEL].astype(F32) * ma + gates_ref[:, D_MODEL:].astype(F32) * mb
    m = jnp.dot(merged.astype(BF16), wo_ref[...], preferred_element_type=F32)
    h1 = _layer_norm(DEEPNORM_ALPHA * h_ref[...] + m, g_ref[...], b_ref[...])
    h1_ref[...] = h1
    h1p_ref[...] = _pack_bf16_pairs(h1)

    h_hi = h1.astype(BF16)
    h_lo = (h1 - h_hi.astype(F32)).astype(BF16)
    part_a = _dot_nt(wra_ref[...], h_hi)
    part_b = _dot_nt(wrb_ref[...], h_lo)
    logits = part_a[:N_EXPERTS] + part_a[N_EXPERTS:2 * N_EXPERTS] + part_b[:N_EXPERTS] + br_ref[...]

    tm = logits.shape[1]
    expert = lax.broadcasted_iota(jnp.int32, (N_EXPERTS, tm), 0)
    vals = logits
    top_val, top_idx = [], []
    for _ in range(TOP_K):
        mx = jnp.max(vals, axis=0, keepdims=True)
        sel = jnp.min(jnp.where(vals == mx, expert, N_EXPERTS), axis=0, keepdims=True)
        top_val.append(mx)
        top_idx.append(sel)
        vals = jnp.where(expert == sel, -jnp.inf, vals)
    ex = [jnp.exp(v - top_val[0]) for v in top_val]
    denom = functools.reduce(jnp.add, ex)
    gw_ref[...] = jnp.concatenate([e / denom for e in ex], axis=0)
    idx_ref[...] = jnp.concatenate(top_idx, axis=0)

    @pl.when(pl.program_id(0) == 0)
    def _():
        seen_ref[...] = jnp.zeros_like(seen_ref)

    chosen = functools.reduce(jnp.logical_or, [expert == s for s in top_idx])
    onehot = jnp.where(chosen, 1.0, 0.0)
    r_i = lax.broadcasted_iota(jnp.int32, (tm, tm), 0)
    c_i = lax.broadcasted_iota(jnp.int32, (tm, tm), 1)
    earlier = jnp.where(r_i < c_i, 1.0, 0.0)
    before = jnp.dot(onehot, earlier, preferred_element_type=F32) + seen_ref[...]
    rank_ref[...] = jnp.concatenate([jnp.sum(jnp.where(expert == s, before, 0.0), axis=0, keepdims=True)
                                     for s in top_idx], axis=0).astype(jnp.int32)
    seen_ref[...] = seen_ref[...] + jnp.sum(onehot, axis=1, keepdims=True)
    counts_ref[...] = seen_ref[...].astype(jnp.int32)


def _merge(oa, ob, gates, h, pa, pb, wo, g, b, wr, br):
    n = h.shape[0]
    tm = min(TM_MERGE, n)
    wr_hi = wr.astype(BF16)
    wr_lo = (wr - wr_hi.astype(F32)).astype(BF16)
    pad = lambda a: jnp.pad(a, ((0, 0), (0, LANES - a.shape[1]))).T
    row = lambda w: pl.BlockSpec((tm, w), lambda i: (i, 0))
    full = lambda r, c: pl.BlockSpec((r, c), lambda i: (0, 0))
    slot_row = pl.BlockSpec((TOP_K, tm), lambda i: (0, i))
    return pl.pallas_call(
        _merge_kernel,
        grid=(n // tm,),
        in_specs=[row(WIDTH_A), row(WIDTH_B), row(GATE_COLS), row(D_MODEL),
                  full(WIDTH_A, D_MODEL), full(WIDTH_B, D_MODEL), full(D_MODEL, D_MODEL),
                  full(1, D_MODEL), full(1, D_MODEL), full(LANES, D_MODEL), full(LANES, D_MODEL),
                  full(N_EXPERTS, 1)],
        out_specs=[row(D_MODEL), row(D_MODEL // 2), slot_row, slot_row, slot_row, full(N_EXPERTS, 1)],
        out_shape=[jax.ShapeDtypeStruct((n, D_MODEL), F32), jax.ShapeDtypeStruct((n, D_MODEL // 2), jnp.uint32),
                   jax.ShapeDtypeStruct((TOP_K, n), F32), jax.ShapeDtypeStruct((TOP_K, n), jnp.int32),
                   jax.ShapeDtypeStruct((TOP_K, n), jnp.int32), jax.ShapeDtypeStruct((N_EXPERTS, 1), jnp.int32)],
        scratch_shapes=[pltpu.VMEM((N_EXPERTS, 1), F32)],
        compiler_params=_params(("arbitrary",)),
        name="merge_ln1_router",
    )(oa, ob, gates, h, pa, pb, wo, g, b, pad(jnp.concatenate([wr_hi, wr_lo], axis=1)), pad(wr_hi),
      br.reshape(N_EXPERTS, 1))


def _dispatch_kernel(pend_ref, dest_ref, h1p_ref, xs_hbm, zero_ref, sem, zero_sem):
    @pl.when(pl.program_id(0) == 0)
    def _():
        zero_ref[...] = jnp.zeros_like(zero_ref)

        def fill(row0):
            block = pl.ds(pl.multiple_of(row0, ROWS_MOE), ROWS_MOE)
            return pltpu.make_async_copy(zero_ref, xs_hbm.at[block, :], zero_sem)
        for e in range(N_EXPERTS):
            fill(jnp.maximum(pend_ref[e] - ROWS_MOE, 0)).start()
        for e in range(N_EXPERTS):
            fill(0).wait()

        first_unused = pend_ref[N_EXPERTS - 1] // ROWS_MOE
        n_blocks = xs_hbm.shape[0] // ROWS_MOE

        @pl.loop(first_unused, n_blocks)
        def _(blk):
            fill(blk * ROWS_MOE).start()

        @pl.loop(first_unused, n_blocks)
        def _(blk):
            fill(0).wait()

    for t in range(TM_DISPATCH):
        for k in range(TOP_K):
            pltpu.make_async_copy(h1p_ref.at[t // SUBLANES, pl.ds(t % SUBLANES, 1), :],
                                  xs_hbm.at[pl.ds(dest_ref[0, 0, t * TOP_K + k], 1), :],
                                  sem).start(priority=k % 2)

    for k in range(TOP_K):
        pltpu.make_async_copy(h1p_ref, h1p_ref, sem).wait()


def _dispatch(dest, pend, h1p, n_rows):
    n = h1p.shape[0]
    tm = min(TM_DISPATCH, n)
    assert tm == TM_DISPATCH
    n_steps = n // tm
    grid_spec = pltpu.PrefetchScalarGridSpec(
        num_scalar_prefetch=1,
        grid=(n_steps,),
        in_specs=[
            pl.BlockSpec((1, 1, tm * TOP_K), lambda i, pe: (i, 0, 0), memory_space=pltpu.SMEM),
            pl.BlockSpec((tm // SUBLANES, SUBLANES, D_MODEL // 2), lambda i, pe: (i, 0, 0)),
        ],
        out_specs=pl.BlockSpec(memory_space=pl.ANY),
        scratch_shapes=[pltpu.VMEM((ROWS_MOE, D_MODEL // 2), jnp.uint32), pltpu.SemaphoreType.DMA(()),
                        pltpu.SemaphoreType.DMA(())],
    )
    return pl.pallas_call(
        _dispatch_kernel,
        grid_spec=grid_spec,
        out_shape=jax.ShapeDtypeStruct((n_rows, D_MODEL // 2), jnp.uint32),
        compiler_params=_params(("arbitrary",)),
        name="dispatch_rows",
    )(pend, dest.reshape(n_steps, 1, tm * TOP_K), h1p.reshape(n // SUBLANES, SUBLANES, D_MODEL // 2))


def _row_gather(src_hbm, idx_ref, dst_ref, sem, n_rows):
    def start(group, _):
        for j in range(SUBLANES):
            pltpu.make_async_copy(src_hbm.at[pl.ds(idx_ref[0, 0, group * SUBLANES + j], 1), :],
                                  dst_ref.at[group, pl.ds(j, 1), :], sem).start(priority=j % 2)
        return 0
    lax.fori_loop(0, n_rows // SUBLANES, start, 0)


def _moe_kernel(bexp_ref, nused_ref, nvalid_ref, x_ref, wi_ref, bi_ref, wo_ref, bo_ref, y_ref, wi_bf, wo_bf):
    i = pl.program_id(0)
    n_used = nused_ref[0]
    n_valid = nvalid_ref[i]
    new_expert = jnp.logical_or(i == 0, bexp_ref[i] != bexp_ref[jnp.maximum(i - 1, 0)])
    half = ROWS_MOE // 2

    @pl.when(jnp.logical_and(new_expert, i < n_used))
    def _():
        wi_bf[...] = wi_ref[...].astype(BF16)
        wo_bf[...] = wo_ref[...].astype(BF16)

    def ffn(rows):
        x = _unpack_bf16_pairs(x_ref[rows, :])
        hb = jnp.dot(x, wi_bf[...], preferred_element_type=F32) + bi_ref[...]
        g = jnp.minimum(hb[:, :D_FF], SWIGLU_LIMIT)
        u = jnp.clip(hb[:, D_FF:], -SWIGLU_LIMIT, SWIGLU_LIMIT)
        a = g * jax.nn.sigmoid(SWIGLU_ALPHA * g) * (u + 1.0)
        y_ref[rows, :] = jnp.dot(a.astype(BF16), wo_bf[...], preferred_element_type=F32) + bo_ref[...]

    @pl.when(jnp.logical_and(i < n_used, n_valid > half))
    def _():
        ffn(slice(0, ROWS_MOE))

    @pl.when(jnp.logical_and(i < n_used, n_valid <= half))
    def _():
        ffn(slice(0, half))
        y_ref[half:, :] = jnp.zeros((ROWS_MOE - half, D_MODEL), F32)

    @pl.when(i >= n_used)
    def _():
        y_ref[...] = jnp.zeros_like(y_ref)


def _moe(block_exp, n_used, n_valid, x_sorted, wi, bi, wo, bo):
    n_blocks = block_exp.shape[0]
    x_block = lambda i, be, nu, nv: (jnp.maximum(jnp.minimum(i, nu[0] - 1), 0), 0)
    expert = lambda i, be, nu, nv: (be[i], 0, 0)
    grid_spec = pltpu.PrefetchScalarGridSpec(
        num_scalar_prefetch=3,
        grid=(n_blocks,),
        in_specs=[
            pl.BlockSpec((ROWS_MOE, D_MODEL // 2), x_block),
            pl.BlockSpec((None, D_MODEL, 2 * D_FF), expert),
            pl.BlockSpec((None, 1, 2 * D_FF), expert),
            pl.BlockSpec((None, D_FF, D_MODEL), expert),
            pl.BlockSpec((None, 1, D_MODEL), expert),
        ],
        out_specs=pl.BlockSpec((ROWS_MOE, D_MODEL), lambda i, be, nu, nv: (i, 0)),
        scratch_shapes=[pltpu.VMEM((D_MODEL, 2 * D_FF), BF16), pltpu.VMEM((D_FF, D_MODEL), BF16)],
    )
    return pl.pallas_call(
        _moe_kernel,
        grid_spec=grid_spec,
        out_shape=jax.ShapeDtypeStruct((n_blocks * ROWS_MOE, D_MODEL), F32),
        compiler_params=_params(("arbitrary",)),
        name="moe_experts",
    )(block_exp, n_used, n_valid, x_sorted, wi, bi, wo, bo)


def _combine_kernel(dest_cur_ref, dest_nxt_ref, gw_ref, h1_ref, g_ref, b_ref, y_hbm, o_ref, ybuf0, ybuf1, sems):
    i = pl.program_id(0)
    n_steps = pl.num_programs(0)
    rows = TOP_K * TM_COMB
    tiles = TM_COMB // SUBLANES

    def wait(buf, sem):
        pltpu.make_async_copy(buf, buf, sem).wait()

    @pl.when(i == 0)
    def _():
        _row_gather(y_hbm, dest_cur_ref, ybuf0, sems.at[0], rows)

    def step(cur, cur_sem, nxt, nxt_sem):
        def issue(r):
            pltpu.make_async_copy(y_hbm.at[pl.ds(dest_nxt_ref[0, 0, r], 1), :],
                                  nxt.at[r // SUBLANES, pl.ds(r % SUBLANES, 1), :], nxt_sem).start(priority=r % 2)
        for r in range(rows // 2):
            issue(r)
        wait(cur, cur_sem)
        for r in range(rows // 2, rows):
            issue(r)
        gw = gw_ref[...]
        f = jnp.zeros((TM_COMB, D_MODEL), F32)
        for k in range(TOP_K):
            f = f + gw[:, k:k + 1] * cur[k * tiles:(k + 1) * tiles].reshape(TM_COMB, D_MODEL)
        o_ref[...] = _layer_norm(DEEPNORM_ALPHA * h1_ref[...] + f, g_ref[...], b_ref[...])

        @pl.when(i == n_steps - 1)
        def _():
            wait(nxt, nxt_sem)

    @pl.when(i % 2 == 0)
    def _():
        step(ybuf0, sems.at[0], ybuf1, sems.at[1])

    @pl.when(i % 2 == 1)
    def _():
        step(ybuf1, sems.at[1], ybuf0, sems.at[0])


def _combine(dest_km, gate_w, h1, g, b, y_buf):
    n = h1.shape[0]
    n_steps = n // TM_COMB
    rows = TOP_K * TM_COMB
    return pl.pallas_call(
        _combine_kernel,
        grid=(n_steps,),
        in_specs=[
            pl.BlockSpec((1, 1, rows), lambda i: (i, 0, 0), memory_space=pltpu.SMEM),
            pl.BlockSpec((1, 1, rows), lambda i: (jnp.minimum(i + 1, n_steps - 1), 0, 0), memory_space=pltpu.SMEM),
            pl.BlockSpec((TM_COMB, TOP_K), lambda i: (i, 0)),
            pl.BlockSpec((TM_COMB, D_MODEL), lambda i: (i, 0)),
            pl.BlockSpec((1, D_MODEL), lambda i: (0, 0)),
            pl.BlockSpec((1, D_MODEL), lambda i: (0, 0)),
            pl.BlockSpec(memory_space=pl.ANY),
        ],
        out_specs=pl.BlockSpec((TM_COMB, D_MODEL), lambda i: (i, 0)),
        out_shape=jax.ShapeDtypeStruct((n, D_MODEL), F32),
        scratch_shapes=[pltpu.VMEM((rows // SUBLANES, SUBLANES, D_MODEL), F32),
                        pltpu.VMEM((rows // SUBLANES, SUBLANES, D_MODEL), F32), pltpu.SemaphoreType.DMA((2,))],
        compiler_params=_params(("arbitrary",)),
        name="combine_ln2",
    )(dest_km, dest_km, gate_w, h1, g, b, y_buf)


def _block_layout(top_idx, rank, counts):
    n_slots = top_idx.shape[1] * TOP_K
    padded = (counts + ROWS_MOE - 1) // ROWS_MOE * ROWS_MOE
    pend = jnp.cumsum(padded)
    pstart = pend - padded
    experts = jnp.arange(N_EXPERTS, dtype=jnp.int32)[:, None, None]
    dest = rank + jnp.sum(jnp.where(top_idx[None] == experts, pstart[:, None, None], 0), axis=0)
    n_blocks = n_slots // ROWS_MOE + N_EXPERTS
    block_start = jnp.arange(n_blocks, dtype=jnp.int32) * ROWS_MOE
    block_exp = jnp.minimum(jnp.sum(block_start[:, None] >= pend[None, :], axis=-1), N_EXPERTS - 1).astype(jnp.int32)
    n_used = (pend[-1] // ROWS_MOE).astype(jnp.int32).reshape(1)
    onehot_exp = block_exp[:, None] == jnp.arange(N_EXPERTS, dtype=jnp.int32)[None, :]
    seg_end = jnp.sum(jnp.where(onehot_exp, (pstart + counts)[None, :], 0), axis=-1)
    n_valid = jnp.clip(seg_end - block_start, 0, ROWS_MOE).astype(jnp.int32)
    return dest.astype(jnp.int32), pend.astype(jnp.int32), block_exp, n_used, n_valid


def kernel(x, ln_in_g, ln_in_b, w_in, b_gate, lambda_q1, lambda_k1, lambda_q2, lambda_k2, subln_w, rel_bias,
           w_branch_a, w_branch_b, w_out, ln1_g, ln1_b, w_router, b_router, w_exp_in, b_exp_in, w_exp_out,
           b_exp_out, ln2_g, ln2_b):
    batch, seq, d = x.shape
    n = batch * seq
    row = lambda a: a.reshape(1, -1).astype(F32)
    l = 0
    h, qkv, gates = _ln_proj(x.reshape(n, d), row(ln_in_g), row(ln_in_b), w_in[l].astype(BF16), row(b_gate[l]))
    slopes = jnp.asarray([2.0 ** (-8.0 * (i + 1) / N_HEADS_A) for i in range(N_HEADS_A)], F32)
    out_a = _diff_attention(qkv, slopes, row(lambda_q1[l]), row(lambda_k1[l]), row(lambda_q2[l]),
                            row(lambda_k2[l]), subln_w[l].reshape(-1, 1).astype(F32), batch, seq)
    out_b = _band_attention(qkv, _band_bias(rel_bias[l]), batch, seq)
    h1, h1p, gate_w, top_idx, rank, counts = _merge(
        out_a, out_b, gates, h, w_branch_a[l].astype(BF16), w_branch_b[l].astype(BF16), w_out[l].astype(BF16),
        row(ln1_g[l]), row(ln1_b[l]), w_router[l].astype(F32), row(b_router[l]))
    dest, pend, block_exp, n_used, n_valid = _block_layout(top_idx, rank, counts[:, 0])
    x_sorted = _dispatch(dest.T, pend, h1p, block_exp.shape[0] * ROWS_MOE)
    y_buf = _moe(block_exp, n_used, n_valid, x_sorted, w_exp_in[l].astype(F32), b_exp_in[l].reshape(N_EXPERTS, 1, -1),
                 w_exp_out[l].astype(F32), b_exp_out[l].reshape(N_EXPERTS, 1, -1))
    n_steps = n // TM_COMB
    dest_km = dest.reshape(TOP_K, n_steps, TM_COMB).transpose(1, 0, 2).reshape(n_steps, 1, TOP_K * TM_COMB)
    out = _combine(dest_km, gate_w.T, h1, row(ln2_g[l]), row(ln2_b[l]), y_buf)
    return out.reshape(batch, seq, d)
```
